```python
import jax, jax.numpy as jnp
from jax import lax
import numpy as np

D_MODEL = 1024
BATCH = 2
SEQ = 16384
DEPTH = 1
DEC_BATCH = 8
DEC_SEQ = 16
PAST_LEN = 4096

CHUNK = 64
CONV_CH = 1024
CONV_WIDTH = 31
N_HEADS = 16
HEAD_DIM = 64
ATT_WIDTH = N_HEADS * HEAD_DIM
ATTN_SCALE = HEAD_DIM ** -0.5
Q_BLOCK = 128
KEY_BLOCK = 128
N_EXPERTS = 256
TOP_K = 8
N_GROUPS = 8
TOPK_GROUPS = 4
EXPERT_FF = 256
SHARED_FF = 256
ROUTED_SCALE = 2.5
EXPERT_BLOCK = 128
LN_EPS = 1e-5
DEEPNORM_ALPHA = (2 * DEPTH) ** 0.25
DEEPNORM_BETA = (8 * DEPTH) ** -0.25
PROJ_SPLITS = [CONV_CH, 2 * CONV_CH, 2 * CONV_CH + ATT_WIDTH, 2 * CONV_CH + 2 * ATT_WIDTH,
               2 * CONV_CH + 3 * ATT_WIDTH, 2 * CONV_CH + 3 * ATT_WIDTH + D_MODEL]
PROJ_WIDTH = 2 * CONV_CH + 3 * ATT_WIDTH + 2 * D_MODEL

kernel_name = "streaming_conv_stickbreak_moe_deepnorm"


def layer_norm(x, g, b):
    xf = x.astype(jnp.float32)
    mu = jnp.mean(xf, axis=-1, keepdims=True)
    var = jnp.mean(jnp.square(xf - mu), axis=-1, keepdims=True)
    return ((xf - mu) * lax.rsqrt(var + LN_EPS) * g + b).astype(x.dtype)


def causal_depthwise(u_ext, w, b):
    y = lax.conv_general_dilated(u_ext, w[:, None, :], window_strides=(1,), padding='VALID',
                                 dimension_numbers=('NWC', 'WIO', 'NWC'),
                                 feature_group_count=u_ext.shape[-1])
    return y + b


def stick_breaking(q, k, v, past_len):
    B, T, H, Dh = q.shape
    S = k.shape[1]
    n_kb = -(-S // KEY_BLOCK)
    pad = n_kb * KEY_BLOCK - S
    k = jnp.pad(k, ((0, 0), (0, pad), (0, 0), (0, 0)))
    v = jnp.pad(v, ((0, 0), (0, pad), (0, 0), (0, 0)))
    k_blocks = k.reshape(B, n_kb, KEY_BLOCK, H, Dh).transpose(1, 0, 2, 3, 4)
    v_blocks = v.reshape(B, n_kb, KEY_BLOCK, H, Dh).transpose(1, 0, 2, 3, 4)
    k_pos = jnp.arange(n_kb * KEY_BLOCK).reshape(n_kb, KEY_BLOCK)
    suffix = (jnp.arange(KEY_BLOCK)[:, None] >= jnp.arange(KEY_BLOCK)[None, :]).astype(jnp.float32)
    qb = T if T <= Q_BLOCK else Q_BLOCK
    outs = []
    for i in range(T // qb):
        q_blk = q[:, i * qb:(i + 1) * qb]
        p0 = past_len + i * qb
        p_last = p0 + qb - 1
        n_vis = max(1, min(n_kb, -(-p_last // KEY_BLOCK)))
        q_pos = p0 + jnp.arange(qb)

        def step(carry, xs, q_blk=q_blk, q_pos=q_pos):
            acc, log_later = carry
            k_c, v_c, kp = xs
            z = jnp.einsum('bqhd,bkhd->bhqk', q_blk, k_c, preferred_element_type=jnp.float32) * ATTN_SCALE
            visible = kp[None, :] < q_pos[:, None]
            log_keep = jnp.where(visible, jax.nn.log_sigmoid(-z), 0.0)
            in_block = jnp.einsum('bhqj,js->bhqs', log_keep, suffix)
            a = jnp.exp(jnp.where(visible, z + in_block + log_later[..., None], -jnp.inf))
            acc = acc + jnp.einsum('bhqk,bkhd->bhqd', a.astype(v_c.dtype), v_c,
                                   preferred_element_type=jnp.float32)
            return (acc, log_later + log_keep.sum(-1)), None

        init = (jnp.zeros((B, H, qb, Dh), jnp.float32), jnp.zeros((B, H, qb), jnp.float32))
        (acc, _), _ = lax.scan(step, init, (k_blocks[:n_vis], v_blocks[:n_vis], k_pos[:n_vis]), reverse=True)
        outs.append(acc.transpose(0, 2, 1, 3).astype(q.dtype))
    return jnp.concatenate(outs, axis=1)


def route(h, w_router, router_bias):
    T = h.shape[0]
    scores = jax.nn.sigmoid(jnp.dot(h, w_router, preferred_element_type=jnp.float32))
    choice = scores + router_bias.astype(jnp.float32)
    grp = choice.reshape(T, N_GROUPS, N_EXPERTS // N_GROUPS)
    grp_score = lax.top_k(grp, 2)[0].sum(-1)
    _, g_idx = lax.top_k(grp_score, TOPK_GROUPS)
    g_mask = jax.nn.one_hot(g_idx, N_GROUPS).sum(-2) > 0
    e_mask = jnp.repeat(g_mask, N_EXPERTS // N_GROUPS, axis=-1)
    _, idx = lax.top_k(jnp.where(e_mask, choice, -jnp.inf), TOP_K)
    w = jnp.take_along_axis(scores, idx, axis=-1)
    w = w / jnp.sum(w, axis=-1, keepdims=True) * ROUTED_SCALE
    return idx, w


def routed_experts(h, idx, wts, we_gate, we_up, we_down):
    T, D = h.shape
    n_assign = T * TOP_K
    n_blocks = -(-(n_assign + N_EXPERTS * (EXPERT_BLOCK - 1)) // EXPERT_BLOCK)
    n_rows = n_blocks * EXPERT_BLOCK
    flat_e = idx.reshape(-1)
    order = jnp.argsort(flat_e)
    sorted_e = flat_e[order]
    sorted_tok = (order // TOP_K).astype(jnp.int32)
    sorted_w = wts.reshape(-1)[order]
    counts = jnp.bincount(flat_e, length=N_EXPERTS)
    starts = jnp.cumsum(counts) - counts
    padded = (counts + EXPERT_BLOCK - 1) // EXPERT_BLOCK * EXPERT_BLOCK
    pad_end = jnp.cumsum(padded)
    pad_start = pad_end - padded
    dest = pad_start[sorted_e] + jnp.arange(n_assign) - starts[sorted_e]
    row_tok = jnp.full((n_rows,), T, jnp.int32).at[dest].set(sorted_tok)
    row_w = jnp.zeros((n_rows,), h.dtype).at[dest].set(sorted_w)
    block_e = jnp.minimum(jnp.searchsorted(pad_end, jnp.arange(n_blocks) * EXPERT_BLOCK, side='right'),
                          N_EXPERTS - 1)
    h_pad = jnp.concatenate([h, jnp.zeros((1, D), h.dtype)], axis=0)

    def expert_block(args):
        tok, e, w = args
        xb = h_pad[tok]
        hid = jax.nn.silu(xb @ we_gate[e]) * (xb @ we_up[e])
        return (hid @ we_down[e]) * w[:, None]

    out = lax.map(expert_block, (row_tok.reshape(n_blocks, EXPERT_BLOCK), block_e,
                                 row_w.reshape(n_blocks, EXPERT_BLOCK)))
    return jnp.zeros((T + 1, D), h.dtype).at[row_tok].add(out.reshape(n_rows, D))[:T]


def trunk_layer(x, conv_hist, k_past, v_past, w_in, b_in, conv_w, conv_b, conv_ln_g, conv_ln_b,
                w_conv_out, w_attn_out, w_out, ln1_g, ln1_b, w_router, router_bias,
                we_gate, we_up, we_down, ws_gate, ws_up, ws_down, ln2_g, ln2_b):
    B, T, D = x.shape
    proj = x @ w_in + b_in
    glu_a, glu_b, q, k, v, gate_c, gate_a = jnp.split(proj, PROJ_SPLITS, axis=-1)
    u = glu_a * jax.nn.sigmoid(glu_b)
    u_ext = jnp.concatenate([conv_hist, u], axis=1)
    c = causal_depthwise(u_ext, conv_w, conv_b)
    c = jax.nn.silu(layer_norm(c, conv_ln_g, conv_ln_b)) @ w_conv_out
    q = q.reshape(B, T, N_HEADS, HEAD_DIM)
    k = k.reshape(B, T, N_HEADS, HEAD_DIM)
    v = v.reshape(B, T, N_HEADS, HEAD_DIM)
    k_all = jnp.concatenate([k_past, k], axis=1)
    v_all = jnp.concatenate([v_past, v], axis=1)
    o = stick_breaking(q, k_all, v_all, k_past.shape[1]).reshape(B, T, ATT_WIDTH) @ w_attn_out
    mixed = (jax.nn.sigmoid(gate_c) * c + jax.nn.sigmoid(gate_a) * o) @ w_out
    x1 = layer_norm(DEEPNORM_ALPHA * x + mixed, ln1_g, ln1_b)
    h = x1.reshape(B * T, D)
    idx, wts = route(h, w_router, router_bias)
    shared = (jax.nn.silu(h @ ws_gate) * (h @ ws_up)) @ ws_down
    f = (shared + routed_experts(h, idx, wts.astype(h.dtype), we_gate, we_up, we_down)).reshape(B, T, D)
    x2 = layer_norm(DEEPNORM_ALPHA * x1 + f, ln2_g, ln2_b)
    return x2, k, v, u_ext[:, -(CONV_WIDTH - 1):]


def setup_inputs(seed: int = 0) -> dict:
    key = jax.random.key(seed)
    ks = jax.random.split(key, 28)
    L = DEPTH

    def nrm(i, shape, scale):
        return jax.random.normal(ks[i], shape, jnp.float32) * scale

    v_lo, v_hi = PROJ_SPLITS[3], PROJ_SPLITS[4]
    col_scale = jnp.ones((PROJ_WIDTH,), jnp.float32).at[v_lo:v_hi].set(DEEPNORM_BETA)
    return {
        "x_prompt": nrm(0, (BATCH, SEQ, D_MODEL), 1.0),
        "x_sample": nrm(1, (DEC_BATCH, DEC_SEQ, D_MODEL), 1.0),
        "cache_k": nrm(2, (L, DEC_BATCH, PAST_LEN, N_HEADS, HEAD_DIM), 1.0),
        "cache_v": nrm(3, (L, DEC_BATCH, PAST_LEN, N_HEADS, HEAD_DIM), DEEPNORM_BETA),
        "state_conv": nrm(4, (L, DEC_BATCH, CONV_WIDTH - 1, CONV_CH), 0.5),
        "ln_in_g": 1.0 + nrm(5, (D_MODEL,), 0.02),
        "ln_in_b": nrm(6, (D_MODEL,), 0.02),
        "w_in": nrm(7, (L, D_MODEL, PROJ_WIDTH), D_MODEL ** -0.5) * col_scale,
        "b_in": nrm(8, (L, PROJ_WIDTH), 0.02),
        "conv_w": nrm(9, (L, CONV_WIDTH, CONV_CH), CONV_WIDTH ** -0.5),
        "conv_b": nrm(10, (L, CONV_CH), 0.02),
        "conv_ln_g": 1.0 + nrm(11, (L, CONV_CH), 0.02),
        "conv_ln_b": nrm(12, (L, CONV_CH), 0.02),
        "w_conv_out": nrm(13, (L, CONV_CH, D_MODEL), CONV_CH ** -0.5 * DEEPNORM_BETA),
        "w_attn_out": nrm(14, (L, ATT_WIDTH, D_MODEL), ATT_WIDTH ** -0.5 * DEEPNORM_BETA),
        "w_out": nrm(15, (L, D_MODEL, D_MODEL), D_MODEL ** -0.5 * DEEPNORM_BETA),
        "ln1_g": 1.0 + nrm(16, (L, D_MODEL), 0.02),
        "ln1_b": nrm(17, (L, D_MODEL), 0.02),
        "w_router": nrm(18, (L, D_MODEL, N_EXPERTS), D_MODEL ** -0.5),
        "router_bias": nrm(19, (L, N_EXPERTS), 0.01),
        "we_gate": nrm(20, (L, N_EXPERTS, D_MODEL, EXPERT_FF), D_MODEL ** -0.5),
        "we_up": nrm(21, (L, N_EXPERTS, D_MODEL, EXPERT_FF), D_MODEL ** -0.5),
        "we_down": nrm(22, (L, N_EXPERTS, EXPERT_FF, D_MODEL), EXPERT_FF ** -0.5 * DEEPNORM_BETA),
        "ws_gate": nrm(23, (L, D_MODEL, SHARED_FF), D_MODEL ** -0.5),
        "ws_up": nrm(24, (L, D_MODEL, SHARED_FF), D_MODEL ** -0.5),
        "ws_down": nrm(25, (L, SHARED_FF, D_MODEL), SHARED_FF ** -0.5 * DEEPNORM_BETA),
        "ln2_g": 1.0 + nrm(26, (L, D_MODEL), 0.02),
        "ln2_b": nrm(27, (L, D_MODEL), 0.02),
    }


def reference(x_prompt, x_sample, cache_k, cache_v, state_conv, ln_in_g, ln_in_b, w_in, b_in,
              conv_w, conv_b, conv_ln_g, conv_ln_b, w_conv_out, w_attn_out, w_out, ln1_g, ln1_b,
              w_router, router_bias, we_gate, we_up, we_down, ws_gate, ws_up, ws_down, ln2_g, ln2_b):
    hp = layer_norm(x_prompt, ln_in_g, ln_in_b)
    hs = layer_norm(x_sample, ln_in_g, ln_in_b)
    B = hp.shape[0]
    kp, vp, cp, ksm, vsm, csm = [], [], [], [], [], []
    for l in range(DEPTH):
        lw = (w_in[l], b_in[l], conv_w[l], conv_b[l], conv_ln_g[l], conv_ln_b[l], w_conv_out[l],
              w_attn_out[l], w_out[l], ln1_g[l], ln1_b[l], w_router[l], router_bias[l],
              we_gate[l], we_up[l], we_down[l], ws_gate[l], ws_up[l], ws_down[l], ln2_g[l], ln2_b[l])
        zero_hist = jnp.zeros((B, CONV_WIDTH - 1, CONV_CH), hp.dtype)
        no_past = jnp.zeros((B, 0, N_HEADS, HEAD_DIM), hp.dtype)
        hp, k_p, v_p, c_p = trunk_layer(hp, zero_hist, no_past, no_past, *lw)
        hs, k_s, v_s, c_s = trunk_layer(hs, state_conv[l], cache_k[l], cache_v[l], *lw)
        kp.append(k_p); vp.append(v_p); cp.append(c_p)
        ksm.append(k_s); vsm.append(v_s); csm.append(c_s)
    return (hp, hs, jnp.stack(kp), jnp.stack(vp), jnp.stack(cp), jnp.stack(ksm), jnp.stack(vsm), jnp.stack(csm))
```

```python
import functools

import jax
import jax.numpy as jnp
from jax import lax
from jax.experimental import pallas as pl
from jax.experimental.pallas import tpu as pltpu

F32 = jnp.float32
BF16 = jnp.bfloat16

N_HEADS = 16
HEAD_DIM = 64
CONV_WIDTH = 31
N_EXPERTS = 256
TOP_K = 8
N_GROUPS = 8
TOPK_GROUPS = 4
GROUP_SIZE = N_EXPERTS // N_GROUPS
ROUTED_SCALE = 2.5
LN_EPS = 1e-5
DEPTH = 1
DEEPNORM_ALPHA = (2 * DEPTH) ** 0.25
ATTN_SCALE = HEAD_DIM ** -0.5

LANES = 128
SUBLANES = 8
VMEM_LIMIT_BYTES = 56 * 1024 * 1024

KEY_BLOCK = 128
HIST_PAD = 32
EXPERT_ROWS = 512
GATHER_CHUNK = 128
LOG_DEAD = -110.0


def _layer_norm(x, g, b):
    mu = jnp.mean(x, axis=-1, keepdims=True)
    xc = x - mu
    var = jnp.mean(xc * xc, axis=-1, keepdims=True)
    return xc * lax.rsqrt(var + LN_EPS) * g + b


def _sigmoid(x):
    return 1.0 / (1.0 + jnp.exp(-x))


def _silu(x):
    return x * _sigmoid(x)


def _dot(a, b):
    return jnp.dot(a, b, preferred_element_type=F32)


def _dot_nt(a, b):
    return lax.dot_general(a, b, (((1,), (1,)), ((), ())), preferred_element_type=F32)


def _params(sem):
    return pltpu.CompilerParams(dimension_semantics=sem, vmem_limit_bytes=VMEM_LIMIT_BYTES)


def _proj_conv_body(x_ref, hist_ref, lng_ref, lnb_ref, win_ref, bin_ref, cw_ref, cb_ref, clg_ref,
                    clb_ref, wco_ref, q_ref, k_ref, v_ref, gc_ref, sa_ref, cs_ref, ubuf, cbuf):
    tm = x_ref.shape[1]
    d = x_ref.shape[2]
    hist = CONV_WIDTH - 1
    lead = HIST_PAD - hist

    @pl.when(pl.program_id(1) == 0)
    def _():
        ubuf[0:lead, :] = jnp.zeros((lead, d), F32)
        ubuf[lead:HIST_PAD, :] = hist_ref[0]

    hb = _layer_norm(x_ref[0], lng_ref[...], lnb_ref[...]).astype(BF16)

    def proj(i):
        return _dot(hb, win_ref[:, i * d:(i + 1) * d]) + bin_ref[:, i * d:(i + 1) * d]

    ubuf[HIST_PAD:HIST_PAD + tm, :] = proj(0) * _sigmoid(proj(1))
    q_ref[0] = (proj(2) * ATTN_SCALE).astype(BF16)
    k_ref[0] = proj(3)
    v_ref[0] = proj(4)
    sa_ref[0] = _sigmoid(proj(6)).astype(BF16)

    rows = min(tm, 32)
    for r0 in range(0, tm, rows):
        acc = jnp.broadcast_to(cb_ref[...], (rows, d))
        for kk in range(CONV_WIDTH):
            acc = acc + cw_ref[kk:kk + 1, :] * ubuf[lead + kk + r0:lead + kk + r0 + rows, :]
        cbuf[r0:r0 + rows, :] = _silu(_layer_norm(acc, clg_ref[...], clb_ref[...])).astype(BF16)
    gc_ref[0] = (_sigmoid(proj(5)) * _dot(cbuf[...], wco_ref[...])).astype(BF16)

    cs_ref[0] = ubuf[tm + lead:tm + HIST_PAD, :]
    ubuf[0:HIST_PAD, :] = ubuf[tm:tm + HIST_PAD, :]


def _proj_conv(x, hist, ln_g, ln_b, w_in, b_in, conv_w, conv_b, cln_g, cln_b, w_co, tm):
    bsz, t, d = x.shape
    pw = w_in.shape[1]
    const = lambda shape: pl.BlockSpec(shape, lambda b, i: (0,) * len(shape))
    tile = lambda: pl.BlockSpec((1, tm, d), lambda b, i: (b, i, 0))
    return pl.pallas_call(
        _proj_conv_body,
        grid=(bsz, t // tm),
        in_specs=[
            tile(),
            pl.BlockSpec((1, CONV_WIDTH - 1, d), lambda b, i: (b, 0, 0)),
            const((1, d)), const((1, d)), const((d, pw)), const((1, pw)),
            const((CONV_WIDTH, d)), const((1, d)), const((1, d)), const((1, d)), const((d, d)),
        ],
        out_specs=[tile(), tile(), tile(), tile(), tile(),
                   pl.BlockSpec((1, CONV_WIDTH - 1, d), lambda b, i: (b, 0, 0))],
        out_shape=[
            jax.ShapeDtypeStruct((bsz, t, d), BF16),
            jax.ShapeDtypeStruct((bsz, t, d), F32),
            jax.ShapeDtypeStruct((bsz, t, d), F32),
            jax.ShapeDtypeStruct((bsz, t, d), BF16),
            jax.ShapeDtypeStruct((bsz, t, d), BF16),
            jax.ShapeDtypeStruct((bsz, CONV_WIDTH - 1, d), F32),
        ],
        scratch_shapes=[pltpu.VMEM((HIST_PAD + tm, d), F32), pltpu.VMEM((tm, d), BF16)],
        compiler_params=_params(("arbitrary", "arbitrary")),
        name="proj_conv",
    )(x, hist, ln_g, ln_b, w_in, b_in, conv_w, conv_b, cln_g, cln_b, w_co)


def _sb_block(qs, kb, vb, acc, ll, vis):
    kbn = kb.shape[0]
    z = _dot_nt(qs, kb)
    lk = -(jnp.maximum(z, 0.0) + jnp.log1p(jnp.exp(-jnp.abs(z))))
    if vis is not None:
        lk = jnp.where(vis, lk, 0.0)
    row = lax.broadcasted_iota(jnp.int32, (kbn, kbn), 0)
    col = lax.broadcasted_iota(jnp.int32, (kbn, kbn), 1)
    suffix = jnp.where(row >= col, 1.0, 0.0).astype(BF16)
    hi = lk.astype(BF16)
    lo = (lk - hi.astype(F32)).astype(BF16)
    inb = _dot(hi, suffix) + _dot(lo, suffix)
    arg = z + inb + ll
    if vis is not None:
        arg = jnp.where(vis, arg, -jnp.inf)
    acc = acc + _dot(jnp.exp(arg).astype(BF16), vb)
    return acc, ll + inb[:, 0:1]


def _sb_older_blocks(qs, k_ref, v_ref, first, acc, ll):
    def live(ll_):
        return (jnp.max(ll_, axis=0, keepdims=True)[0, 0] > LOG_DEAD).astype(jnp.int32)

    def cond(c):
        return jnp.logical_and(c[0] >= 0, c[3] > 0)

    def body(c):
        j, acc_, ll_, _ = c
        start = pl.multiple_of(j * KEY_BLOCK, KEY_BLOCK)
        kb = k_ref[0, pl.ds(start, KEY_BLOCK), :].astype(BF16)
        vb = v_ref[0, pl.ds(start, KEY_BLOCK), :].astype(BF16)
        acc_, ll_ = _sb_block(qs, kb, vb, acc_, ll_, None)
        return j - 1, acc_, ll_, live(ll_)

    _, acc, _, _ = lax.while_loop(cond, body, (first, acc, ll, live(ll)))
    return acc


def _stack_heads(q):
    lane = lax.broadcasted_iota(jnp.int32, q.shape, 1)
    zero = jnp.zeros_like(q)
    return jnp.concatenate([jnp.where(lane < HEAD_DIM, q, zero), jnp.where(lane >= HEAD_DIM, q, zero)], axis=0)


def _row_in_head(rows, cols):
    r = lax.broadcasted_iota(jnp.int32, (rows, cols), 0)
    return jnp.where(r >= rows // 2, r - rows // 2, r)


def _unstack_heads(acc):
    r = acc.shape[0] // 2
    lane = lax.broadcasted_iota(jnp.int32, (r, LANES), 1)
    return jnp.where(lane < HEAD_DIM, acc[:r], acc[r:])


def _attn_prompt_body(q_ref, k_ref, v_ref, o_ref):
    tq = q_ref.shape[1]
    step = pl.program_id(2)
    row = _row_in_head(2 * KEY_BLOCK, KEY_BLOCK)
    col = lax.broadcasted_iota(jnp.int32, (2 * KEY_BLOCK, KEY_BLOCK), 1)
    vis = col < row
    for qi in range(tq // KEY_BLOCK):
        g = step * (tq // KEY_BLOCK) + qi
        qs = _stack_heads(q_ref[0, qi * KEY_BLOCK:(qi + 1) * KEY_BLOCK, :])
        start = pl.multiple_of(g * KEY_BLOCK, KEY_BLOCK)
        kb = k_ref[0, pl.ds(start, KEY_BLOCK), :].astype(BF16)
        vb = v_ref[0, pl.ds(start, KEY_BLOCK), :].astype(BF16)
        acc = jnp.zeros((2 * KEY_BLOCK, LANES), F32)
        ll = jnp.zeros((2 * KEY_BLOCK, 1), F32)
        acc, ll = _sb_block(qs, kb, vb, acc, ll, vis)
        acc = _sb_older_blocks(qs, k_ref, v_ref, g - 1, acc, ll)
        o_ref[0, qi * KEY_BLOCK:(qi + 1) * KEY_BLOCK, :] = _unstack_heads(acc).astype(BF16)


def _attn_prompt(q, k, v, tq):
    bsz, t, d = q.shape
    pairs = d // LANES
    return pl.pallas_call(
        _attn_prompt_body,
        grid=(bsz, pairs, t // tq),
        in_specs=[
            pl.BlockSpec((1, tq, LANES), lambda b, p, i: (b, i, p)),
            pl.BlockSpec((1, t, LANES), lambda b, p, i: (b, 0, p)),
            pl.BlockSpec((1, t, LANES), lambda b, p, i: (b, 0, p)),
        ],
        out_specs=pl.BlockSpec((1, tq, LANES), lambda b, p, i: (b, i, p)),
        out_shape=jax.ShapeDtypeStruct((bsz, t, d), BF16),
        compiler_params=_params(("arbitrary", "arbitrary", "arbitrary")),
        name="attn_prompt",
    )(q, k, v)


def _attn_sample_body(q_ref, kn_ref, vn_ref, ck_ref, cv_ref, o_ref):
    tq = q_ref.shape[1]
    past_blocks = ck_ref.shape[1] // KEY_BLOCK
    qs = _stack_heads(q_ref[0])
    row = _row_in_head(2 * tq, KEY_BLOCK)
    col = lax.broadcasted_iota(jnp.int32, (2 * tq, KEY_BLOCK), 1)
    acc = jnp.zeros((2 * tq, LANES), F32)
    ll = jnp.zeros((2 * tq, 1), F32)
    acc, ll = _sb_block(qs, kn_ref[0].astype(BF16), vn_ref[0].astype(BF16), acc, ll, col < row)
    acc = _sb_older_blocks(qs, ck_ref, cv_ref, past_blocks - 1, acc, ll)
    o_ref[0] = _unstack_heads(acc).astype(BF16)


def _attn_sample(q, k_new, v_new, cache_k, cache_v):
    bsz, t, d = q.shape
    past = cache_k.shape[1]
    pairs = d // LANES
    return pl.pallas_call(
        _attn_sample_body,
        grid=(bsz, pairs),
        in_specs=[
            pl.BlockSpec((1, t, LANES), lambda b, p: (b, 0, p)),
            pl.BlockSpec((1, KEY_BLOCK, LANES), lambda b, p: (b, 0, p)),
            pl.BlockSpec((1, KEY_BLOCK, LANES), lambda b, p: (b, 0, p)),
            pl.BlockSpec((1, past, LANES), lambda b, p: (b, 0, p)),
            pl.BlockSpec((1, past, LANES), lambda b, p: (b, 0, p)),
        ],
        out_specs=pl.BlockSpec((1, t, LANES), lambda b, p: (b, 0, p)),
        out_shape=jax.ShapeDtypeStruct((bsz, t, d), BF16),
        compiler_params=_params(("arbitrary", "arbitrary")),
        name="attn_sample",
    )(q, k_new, v_new, cache_k, cache_v)


def _first_argmax(cur, iota_f, n):
    m = jnp.max(cur, axis=0, keepdims=True)
    i = jnp.min(jnp.where(cur == m, iota_f, float(n)), axis=0, keepdims=True)
    return m, i


def _post_attn_body(o_ref, gc_ref, sa_ref, x_ref, cnt_ref, lng_ref, lnb_ref, wao_ref, wo_ref, l1g_ref,
                    l1b_ref, wrh_ref, wrl_ref, rb_ref, wsgu_ref, wsd_ref,
                    x1_ref, part_ref, idx_ref, wts_ref, rank_ref, cnt_out_ref, run):
    tm = x_ref.shape[0]
    ff = wsd_ref.shape[0]

    @pl.when(pl.program_id(0) == 0)
    def _():
        run[...] = cnt_ref[...]

    h = _layer_norm(x_ref[...], lng_ref[...], lnb_ref[...])
    att = _dot(o_ref[...], wao_ref[...])
    merged = gc_ref[...].astype(F32) + sa_ref[...].astype(F32) * att
    mixed = _dot(merged.astype(BF16), wo_ref[...])
    x1 = _layer_norm(DEEPNORM_ALPHA * h + mixed, l1g_ref[...], l1b_ref[...])
    x1_ref[...] = x1

    x1h = x1.astype(BF16)
    gu = _dot(x1h, wsgu_ref[...])
    shared = _dot((_silu(gu[:, :ff]) * gu[:, ff:]).astype(BF16), wsd_ref[...])
    part_ref[...] = DEEPNORM_ALPHA * x1 + shared

    x1l = (x1 - x1h.astype(F32)).astype(BF16)
    logits = _dot_nt(wrh_ref[...], x1h) + _dot_nt(wrl_ref[...], x1h) + _dot_nt(wrh_ref[...], x1l)
    scores = _sigmoid(logits)
    choice = scores + rb_ref[...]
    neg = -jnp.inf

    giota = lax.broadcasted_iota(jnp.int32, (GROUP_SIZE, tm), 0).astype(F32)
    gs = []
    for g in range(N_GROUPS):
        blk = choice[g * GROUP_SIZE:(g + 1) * GROUP_SIZE, :]
        m1, i1 = _first_argmax(blk, giota, GROUP_SIZE)
        m2 = jnp.max(jnp.where(giota == i1, neg, blk), axis=0, keepdims=True)
        gs.append(m1 + m2)
    gscore = jnp.concatenate(gs, axis=0)

    g8 = lax.broadcasted_iota(jnp.int32, (N_GROUPS, tm), 0).astype(F32)
    gsel = jnp.zeros((N_GROUPS, tm), F32)
    cur = gscore
    for _ in range(TOPK_GROUPS):
        _, i = _first_argmax(cur, g8, N_GROUPS)
        hit = g8 == i
        gsel = jnp.where(hit, 1.0, gsel)
        cur = jnp.where(hit, neg, cur)
    emask = jnp.concatenate(
        [jnp.broadcast_to(gsel[g:g + 1, :], (GROUP_SIZE, tm)) for g in range(N_GROUPS)], axis=0)

    eiota = lax.broadcasted_iota(jnp.int32, (N_EXPERTS, tm), 0).astype(F32)
    cur = jnp.where(emask > 0.0, choice, neg)
    hits, sel_w = [], []
    for r in range(TOP_K):
        _, i = _first_argmax(cur, eiota, N_EXPERTS)
        hit = eiota == i
        hits.append(hit)
        sel_w.append(jnp.sum(jnp.where(hit, scores, 0.0), axis=0, keepdims=True))
        cur = jnp.where(hit, neg, cur)
        idx_ref[r:r + 1, :] = i.astype(jnp.int32)
    wsum = sel_w[0]
    for r in range(1, TOP_K):
        wsum = wsum + sel_w[r]
    for r in range(TOP_K):
        wts_ref[r:r + 1, :] = sel_w[r] / wsum * ROUTED_SCALE

    picked = jnp.zeros((N_EXPERTS, tm), F32)
    for r in range(TOP_K):
        picked = jnp.where(hits[r], 1.0, picked)
    picked_b = picked.astype(BF16)
    trow = lax.broadcasted_iota(jnp.int32, (tm, tm), 0)
    tcol = lax.broadcasted_iota(jnp.int32, (tm, tm), 1)
    earlier = jnp.where(trow < tcol, 1.0, 0.0).astype(BF16)
    before = _dot(picked_b, earlier) + run[:, 0:1]
    for r in range(TOP_K):
        rank_ref[r:r + 1, :] = jnp.sum(jnp.where(hits[r], before, 0.0), axis=0, keepdims=True).astype(jnp.int32)
    run[...] = run[...] + _dot(picked_b, jnp.ones((tm, LANES), BF16))
    cnt_out_ref[...] = run[...]


def _post_attn(o, gc, sa, x, cnt_in, ln_g, ln_b, w_ao, w_o, l1g, l1b, wr_hi, wr_lo, r_bias, ws_gu, ws_d, tm):
    n, d = x.shape
    ff = ws_d.shape[0]
    const = lambda shape: pl.BlockSpec(shape, lambda i: (0,) * len(shape))
    tile = lambda: pl.BlockSpec((tm, d), lambda i: (i, 0))
    small = lambda: pl.BlockSpec((TOP_K, tm), lambda i: (0, i))
    return pl.pallas_call(
        _post_attn_body,
        grid=(n // tm,),
        in_specs=[tile(), tile(), tile(), tile(), const((N_EXPERTS, LANES)),
                  const((1, d)), const((1, d)), const((d, d)), const((d, d)), const((1, d)), const((1, d)),
                  const((N_EXPERTS, d)), const((N_EXPERTS, d)), const((N_EXPERTS, 1)),
                  const((d, 2 * ff)), const((ff, d))],
        out_specs=[tile(), tile(), small(), small(), small(), const((N_EXPERTS, LANES))],
        out_shape=[
            jax.ShapeDtypeStruct((n, d), F32),
            jax.ShapeDtypeStruct((n, d), F32),
            jax.ShapeDtypeStruct((TOP_K, n), jnp.int32),
            jax.ShapeDtypeStruct((TOP_K, n), F32),
            jax.ShapeDtypeStruct((TOP_K, n), jnp.int32),
            jax.ShapeDtypeStruct((N_EXPERTS, LANES), F32),
        ],
        scratch_shapes=[pltpu.VMEM((N_EXPERTS, LANES), F32)],
        compiler_params=_params(("arbitrary",)),
        name="post_attn",
    )(o, gc, sa, x, cnt_in, ln_g, ln_b, w_ao, w_o, l1g, l1b, wr_hi, wr_lo, r_bias, ws_gu, ws_d)


def _experts_body(be_ref, bc_ref, rt_hbm, x_hbm, wg_ref, wu_ref, wd_ref, ys_ref,
                  idx_smem, xbuf, isem, gsem, wgu_bf, wd_bf):
    b = pl.program_id(0)
    nb = pl.num_programs(0)
    slot = b % 2
    ff = wd_ref.shape[1]

    def idx_copy(blk, s):
        return pltpu.make_async_copy(rt_hbm.at[pl.ds(blk, 1)], idx_smem.at[pl.ds(s, 1)], isem.at[s])

    def chunk_copy(s, c):
        rows = pl.ds(c * GATHER_CHUNK, GATHER_CHUNK)
        return pltpu.make_async_copy(x_hbm.at[pl.ds(0, GATHER_CHUNK)], xbuf.at[s, rows], gsem.at[s])

    def issue_rows(blk, s):
        def body(r, carry):
            tok = idx_smem[s, r]
            pltpu.make_async_copy(x_hbm.at[pl.ds(tok, 1)], xbuf.at[s, pl.ds(r, 1)], gsem.at[s]).start()
            return carry
        lax.fori_loop(0, bc_ref[blk] * GATHER_CHUNK, body, 0)

    def wait_rows(blk, s):
        def body(c, carry):
            chunk_copy(s, c).wait()
            return carry
        lax.fori_loop(0, bc_ref[blk], body, 0)

    @pl.when(b == 0)
    def _():
        xbuf[...] = jnp.zeros(xbuf.shape, F32)
        first = idx_copy(0, 0)
        first.start()
        first.wait()
        issue_rows(0, 0)

        @pl.when(nb > 1)
        def _():
            idx_copy(1, 1).start()

    @pl.when(b + 1 < nb)
    def _():
        idx_copy(b + 1, 1 - slot).wait()
        issue_rows(b + 1, 1 - slot)

    wait_rows(b, slot)

    @pl.when(b + 2 < nb)
    def _():
        idx_copy(b + 2, slot).start()

    e_now = be_ref[b]
    e_prev = be_ref[jnp.maximum(b - 1, 0)]

    @pl.when(jnp.logical_or(b == 0, e_now != e_prev))
    def _():
        wgu_bf[:, 0:ff] = wg_ref[0].astype(BF16)
        wgu_bf[:, ff:2 * ff] = wu_ref[0].astype(BF16)
        wd_bf[...] = wd_ref[0].astype(BF16)

    @pl.when(bc_ref[b] > 0)
    def _():
        xb = xbuf[slot].astype(BF16)
        gu = _dot(xb, wgu_bf[...])
        hid = _silu(gu[:, :ff]) * gu[:, ff:]
        ys_ref[...] = _dot(hid.astype(BF16), wd_bf[...])

    @pl.when(bc_ref[b] == 0)
    def _():
        ys_ref[...] = jnp.zeros(ys_ref.shape, F32)


def _experts(block_e, block_chunks, row_tok, x1, we_gate, we_up, we_down):
    nb = block_e.shape[0]
    d = x1.shape[1]
    ff = we_gate.shape[2]
    grid_spec = pltpu.PrefetchScalarGridSpec(
        num_scalar_prefetch=2,
        grid=(nb,),
        in_specs=[
            pl.BlockSpec(memory_space=pl.ANY),
            pl.BlockSpec(memory_space=pl.ANY),
            pl.BlockSpec((1, d, ff), lambda b, be, bc: (be[b], 0, 0)),
            pl.BlockSpec((1, d, ff), lambda b, be, bc: (be[b], 0, 0)),
            pl.BlockSpec((1, ff, d), lambda b, be, bc: (be[b], 0, 0)),
        ],
        out_specs=pl.BlockSpec((EXPERT_ROWS, d), lambda b, be, bc: (b, 0)),
        scratch_shapes=[
            pltpu.SMEM((2, EXPERT_ROWS), jnp.int32),
            pltpu.VMEM((2, EXPERT_ROWS, d), F32),
            pltpu.SemaphoreType.DMA((2,)),
            pltpu.SemaphoreType.DMA((2,)),
            pltpu.VMEM((d, 2 * ff), BF16),
            pltpu.VMEM((ff, d), BF16),
        ],
    )
    return pl.pallas_call(
        _experts_body,
        grid_spec=grid_spec,
        out_shape=jax.ShapeDtypeStruct((nb * EXPERT_ROWS, d), F32),
        compiler_params=_params(("arbitrary",)),
        name="experts",
    )(block_e, block_chunks, row_tok, x1, we_gate, we_up, we_down)


def _combine_body(dt_hbm, ys_hbm, part_ref, w_ref, g_ref, b_ref, y_ref, idx_smem, gbuf, isem, gsem):
    i = pl.program_id(0)
    n = pl.num_programs(0)
    slot = i % 2
    tm = part_ref.shape[0]

    def idx_copy(tile, s):
        return pltpu.make_async_copy(dt_hbm.at[pl.ds(tile, 1)], idx_smem.at[pl.ds(s, 1)], isem.at[s])

    def issue_rows(s):
        for j in range(TOP_K):
            def body(t, carry):
                row = idx_smem[s, j * tm + t]
                pltpu.make_async_copy(ys_hbm.at[pl.ds(row, 1)], gbuf.at[s, j, pl.ds(t, 1)], gsem.at[s]).start()
                return carry
            lax.fori_loop(0, tm, body, 0)

    def wait_rows(s):
        for j in range(TOP_K):
            pltpu.make_async_copy(ys_hbm.at[pl.ds(0, tm)], gbuf.at[s, j], gsem.at[s]).wait()

    @pl.when(i == 0)
    def _():
        first = idx_copy(0, 0)
        first.start()
        first.wait()
        issue_rows(0)

        @pl.when(n > 1)
        def _():
            idx_copy(1, 1).start()

    @pl.when(i + 1 < n)
    def _():
        idx_copy(i + 1, 1 - slot).wait()
        issue_rows(1 - slot)

    wait_rows(slot)

    @pl.when(i + 2 < n)
    def _():
        idx_copy(i + 2, slot).start()

    f = part_ref[...]
    for j in range(TOP_K):
        f = f + w_ref[:, j:j + 1] * gbuf[slot, j]
    y_ref[...] = _layer_norm(f, g_ref[...], b_ref[...])


def _combine(dest_tiles, ys, part, wts, ln_g, ln_b, tm):
    n, d = part.shape
    return pl.pallas_call(
        _combine_body,
        grid=(n // tm,),
        in_specs=[
            pl.BlockSpec(memory_space=pl.ANY),
            pl.BlockSpec(memory_space=pl.ANY),
            pl.BlockSpec((tm, d), lambda i: (i, 0)),
            pl.BlockSpec((tm, TOP_K), lambda i: (i, 0)),
            pl.BlockSpec((1, d), lambda i: (0, 0)),
            pl.BlockSpec((1, d), lambda i: (0, 0)),
        ],
        out_specs=pl.BlockSpec((tm, d), lambda i: (i, 0)),
        out_shape=jax.ShapeDtypeStruct((n, d), F32),
        scratch_shapes=[
            pltpu.SMEM((2, TOP_K * tm), jnp.int32),
            pltpu.VMEM((2, TOP_K, tm, d), F32),
            pltpu.SemaphoreType.DMA((2,)),
            pltpu.SemaphoreType.DMA((2,)),
        ],
        compiler_params=_params(("arbitrary",)),
        name="combine",
    )(dest_tiles, ys, part, wts, ln_g, ln_b)


def _dest_tiles(dest, tm):
    k, n = dest.shape
    return dest.reshape(k, n // tm, tm).transpose(1, 0, 2).reshape(n // tm, k * tm)


def kernel(x_prompt, x_sample, cache_k, cache_v, state_conv, ln_in_g, ln_in_b, w_in, b_in, conv_w, conv_b,
           conv_ln_g, conv_ln_b, w_conv_out, w_attn_out, w_out, ln1_g, ln1_b, w_router, router_bias,
           we_gate, we_up, we_down, ws_gate, ws_up, ws_down, ln2_g, ln2_b):
    bp, tp, d = x_prompt.shape
    bs, ts, _ = x_sample.shape
    past = cache_k.shape[2]
    assert w_in.shape[0] == DEPTH and d == N_HEADS * HEAD_DIM and conv_w.shape[2] == d
    assert tp % 512 == 0 and ts <= KEY_BLOCK and ts % 16 == 0 and past % KEY_BLOCK == 0
    n_p, n_s = bp * tp, bs * ts
    assert n_s % LANES == 0

    row = lambda a: a.reshape(1, -1)
    lng, lnb = row(ln_in_g), row(ln_in_b)
    w_in_b = w_in[0].astype(BF16)
    w_co_b = w_conv_out[0].astype(BF16)
    conv_args = (lng, lnb, w_in_b, b_in, conv_w[0], conv_b, conv_ln_g, conv_ln_b, w_co_b)

    qp, kp, vp, gcp, sap, csp = _proj_conv(x_prompt, jnp.zeros((bp, CONV_WIDTH - 1, d), F32), *conv_args, tm=256)
    op = _attn_prompt(qp, kp, vp, tq=512)

    qs, ks, vs, gcs, sas, css = _proj_conv(x_sample, state_conv[0], *conv_args, tm=ts)
    pad_new = lambda a: jnp.pad(a, ((0, 0), (0, KEY_BLOCK - ts), (0, 0)))
    os_ = _attn_sample(qs, pad_new(ks), pad_new(vs), cache_k[0].reshape(bs, past, d), cache_v[0].reshape(bs, past, d))

    wr = w_router[0].T
    wr_hi = wr.astype(BF16)
    wr_lo = (wr - wr_hi.astype(F32)).astype(BF16)
    post_args = (lng, lnb, w_attn_out[0].astype(BF16), w_out[0].astype(BF16), ln1_g, ln1_b, wr_hi, wr_lo,
                 router_bias.reshape(N_EXPERTS, 1),
                 jnp.concatenate([ws_gate[0], ws_up[0]], axis=1).astype(BF16), ws_down[0].astype(BF16))
    flat = lambda a, n: a.reshape(n, d)
    zero_cnt = jnp.zeros((N_EXPERTS, LANES), F32)
    x1p, partp, idxp, wtp, rankp, cnt_p = _post_attn(
        flat(op, n_p), flat(gcp, n_p), flat(sap, n_p), flat(x_prompt, n_p), zero_cnt, *post_args, tm=256)
    x1s, parts, idxs, wts, ranks, cnt = _post_attn(
        flat(os_, n_s), flat(gcs, n_s), flat(sas, n_s), flat(x_sample, n_s), cnt_p, *post_args, tm=n_s)

    n_tok = n_p + n_s
    counts = cnt[:, 0].astype(jnp.int32)
    padded = (counts + GATHER_CHUNK - 1) // GATHER_CHUNK * GATHER_CHUNK
    rows_e = (counts + EXPERT_ROWS - 1) // EXPERT_ROWS * EXPERT_ROWS
    row_end = jnp.cumsum(rows_e)
    row_start = row_end - rows_e
    n_blocks = (n_tok * TOP_K + N_EXPERTS * (EXPERT_ROWS - 1) + EXPERT_ROWS - 1) // EXPERT_ROWS
    blk_row0 = jnp.arange(n_blocks, dtype=jnp.int32) * EXPERT_ROWS
    block_e = jnp.minimum(jnp.searchsorted(row_end, blk_row0, side='right'), N_EXPERTS - 1).astype(jnp.int32)
    left = padded[block_e] - (blk_row0 - row_start[block_e])
    block_chunks = (jnp.clip(left, 0, EXPERT_ROWS) // GATHER_CHUNK).astype(jnp.int32)

    idx_all = jnp.concatenate([idxp, idxs], axis=1)
    rank_all = jnp.concatenate([rankp, ranks], axis=1)
    dest = row_start[idx_all] + rank_all
    tok = jnp.broadcast_to(jnp.arange(n_tok, dtype=jnp.int32)[None, :], dest.shape)
    row_tok = jnp.zeros((n_blocks * EXPERT_ROWS,), jnp.int32).at[dest.reshape(-1)].set(tok.reshape(-1))

    x1_all = jnp.concatenate([x1p, x1s], axis=0)
    ys = _experts(block_e, block_chunks, row_tok.reshape(n_blocks, EXPERT_ROWS), x1_all, we_gate[0], we_up[0],
                  we_down[0])

    yp = _combine(_dest_tiles(dest[:, :n_p], 128), ys, partp, wtp.T, ln2_g, ln2_b, tm=128)
    ysm = _combine(_dest_tiles(dest[:, n_p:], n_s), ys, parts, wts.T, ln2_g, ln2_b, tm=n_s)

    heads = lambda a, b, t: a.reshape(1, b, t, N_HEADS, HEAD_DIM)
    return (yp.reshape(bp, tp, d), ysm.reshape(bs, ts, d),
            heads(kp, bp, tp), heads(vp, bp, tp), csp[None],
            heads(ks, bs, ts), heads(vs, bs, ts), css[None])
```

```python
import functools

import jax
import jax.numpy as jnp
from jax import lax
from jax.experimental import pallas as pl
from jax.experimental.pallas import tpu as pltpu

F32 = jnp.float32
BF16 = jnp.bfloat16

N_HEADS = 16
HEAD_DIM = 64
CONV_WIDTH = 31
N_EXPERTS = 256
TOP_K = 8
N_GROUPS = 8
TOPK_GROUPS = 4
GROUP_SIZE = N_EXPERTS // N_GROUPS
ROUTED_SCALE = 2.5
LN_EPS = 1e-5
DEPTH = 1
DEEPNORM_ALPHA = (2 * DEPTH) ** 0.25
ATTN_SCALE = HEAD_DIM ** -0.5

LANES = 128
SUBLANES = 8
VMEM_LIMIT_BYTES = 56 * 1024 * 1024

KEY_BLOCK = 128
HIST_PAD = 32
GATHER_CHUNK = 128
BLOCK_CHUNKS = 4
LOG_DEAD = -110.0


def _layer_norm(x, g, b):
    mu = jnp.mean(x, axis=-1, keepdims=True)
    xc = x - mu
    var = jnp.mean(xc * xc, axis=-1, keepdims=True)
    return xc * lax.rsqrt(var + LN_EPS) * g + b


def _sigmoid(x):
    return 1.0 / (1.0 + jnp.exp(-x))


def _silu(x):
    return x * _sigmoid(x)


def _dot(a, b):
    return jnp.dot(a, b, preferred_element_type=F32)


def _dot_nt(a, b):
    return lax.dot_general(a, b, (((1,), (1,)), ((), ())), preferred_element_type=F32)


def _params(sem):
    return pltpu.CompilerParams(dimension_semantics=sem, vmem_limit_bytes=VMEM_LIMIT_BYTES)


def _dma_params(sem):
    return pltpu.CompilerParams(dimension_semantics=sem, vmem_limit_bytes=VMEM_LIMIT_BYTES,
                                disable_bounds_checks=True)


def _proj_conv_body(x_ref, hist_ref, lng_ref, lnb_ref, win_ref, bin_ref, cw_ref, cb_ref, clg_ref,
                    clb_ref, wco_ref, q_ref, k_ref, v_ref, gc_ref, sa_ref, cs_ref, ubuf, cbuf):
    tm = x_ref.shape[1]
    d = x_ref.shape[2]
    hist = CONV_WIDTH - 1
    lead = HIST_PAD - hist

    @pl.when(pl.program_id(1) == 0)
    def _():
        ubuf[0:lead, :] = jnp.zeros((lead, d), F32)
        ubuf[lead:HIST_PAD, :] = hist_ref[0]

    hb = _layer_norm(x_ref[0], lng_ref[...], lnb_ref[...]).astype(BF16)

    def proj(i):
        return _dot(hb, win_ref[:, i * d:(i + 1) * d]) + bin_ref[:, i * d:(i + 1) * d]

    ubuf[HIST_PAD:HIST_PAD + tm, :] = proj(0) * _sigmoid(proj(1))
    q_ref[0] = (proj(2) * ATTN_SCALE).astype(BF16)
    k_ref[0] = proj(3)
    v_ref[0] = proj(4)
    sa_ref[0] = _sigmoid(proj(6)).astype(BF16)

    rows = min(tm, 32)
    for r0 in range(0, tm, rows):
        acc = jnp.broadcast_to(cb_ref[...], (rows, d))
        for kk in range(CONV_WIDTH):
            acc = acc + cw_ref[kk:kk + 1, :] * ubuf[lead + kk + r0:lead + kk + r0 + rows, :]
        cbuf[r0:r0 + rows, :] = _silu(_layer_norm(acc, clg_ref[...], clb_ref[...])).astype(BF16)
    gc_ref[0] = (_sigmoid(proj(5)) * _dot(cbuf[...], wco_ref[...])).astype(BF16)

    cs_ref[0] = ubuf[tm + lead:tm + HIST_PAD, :]
    ubuf[0:HIST_PAD, :] = ubuf[tm:tm + HIST_PAD, :]


def _proj_conv(x, hist, ln_g, ln_b, w_in, b_in, conv_w, conv_b, cln_g, cln_b, w_co, tm):
    bsz, t, d = x.shape
    pw = w_in.shape[1]
    const = lambda shape: pl.BlockSpec(shape, lambda b, i: (0,) * len(shape))
    tile = lambda: pl.BlockSpec((1, tm, d), lambda b, i: (b, i, 0))
    return pl.pallas_call(
        _proj_conv_body,
        grid=(bsz, t // tm),
        in_specs=[
            tile(),
            pl.BlockSpec((1, CONV_WIDTH - 1, d), lambda b, i: (b, 0, 0)),
            const((1, d)), const((1, d)), const((d, pw)), const((1, pw)),
            const((CONV_WIDTH, d)), const((1, d)), const((1, d)), const((1, d)), const((d, d)),
        ],
        out_specs=[tile(), tile(), tile(), tile(), tile(),
                   pl.BlockSpec((1, CONV_WIDTH - 1, d), lambda b, i: (b, 0, 0))],
        out_shape=[
            jax.ShapeDtypeStruct((bsz, t, d), BF16),
            jax.ShapeDtypeStruct((bsz, t, d), F32),
            jax.ShapeDtypeStruct((bsz, t, d), F32),
            jax.ShapeDtypeStruct((bsz, t, d), BF16),
            jax.ShapeDtypeStruct((bsz, t, d), BF16),
            jax.ShapeDtypeStruct((bsz, CONV_WIDTH - 1, d), F32),
        ],
        scratch_shapes=[pltpu.VMEM((HIST_PAD + tm, d), F32), pltpu.VMEM((tm, d), BF16)],
        compiler_params=_params(("arbitrary", "arbitrary")),
        name="proj_conv",
    )(x, hist, ln_g, ln_b, w_in, b_in, conv_w, conv_b, cln_g, cln_b, w_co)


def _sb_block(qs, kb, vb, acc, ll, vis):
    kbn = kb.shape[0]
    z = _dot_nt(qs, kb)
    lk = -(jnp.maximum(z, 0.0) + jnp.log1p(jnp.exp(-jnp.abs(z))))
    if vis is not None:
        lk = jnp.where(vis, lk, 0.0)
    row = lax.broadcasted_iota(jnp.int32, (kbn, kbn), 0)
    col = lax.broadcasted_iota(jnp.int32, (kbn, kbn), 1)
    suffix = jnp.where(row >= col, 1.0, 0.0).astype(BF16)
    hi = lk.astype(BF16)
    lo = (lk - hi.astype(F32)).astype(BF16)
    inb = _dot(hi, suffix) + _dot(lo, suffix)
    arg = z + inb + ll
    if vis is not None:
        arg = jnp.where(vis, arg, -jnp.inf)
    acc = acc + _dot(jnp.exp(arg).astype(BF16), vb)
    return acc, ll + inb[:, 0:1]


def _sb_older_blocks(qs, k_ref, v_ref, first, acc, ll):
    def live(ll_):
        return (jnp.max(ll_, axis=0, keepdims=True)[0, 0] > LOG_DEAD).astype(jnp.int32)

    def cond(c):
        return jnp.logical_and(c[0] >= 0, c[3] > 0)

    def body(c):
        j, acc_, ll_, _ = c
        start = pl.multiple_of(j * KEY_BLOCK, KEY_BLOCK)
        kb = k_ref[0, pl.ds(start, KEY_BLOCK), :].astype(BF16)
        vb = v_ref[0, pl.ds(start, KEY_BLOCK), :].astype(BF16)
        acc_, ll_ = _sb_block(qs, kb, vb, acc_, ll_, None)
        return j - 1, acc_, ll_, live(ll_)

    _, acc, _, _ = lax.while_loop(cond, body, (first, acc, ll, live(ll)))
    return acc


def _stack_heads(q):
    lane = lax.broadcasted_iota(jnp.int32, q.shape, 1)
    zero = jnp.zeros_like(q)
    return jnp.concatenate([jnp.where(lane < HEAD_DIM, q, zero), jnp.where(lane >= HEAD_DIM, q, zero)], axis=0)


def _row_in_head(rows, cols):
    r = lax.broadcasted_iota(jnp.int32, (rows, cols), 0)
    return jnp.where(r >= rows // 2, r - rows // 2, r)


def _unstack_heads(acc):
    r = acc.shape[0] // 2
    lane = lax.broadcasted_iota(jnp.int32, (r, LANES), 1)
    return jnp.where(lane < HEAD_DIM, acc[:r], acc[r:])


def _attn_prompt_body(q_ref, k_ref, v_ref, o_ref):
    tq = q_ref.shape[1]
    nq = tq // KEY_BLOCK
    g0 = pl.program_id(2) * nq
    row = _row_in_head(2 * KEY_BLOCK, KEY_BLOCK)
    col = lax.broadcasted_iota(jnp.int32, (2 * KEY_BLOCK, KEY_BLOCK), 1)
    vis = col < row

    def kv_block(j):
        start = pl.multiple_of(j * KEY_BLOCK, KEY_BLOCK)
        return (k_ref[0, pl.ds(start, KEY_BLOCK), :].astype(BF16),
                v_ref[0, pl.ds(start, KEY_BLOCK), :].astype(BF16))

    def any_live(lls, off):
        m = jnp.full((2 * KEY_BLOCK, 1), -jnp.inf, F32)
        for qi in range(nq):
            m = jnp.maximum(m, jnp.where(g0 + qi - off >= 0, lls[qi], -jnp.inf))
        return (jnp.max(m, axis=0, keepdims=True)[0, 0] > LOG_DEAD).astype(jnp.int32)

    qss, accs, lls = [], [], []
    for qi in range(nq):
        qs = _stack_heads(q_ref[0, qi * KEY_BLOCK:(qi + 1) * KEY_BLOCK, :])
        kb, vb = kv_block(g0 + qi)
        acc, ll = _sb_block(qs, kb, vb, jnp.zeros((2 * KEY_BLOCK, LANES), F32),
                            jnp.zeros((2 * KEY_BLOCK, 1), F32), vis)
        qss.append(qs)
        accs.append(acc)
        lls.append(ll)

    def cond(c):
        return c[3] > 0

    def body(c):
        off, accs_, lls_, _ = c
        new_accs, new_lls = [], []
        for qi in range(nq):
            j = g0 + qi - off
            kb, vb = kv_block(jnp.maximum(j, 0))
            ll_in = jnp.where(j >= 0, lls_[qi], -jnp.inf)
            acc, ll = _sb_block(qss[qi], kb, vb, accs_[qi], ll_in, None)
            new_accs.append(acc)
            new_lls.append(ll)
        return off + 1, tuple(new_accs), tuple(new_lls), any_live(new_lls, off + 1)

    one = jnp.int32(1)
    _, accs, _, _ = lax.while_loop(cond, body, (one, tuple(accs), tuple(lls), any_live(lls, one)))
    for qi in range(nq):
        o_ref[0, qi * KEY_BLOCK:(qi + 1) * KEY_BLOCK, :] = _unstack_heads(accs[qi]).astype(BF16)


def _attn_prompt(q, k, v, tq):
    bsz, t, d = q.shape
    pairs = d // LANES
    return pl.pallas_call(
        _attn_prompt_body,
        grid=(bsz, pairs, t // tq),
        in_specs=[
            pl.BlockSpec((1, tq, LANES), lambda b, p, i: (b, i, p)),
            pl.BlockSpec((1, t, LANES), lambda b, p, i: (b, 0, p)),
            pl.BlockSpec((1, t, LANES), lambda b, p, i: (b, 0, p)),
        ],
        out_specs=pl.BlockSpec((1, tq, LANES), lambda b, p, i: (b, i, p)),
        out_shape=jax.ShapeDtypeStruct((bsz, t, d), BF16),
        compiler_params=_params(("arbitrary", "arbitrary", "arbitrary")),
        name="attn_prompt",
    )(q, k, v)


def _attn_sample_body(q_ref, kn_ref, vn_ref, ck_ref, cv_ref, o_ref):
    tq = q_ref.shape[1]
    past_blocks = ck_ref.shape[1] // KEY_BLOCK
    qs = _stack_heads(q_ref[0])
    row = _row_in_head(2 * tq, KEY_BLOCK)
    col = lax.broadcasted_iota(jnp.int32, (2 * tq, KEY_BLOCK), 1)
    acc = jnp.zeros((2 * tq, LANES), F32)
    ll = jnp.zeros((2 * tq, 1), F32)
    acc, ll = _sb_block(qs, kn_ref[0].astype(BF16), vn_ref[0].astype(BF16), acc, ll, col < row)
    acc = _sb_older_blocks(qs, ck_ref, cv_ref, past_blocks - 1, acc, ll)
    o_ref[0] = _unstack_heads(acc).astype(BF16)


def _attn_sample(q, k_new, v_new, cache_k, cache_v):
    bsz, t, d = q.shape
    past = cache_k.shape[1]
    pairs = d // LANES
    return pl.pallas_call(
        _attn_sample_body,
        grid=(bsz, pairs),
        in_specs=[
            pl.BlockSpec((1, t, LANES), lambda b, p: (b, 0, p)),
            pl.BlockSpec((1, KEY_BLOCK, LANES), lambda b, p: (b, 0, p)),
            pl.BlockSpec((1, KEY_BLOCK, LANES), lambda b, p: (b, 0, p)),
            pl.BlockSpec((1, past, LANES), lambda b, p: (b, 0, p)),
            pl.BlockSpec((1, past, LANES), lambda b, p: (b, 0, p)),
        ],
        out_specs=pl.BlockSpec((1, t, LANES), lambda b, p: (b, 0, p)),
        out_shape=jax.ShapeDtypeStruct((bsz, t, d), BF16),
        compiler_params=_params(("arbitrary", "arbitrary")),
        name="attn_sample",
    )(q, k_new, v_new, cache_k, cache_v)


def _first_argmax(cur, iota_f, n):
    m = jnp.max(cur, axis=0, keepdims=True)
    i = jnp.min(jnp.where(cur == m, iota_f, float(n)), axis=0, keepdims=True)
    return m, i


def _post_attn_body(o_ref, gc_ref, sa_ref, x_ref, cnt_ref, lng_ref, lnb_ref, wao_ref, wo_ref, l1g_ref,
                    l1b_ref, wrh_ref, wrl_ref, rb_ref, wsgu_ref, wsd_ref,
                    x1_ref, part_ref, idx_ref, wts_ref, rank_ref, cnt_out_ref, run):
    tm = x_ref.shape[0]
    ff = wsd_ref.shape[0]

    @pl.when(pl.program_id(0) == 0)
    def _():
        run[...] = cnt_ref[...]

    h = _layer_norm(x_ref[...], lng_ref[...], lnb_ref[...])
    att = _dot(o_ref[...], wao_ref[...])
    merged = gc_ref[...].astype(F32) + sa_ref[...].astype(F32) * att
    mixed = _dot(merged.astype(BF16), wo_ref[...])
    x1 = _layer_norm(DEEPNORM_ALPHA * h + mixed, l1g_ref[...], l1b_ref[...])
    x1_ref[...] = x1

    x1h = x1.astype(BF16)
    gu = _dot(x1h, wsgu_ref[...])
    shared = _dot((_silu(gu[:, :ff]) * gu[:, ff:]).astype(BF16), wsd_ref[...])
    part_ref[...] = DEEPNORM_ALPHA * x1 + shared

    x1l = (x1 - x1h.astype(F32)).astype(BF16)
    logits = _dot_nt(wrh_ref[...], x1h) + _dot_nt(wrl_ref[...], x1h) + _dot_nt(wrh_ref[...], x1l)
    scores = _sigmoid(logits)
    choice = scores + rb_ref[...]
    neg = -jnp.inf

    giota = lax.broadcasted_iota(jnp.int32, (GROUP_SIZE, tm), 0).astype(F32)
    gs = []
    for g in range(N_GROUPS):
        blk = choice[g * GROUP_SIZE:(g + 1) * GROUP_SIZE, :]
        m1, i1 = _first_argmax(blk, giota, GROUP_SIZE)
        m2 = jnp.max(jnp.where(giota == i1, neg, blk), axis=0, keepdims=True)
        gs.append(m1 + m2)
    gscore = jnp.concatenate(gs, axis=0)

    g8 = lax.broadcasted_iota(jnp.int32, (N_GROUPS, tm), 0).astype(F32)
    gsel = jnp.zeros((N_GROUPS, tm), F32)
    cur = gscore
    for _ in range(TOPK_GROUPS):
        _, i = _first_argmax(cur, g8, N_GROUPS)
        hit = g8 == i
        gsel = jnp.where(hit, 1.0, gsel)
        cur = jnp.where(hit, neg, cur)
    emask = jnp.concatenate(
        [jnp.broadcast_to(gsel[g:g + 1, :], (GROUP_SIZE, tm)) for g in range(N_GROUPS)], axis=0)

    eiota = lax.broadcasted_iota(jnp.int32, (N_EXPERTS, tm), 0).astype(F32)
    cur = jnp.where(emask > 0.0, choice, neg)
    hits, sel_w = [], []
    for r in range(TOP_K):
        _, i = _first_argmax(cur, eiota, N_EXPERTS)
        hit = eiota == i
        hits.append(hit)
        sel_w.append(jnp.sum(jnp.where(hit, scores, 0.0), axis=0, keepdims=True))
        cur = jnp.where(hit, neg, cur)
        idx_ref[r:r + 1, :] = i.astype(jnp.int32)
    wsum = sel_w[0]
    for r in range(1, TOP_K):
        wsum = wsum + sel_w[r]
    for r in range(TOP_K):
        wts_ref[r:r + 1, :] = sel_w[r] / wsum * ROUTED_SCALE

    picked = jnp.zeros((N_EXPERTS, tm), F32)
    for r in range(TOP_K):
        picked = jnp.where(hits[r], 1.0, picked)
    picked_b = picked.astype(BF16)
    trow = lax.broadcasted_iota(jnp.int32, (tm, tm), 0)
    tcol = lax.broadcasted_iota(jnp.int32, (tm, tm), 1)
    earlier = jnp.where(trow < tcol, 1.0, 0.0).astype(BF16)
    before = _dot(picked_b, earlier) + run[:, 0:1]
    for r in range(TOP_K):
        rank_ref[r:r + 1, :] = jnp.sum(jnp.where(hits[r], before, 0.0), axis=0, keepdims=True).astype(jnp.int32)
    run[...] = run[...] + _dot(picked_b, jnp.ones((tm, LANES), BF16))
    cnt_out_ref[...] = run[...]


def _post_attn(o, gc, sa, x, cnt_in, ln_g, ln_b, w_ao, w_o, l1g, l1b, wr_hi, wr_lo, r_bias, ws_gu, ws_d, tm):
    n, d = x.shape
    ff = ws_d.shape[0]
    const = lambda shape: pl.BlockSpec(shape, lambda i: (0,) * len(shape))
    tile = lambda: pl.BlockSpec((tm, d), lambda i: (i, 0))
    small = lambda: pl.BlockSpec((TOP_K, tm), lambda i: (0, i))
    return pl.pallas_call(
        _post_attn_body,
        grid=(n // tm,),
        in_specs=[tile(), tile(), tile(), tile(), const((N_EXPERTS, LANES)),
                  const((1, d)), const((1, d)), const((d, d)), const((d, d)), const((1, d)), const((1, d)),
                  const((N_EXPERTS, d)), const((N_EXPERTS, d)), const((N_EXPERTS, 1)),
                  const((d, 2 * ff)), const((ff, d))],
        out_specs=[tile(), tile(), small(), small(), small(), const((N_EXPERTS, LANES))],
        out_shape=[
            jax.ShapeDtypeStruct((n, d), F32),
            jax.ShapeDtypeStruct((n, d), F32),
            jax.ShapeDtypeStruct((TOP_K, n), jnp.int32),
            jax.ShapeDtypeStruct((TOP_K, n), F32),
            jax.ShapeDtypeStruct((TOP_K, n), jnp.int32),
            jax.ShapeDtypeStruct((N_EXPERTS, LANES), F32),
        ],
        scratch_shapes=[pltpu.VMEM((N_EXPERTS, LANES), F32)],
        compiler_params=_params(("arbitrary",)),
        name="post_attn",
    )(o, gc, sa, x, cnt_in, ln_g, ln_b, w_ao, w_o, l1g, l1b, wr_hi, wr_lo, r_bias, ws_gu, ws_d)


def _dest_body(idx_ref, rank_ref, rs_ref, dest_ref):
    tm = idx_ref.shape[1]
    eiota = lax.broadcasted_iota(jnp.int32, (N_EXPERTS, tm), 0)
    rs = rs_ref[...]
    for r in range(TOP_K):
        base = jnp.sum(jnp.where(eiota == idx_ref[r:r + 1, :], rs, 0.0), axis=0, keepdims=True)
        dest_ref[r:r + 1, :] = base.astype(jnp.int32) + rank_ref[r:r + 1, :]


def _dest(idx, rank, row_start_f, tm):
    n = idx.shape[1]
    small = lambda: pl.BlockSpec((TOP_K, tm), lambda i: (0, i))
    return pl.pallas_call(
        _dest_body,
        grid=(n // tm,),
        in_specs=[small(), small(), pl.BlockSpec((N_EXPERTS, 1), lambda i: (0, 0))],
        out_specs=small(),
        out_shape=jax.ShapeDtypeStruct((TOP_K, n), jnp.int32),
        compiler_params=_params(("arbitrary",)),
        name="dest",
    )(idx, rank, row_start_f)


def _dispatch_body(meta_ref, dt_hbm, xp_ref, xsm_ref, xs_hbm, idx_smem, zbuf, isem, ssem, zsem, *, n_tok_steps):
    i = pl.program_id(0)
    n = pl.num_programs(0)
    slot = i % 2
    tm = xp_ref.shape[0]
    total_chunks = xs_hbm.shape[0] // GATHER_CHUNK

    def idx_copy(tile, s):
        return pltpu.make_async_copy(dt_hbm.at[pl.ds(tile, 1)], idx_smem.at[pl.ds(s, 1)], isem.at[s])

    @pl.when(i == 0)
    def _():
        idx_copy(0, 0).start()

    idx_copy(i, slot).wait()

    @pl.when(i + 1 < n)
    def _():
        idx_copy(i + 1, 1 - slot).start()

    def scatter(src_ref):
        cnt = src_ref.shape[0]
        for j in range(TOP_K):
            def body(t, carry):
                row = idx_smem[slot, j * tm + t]
                pltpu.make_async_copy(src_ref.at[pl.ds(t, 1)], xs_hbm.at[pl.ds(row, 1)], ssem).start()
                return carry
            lax.fori_loop(0, cnt, body, 0, unroll=8)
        for j in range(TOP_K):
            pltpu.make_async_copy(src_ref, xs_hbm.at[pl.ds(0, cnt)], ssem).wait()

    @pl.when(i < n_tok_steps - 1)
    def _():
        scatter(xp_ref)

    @pl.when(i == n_tok_steps - 1)
    def _():
        scatter(xsm_ref)
        zbuf[...] = jnp.zeros(zbuf.shape, F32)

    @pl.when(i >= n_tok_steps)
    def _():
        per_tile = TOP_K * tm
        cnt = jnp.clip(meta_ref[1] - (i - n_tok_steps) * per_tile, 0, per_tile)
        zrow = zbuf.at[pl.ds(0, 1)]

        def pad_start(r, carry):
            pltpu.make_async_copy(zrow, xs_hbm.at[pl.ds(idx_smem[slot, r], 1)], zsem).start()
            return carry
        lax.fori_loop(0, cnt, pad_start, 0)

        def pad_wait(r, carry):
            pltpu.make_async_copy(zrow, xs_hbm.at[pl.ds(0, 1)], zsem).wait()
            return carry
        lax.fori_loop(0, cnt, pad_wait, 0)

    @pl.when(i == n - 1)
    def _():
        def tail_chunk(c):
            return pltpu.make_async_copy(zbuf, xs_hbm.at[pl.ds(c * GATHER_CHUNK, GATHER_CHUNK)], zsem)

        def tail_start(c, carry):
            tail_chunk(c).start()
            return carry
        lax.fori_loop(meta_ref[0], total_chunks, tail_start, 0)

        def tail_wait(c, carry):
            tail_chunk(c).wait()
            return carry
        lax.fori_loop(meta_ref[0], total_chunks, tail_wait, 0)


def _dispatch(meta, tiles, x1p, x1s, n_rows, tm):
    n_p, d = x1p.shape
    n_s = x1s.shape[0]
    n_pt = n_p // tm
    grid_spec = pltpu.PrefetchScalarGridSpec(
        num_scalar_prefetch=1,
        grid=(tiles.shape[0],),
        in_specs=[
            pl.BlockSpec(memory_space=pl.ANY),
            pl.BlockSpec((tm, d), lambda i, *_: (jnp.minimum(i, n_pt - 1), 0)),
            pl.BlockSpec((n_s, d), lambda i, *_: (0, 0)),
        ],
        out_specs=pl.BlockSpec(memory_space=pl.ANY),
        scratch_shapes=[
            pltpu.SMEM((2, TOP_K * tm), jnp.int32),
            pltpu.VMEM((GATHER_CHUNK, d), F32),
            pltpu.SemaphoreType.DMA((2,)),
            pltpu.SemaphoreType.DMA(()),
            pltpu.SemaphoreType.DMA(()),
        ],
    )
    return pl.pallas_call(
        functools.partial(_dispatch_body, n_tok_steps=n_pt + 1),
        grid_spec=grid_spec,
        out_shape=jax.ShapeDtypeStruct((n_rows, d), F32),
        compiler_params=_dma_params(("arbitrary",)),
        name="dispatch",
    )(meta, tiles, x1p, x1s)


def _experts_body(be_ref, bx_ref, bn_ref, bz_ref, xs_hbm, wg_ref, wu_ref, wd_ref, ys_hbm,
                  xbuf, obuf, gsem, osem, wgu_bf, wd_bf):
    b = pl.program_id(0)
    nb = pl.num_programs(0)
    slot = b % 2
    ff = wd_ref.shape[1]

    def in_copy(blk, s, c):
        src = xs_hbm.at[pl.ds((bx_ref[blk] + c) * GATHER_CHUNK, GATHER_CHUNK)]
        return pltpu.make_async_copy(src, xbuf.at[s, pl.ds(c * GATHER_CHUNK, GATHER_CHUNK)], gsem.at[s])

    def out_copy(blk, s, c):
        dst = ys_hbm.at[pl.ds((bx_ref[blk] + c) * GATHER_CHUNK, GATHER_CHUNK)]
        return pltpu.make_async_copy(obuf.at[s, pl.ds(c * GATHER_CHUNK, GATHER_CHUNK)], dst, osem.at[s])

    def for_chunks(blk, count, fn):
        for c in range(BLOCK_CHUNKS):
            @pl.when(c < count)
            def _():
                fn(c)

    def n_in(blk):
        return jnp.where(bz_ref[blk] == 0, bn_ref[blk], 0)

    @pl.when(b == 0)
    def _():
        for_chunks(0, n_in(0), lambda c: in_copy(0, 0, c).start())

    @pl.when(b + 1 < nb)
    def _():
        for_chunks(b + 1, n_in(b + 1), lambda c: in_copy(b + 1, 1 - slot, c).start())

    @pl.when(b >= 2)
    def _():
        for_chunks(b - 2, bn_ref[b - 2], lambda c: out_copy(b - 2, slot, c).wait())

    for_chunks(b, n_in(b), lambda c: in_copy(b, slot, c).wait())

    e_now = be_ref[b]
    e_prev = be_ref[jnp.maximum(b - 1, 0)]

    @pl.when(jnp.logical_or(b == 0, e_now != e_prev))
    def _():
        wgu_bf[:, 0:ff] = wg_ref[0].astype(BF16)
        wgu_bf[:, ff:2 * ff] = wu_ref[0].astype(BF16)
        wd_bf[...] = wd_ref[0].astype(BF16)

    for nch in range(1, BLOCK_CHUNKS + 1):
        @pl.when(n_in(b) == nch)
        def _():
            rows = nch * GATHER_CHUNK
            gu = _dot(xbuf[slot, 0:rows, :].astype(BF16), wgu_bf[...])
            hid = _silu(gu[:, :ff]) * gu[:, ff:]
            obuf[slot, 0:rows, :] = _dot(hid.astype(BF16), wd_bf[...])

    @pl.when(bz_ref[b] != 0)
    def _():
        obuf[slot] = jnp.zeros(obuf.shape[1:], F32)

    for_chunks(b, bn_ref[b], lambda c: out_copy(b, slot, c).start())

    @pl.when(b == nb - 1)
    def _():
        for_chunks(b, bn_ref[b], lambda c: out_copy(b, slot, c).wait())

        @pl.when(nb > 1)
        def _():
            for_chunks(b - 1, bn_ref[b - 1], lambda c: out_copy(b - 1, 1 - slot, c).wait())


def _experts(block_e, block_x, block_n, block_z, xs, we_gate, we_up, we_down):
    nb = block_e.shape[0]
    n_rows, d = xs.shape
    ff = we_gate.shape[2]
    rows = BLOCK_CHUNKS * GATHER_CHUNK
    grid_spec = pltpu.PrefetchScalarGridSpec(
        num_scalar_prefetch=4,
        grid=(nb,),
        in_specs=[
            pl.BlockSpec(memory_space=pl.ANY),
            pl.BlockSpec((1, d, ff), lambda b, be, *_: (be[b], 0, 0)),
            pl.BlockSpec((1, d, ff), lambda b, be, *_: (be[b], 0, 0)),
            pl.BlockSpec((1, ff, d), lambda b, be, *_: (be[b], 0, 0)),
        ],
        out_specs=pl.BlockSpec(memory_space=pl.ANY),
        scratch_shapes=[
            pltpu.VMEM((2, rows, d), F32),
            pltpu.VMEM((2, rows, d), F32),
            pltpu.SemaphoreType.DMA((2,)),
            pltpu.SemaphoreType.DMA((2,)),
            pltpu.VMEM((d, 2 * ff), BF16),
            pltpu.VMEM((ff, d), BF16),
        ],
    )
    return pl.pallas_call(
        _experts_body,
        grid_spec=grid_spec,
        out_shape=jax.ShapeDtypeStruct((n_rows, d), F32),
        compiler_params=_dma_params(("arbitrary",)),
        name="experts",
    )(block_e, block_x, block_n, block_z, xs, we_gate, we_up, we_down)


def _combine_body(dt_hbm, ys_hbm, part_ref, w_ref, g_ref, b_ref, y_ref, idx_smem, gbuf, isem, gsem):
    i = pl.program_id(0)
    n = pl.num_programs(0)
    slot = i % 2
    tm = part_ref.shape[0]

    def idx_copy(tile, s):
        return pltpu.make_async_copy(dt_hbm.at[pl.ds(tile, 1)], idx_smem.at[pl.ds(s, 1)], isem.at[s])

    def issue_rows(s):
        for j in range(TOP_K):
            def body(t, carry):
                row = idx_smem[s, j * tm + t]
                pltpu.make_async_copy(ys_hbm.at[pl.ds(row, 1)], gbuf.at[s, j, pl.ds(t, 1)], gsem.at[s]).start()
                return carry
            lax.fori_loop(0, tm, body, 0, unroll=8)

    def wait_rows(s):
        for j in range(TOP_K):
            pltpu.make_async_copy(ys_hbm.at[pl.ds(0, tm)], gbuf.at[s, j], gsem.at[s]).wait()

    @pl.when(i == 0)
    def _():
        first = idx_copy(0, 0)
        first.start()
        first.wait()
        issue_rows(0)

        @pl.when(n > 1)
        def _():
            idx_copy(1, 1).start()

    @pl.when(i + 1 < n)
    def _():
        idx_copy(i + 1, 1 - slot).wait()
        issue_rows(1 - slot)

    wait_rows(slot)

    @pl.when(i + 2 < n)
    def _():
        idx_copy(i + 2, slot).start()

    f = part_ref[...]
    for j in range(TOP_K):
        f = f + w_ref[:, j:j + 1] * gbuf[slot, j]
    y_ref[...] = _layer_norm(f, g_ref[...], b_ref[...])


def _combine(dest_tiles, ys, part, wts, ln_g, ln_b, tm):
    n, d = part.shape
    return pl.pallas_call(
        _combine_body,
        grid=(n // tm,),
        in_specs=[
            pl.BlockSpec(memory_space=pl.ANY),
            pl.BlockSpec(memory_space=pl.ANY),
            pl.BlockSpec((tm, d), lambda i: (i, 0)),
            pl.BlockSpec((tm, TOP_K), lambda i: (i, 0)),
            pl.BlockSpec((1, d), lambda i: (0, 0)),
            pl.BlockSpec((1, d), lambda i: (0, 0)),
        ],
        out_specs=pl.BlockSpec((tm, d), lambda i: (i, 0)),
        out_shape=jax.ShapeDtypeStruct((n, d), F32),
        scratch_shapes=[
            pltpu.SMEM((2, TOP_K * tm), jnp.int32),
            pltpu.VMEM((2, TOP_K, tm, d), F32),
            pltpu.SemaphoreType.DMA((2,)),
            pltpu.SemaphoreType.DMA((2,)),
        ],
        compiler_params=_dma_params(("arbitrary",)),
        name="combine",
    )(dest_tiles, ys, part, wts, ln_g, ln_b)


def _dest_tiles(dest, tm):
    k, n = dest.shape
    return dest.reshape(k, n // tm, tm).transpose(1, 0, 2).reshape(n // tm, k * tm)


def kernel(x_prompt, x_sample, cache_k, cache_v, state_conv, ln_in_g, ln_in_b, w_in, b_in, conv_w, conv_b,
           conv_ln_g, conv_ln_b, w_conv_out, w_attn_out, w_out, ln1_g, ln1_b, w_router, router_bias,
           we_gate, we_up, we_down, ws_gate, ws_up, ws_down, ln2_g, ln2_b):
    bp, tp, d = x_prompt.shape
    bs, ts, _ = x_sample.shape
    past = cache_k.shape[2]
    assert w_in.shape[0] == DEPTH and d == N_HEADS * HEAD_DIM and conv_w.shape[2] == d
    assert tp % 1024 == 0 and ts <= KEY_BLOCK and ts % 16 == 0 and past % KEY_BLOCK == 0
    n_p, n_s = bp * tp, bs * ts
    assert n_s % LANES == 0

    row = lambda a: a.reshape(1, -1)
    lng, lnb = row(ln_in_g), row(ln_in_b)
    w_in_b = w_in[0].astype(BF16)
    w_co_b = w_conv_out[0].astype(BF16)
    conv_args = (lng, lnb, w_in_b, b_in, conv_w[0], conv_b, conv_ln_g, conv_ln_b, w_co_b)

    qp, kp, vp, gcp, sap, csp = _proj_conv(x_prompt, jnp.zeros((bp, CONV_WIDTH - 1, d), F32), *conv_args, tm=256)
    op = _attn_prompt(qp, kp, vp, tq=512)

    qs, ks, vs, gcs, sas, css = _proj_conv(x_sample, state_conv[0], *conv_args, tm=ts)
    pad_new = lambda a: jnp.pad(a, ((0, 0), (0, KEY_BLOCK - ts), (0, 0)))
    os_ = _attn_sample(qs, pad_new(ks), pad_new(vs), cache_k[0].reshape(bs, past, d), cache_v[0].reshape(bs, past, d))

    wr = w_router[0].T
    wr_hi = wr.astype(BF16)
    wr_lo = (wr - wr_hi.astype(F32)).astype(BF16)
    post_args = (lng, lnb, w_attn_out[0].astype(BF16), w_out[0].astype(BF16), ln1_g, ln1_b, wr_hi, wr_lo,
                 router_bias.reshape(N_EXPERTS, 1),
                 jnp.concatenate([ws_gate[0], ws_up[0]], axis=1).astype(BF16), ws_down[0].astype(BF16))
    flat = lambda a, n: a.reshape(n, d)
    zero_cnt = jnp.zeros((N_EXPERTS, LANES), F32)
    x1p, partp, idxp, wtp, rankp, cnt_p = _post_attn(
        flat(op, n_p), flat(gcp, n_p), flat(sap, n_p), flat(x_prompt, n_p), zero_cnt, *post_args, tm=256)
    x1s, parts, idxs, wts, ranks, cnt = _post_attn(
        flat(os_, n_s), flat(gcs, n_s), flat(sas, n_s), flat(x_sample, n_s), cnt_p, *post_args, tm=n_s)

    i32 = jnp.int32
    n_tok = n_p + n_s
    total_chunks = (n_tok * TOP_K + N_EXPERTS * (GATHER_CHUNK - 1) + GATHER_CHUNK - 1) // GATHER_CHUNK
    n_blocks = total_chunks // BLOCK_CHUNKS + N_EXPERTS
    counts = cnt[:, 0].astype(i32)
    chunks = (counts + GATHER_CHUNK - 1) // GATHER_CHUNK
    chunk_end = jnp.cumsum(chunks)
    chunk_start = chunk_end - chunks
    row_start = chunk_start * GATHER_CHUNK
    used_chunks = chunk_end[-1]
    blocks = (chunks + BLOCK_CHUNKS - 1) // BLOCK_CHUNKS
    blk_end = jnp.cumsum(blocks)
    blk_start = blk_end - blocks
    bid = jnp.arange(n_blocks, dtype=i32)
    block_e = jnp.minimum(jnp.searchsorted(blk_end, bid, side='right'), N_EXPERTS - 1).astype(i32)
    k_in_e = bid - blk_start[block_e]
    is_tail = bid >= blk_end[-1]
    tail_x = used_chunks + BLOCK_CHUNKS * (bid - blk_end[-1])
    block_x = jnp.where(is_tail, tail_x, chunk_start[block_e] + BLOCK_CHUNKS * k_in_e)
    block_n = jnp.where(is_tail, total_chunks - tail_x, chunks[block_e] - BLOCK_CHUNKS * k_in_e)
    block_n = jnp.clip(block_n, 0, BLOCK_CHUNKS).astype(i32)
    block_x = jnp.where(block_n > 0, block_x, 0).astype(i32)
    block_z = is_tail.astype(i32)
    pad_e = chunks * GATHER_CHUNK - counts
    pad_end = jnp.cumsum(pad_e)
    meta = jnp.stack([used_chunks, pad_end[-1]]).astype(i32)
    pid = jnp.arange(N_EXPERTS * (GATHER_CHUNK - 1), dtype=i32)
    pad_owner = jnp.minimum(jnp.searchsorted(pad_end, pid, side='right'), N_EXPERTS - 1)
    pad_rows = row_start[pad_owner] + counts[pad_owner] + pid - (pad_end - pad_e)[pad_owner]
    pad_rows = jnp.where(pid < pad_end[-1], pad_rows, 0).astype(i32)

    rs_f = row_start.astype(F32).reshape(N_EXPERTS, 1)
    dest_p = _dest(idxp, rankp, rs_f, tm=1024)
    dest_s = _dest(idxs, ranks, rs_f, tm=n_s)

    tm_d = 256
    pad_s = jnp.pad(dest_s.reshape(TOP_K, 1, n_s), ((0, 0), (0, 0), (0, tm_d - n_s))).reshape(1, TOP_K * tm_d)
    per_tile = TOP_K * tm_d
    pad_tiles = jnp.pad(pad_rows, (0, -pad_rows.shape[0] % per_tile)).reshape(-1, per_tile)
    disp_tiles = jnp.concatenate([_dest_tiles(dest_p, tm_d), pad_s, pad_tiles], axis=0)
    xs = _dispatch(meta, disp_tiles, x1p, x1s, total_chunks * GATHER_CHUNK, tm=tm_d)
    ys = _experts(block_e, block_x, block_n, block_z, xs, we_gate[0], we_up[0], we_down[0])

    yp = _combine(_dest_tiles(dest_p, 128), ys, partp, wtp.T, ln2_g, ln2_b, tm=128)
    ysm = _combine(_dest_tiles(dest_s, n_s), ys, parts, wts.T, ln2_g, ln2_b, tm=n_s)

    heads = lambda a, b, t: a.reshape(1, b, t, N_HEADS, HEAD_DIM)
    return (yp.reshape(bp, tp, d), ysm.reshape(bs, ts, d),
            heads(kp, bp, tp), heads(vp, bp, tp), csp[None],
            heads(ks, bs, ts), heads(vs, bs, ts), css[None])
```

```python
import functools

import jax
import jax.numpy as jnp
from jax import lax
from jax.experimental import pallas as pl
from jax.experimental.pallas import tpu as pltpu

F32 = jnp.float32
BF16 = jnp.bfloat16

N_HEADS = 16
HEAD_DIM = 64
CONV_WIDTH = 31
N_EXPERTS = 256
TOP_K = 8
N_GROUPS = 8
TOPK_GROUPS = 4
GROUP_SIZE = N_EXPERTS // N_GROUPS
ROUTED_SCALE = 2.5
LN_EPS = 1e-5
DEPTH = 1
DEEPNORM_ALPHA = (2 * DEPTH) ** 0.25
ATTN_SCALE = HEAD_DIM ** -0.5

LANES = 128
SUBLANES = 8
VMEM_LIMIT_BYTES = 56 * 1024 * 1024

KEY_BLOCK = 128
HIST_PAD = 32
GATHER_CHUNK = 128
BLOCK_CHUNKS = 4
LOG_DEAD = -110.0


def _layer_norm(x, g, b):
    mu = jnp.mean(x, axis=-1, keepdims=True)
    xc = x - mu
    var = jnp.mean(xc * xc, axis=-1, keepdims=True)
    return xc * lax.rsqrt(var + LN_EPS) * g + b


def _sigmoid(x):
    return 1.0 / (1.0 + jnp.exp(-x))


def _silu(x):
    return x * _sigmoid(x)


def _dot(a, b):
    return jnp.dot(a, b, preferred_element_type=F32)


def _dot_nt(a, b):
    return lax.dot_general(a, b, (((1,), (1,)), ((), ())), preferred_element_type=F32)


def _params(sem):
    return pltpu.CompilerParams(dimension_semantics=sem, vmem_limit_bytes=VMEM_LIMIT_BYTES)


def _dma_params(sem):
    return pltpu.CompilerParams(dimension_semantics=sem, vmem_limit_bytes=VMEM_LIMIT_BYTES,
                                disable_bounds_checks=True)


def _proj_conv_body(x_ref, hist_ref, lng_ref, lnb_ref, win_ref, bin_ref, cw_ref, cb_ref, clg_ref,
                    clb_ref, wco_ref, q_ref, k_ref, v_ref, gc_ref, sa_ref, cs_ref, ubuf, cbuf):
    tm = x_ref.shape[1]
    d = x_ref.shape[2]
    hist = CONV_WIDTH - 1
    lead = HIST_PAD - hist

    @pl.when(pl.program_id(1) == 0)
    def _():
        ubuf[0:lead, :] = jnp.zeros((lead, d), F32)
        ubuf[lead:HIST_PAD, :] = hist_ref[0]

    hb = _layer_norm(x_ref[0], lng_ref[...], lnb_ref[...]).astype(BF16)

    def proj(i):
        return _dot(hb, win_ref[:, i * d:(i + 1) * d]) + bin_ref[:, i * d:(i + 1) * d]

    ubuf[HIST_PAD:HIST_PAD + tm, :] = proj(0) * _sigmoid(proj(1))
    q_ref[0] = (proj(2) * ATTN_SCALE).astype(BF16)
    k_ref[0] = proj(3)
    v_ref[0] = proj(4)
    sa_ref[0] = _sigmoid(proj(6)).astype(BF16)

    rows = min(tm, 32)
    for r0 in range(0, tm, rows):
        acc = jnp.broadcast_to(cb_ref[...], (rows, d))
        for kk in range(CONV_WIDTH):
            acc = acc + cw_ref[kk:kk + 1, :] * ubuf[lead + kk + r0:lead + kk + r0 + rows, :]
        cbuf[r0:r0 + rows, :] = _silu(_layer_norm(acc, clg_ref[...], clb_ref[...])).astype(BF16)
    gc_ref[0] = (_sigmoid(proj(5)) * _dot(cbuf[...], wco_ref[...])).astype(BF16)

    cs_ref[0] = ubuf[tm + lead:tm + HIST_PAD, :]
    ubuf[0:HIST_PAD, :] = ubuf[tm:tm + HIST_PAD, :]


def _proj_conv(x, hist, ln_g, ln_b, w_in, b_in, conv_w, conv_b, cln_g, cln_b, w_co, tm):
    bsz, t, d = x.shape
    pw = w_in.shape[1]
    const = lambda shape: pl.BlockSpec(shape, lambda b, i: (0,) * len(shape))
    tile = lambda: pl.BlockSpec((1, tm, d), lambda b, i: (b, i, 0))
    return pl.pallas_call(
        _proj_conv_body,
        grid=(bsz, t // tm),
        in_specs=[
            tile(),
            pl.BlockSpec((1, CONV_WIDTH - 1, d), lambda b, i: (b, 0, 0)),
            const((1, d)), const((1, d)), const((d, pw)), const((1, pw)),
            const((CONV_WIDTH, d)), const((1, d)), const((1, d)), const((1, d)), const((d, d)),
        ],
        out_specs=[tile(), tile(), tile(), tile(), tile(),
                   pl.BlockSpec((1, CONV_WIDTH - 1, d), lambda b, i: (b, 0, 0))],
        out_shape=[
            jax.ShapeDtypeStruct((bsz, t, d), BF16),
            jax.ShapeDtypeStruct((bsz, t, d), F32),
            jax.ShapeDtypeStruct((bsz, t, d), F32),
            jax.ShapeDtypeStruct((bsz, t, d), BF16),
            jax.ShapeDtypeStruct((bsz, t, d), BF16),
            jax.ShapeDtypeStruct((bsz, CONV_WIDTH - 1, d), F32),
        ],
        scratch_shapes=[pltpu.VMEM((HIST_PAD + tm, d), F32), pltpu.VMEM((tm, d), BF16)],
        compiler_params=_params(("arbitrary", "arbitrary")),
        name="proj_conv",
    )(x, hist, ln_g, ln_b, w_in, b_in, conv_w, conv_b, cln_g, cln_b, w_co)


def _sb_block(qs, kb, vb, acc, ll, vis):
    kbn = kb.shape[0]
    z = _dot_nt(qs, kb)
    lk = -(jnp.maximum(z, 0.0) + jnp.log1p(jnp.exp(-jnp.abs(z))))
    if vis is not None:
        lk = jnp.where(vis, lk, 0.0)
    row = lax.broadcasted_iota(jnp.int32, (kbn, kbn), 0)
    col = lax.broadcasted_iota(jnp.int32, (kbn, kbn), 1)
    suffix = jnp.where(row >= col, 1.0, 0.0).astype(BF16)
    hi = lk.astype(BF16)
    lo = (lk - hi.astype(F32)).astype(BF16)
    inb = _dot(hi, suffix) + _dot(lo, suffix)
    arg = z + inb + ll
    if vis is not None:
        arg = jnp.where(vis, arg, -jnp.inf)
    acc = acc + _dot(jnp.exp(arg).astype(BF16), vb)
    return acc, ll + inb[:, 0:1]


def _sb_older_blocks(qs, k_ref, v_ref, first, acc, ll):
    def live(ll_):
        return (jnp.max(ll_, axis=0, keepdims=True)[0, 0] > LOG_DEAD).astype(jnp.int32)

    def cond(c):
        return jnp.logical_and(c[0] >= 0, c[3] > 0)

    def body(c):
        j, acc_, ll_, _ = c
        start = pl.multiple_of(j * KEY_BLOCK, KEY_BLOCK)
        kb = k_ref[0, pl.ds(start, KEY_BLOCK), :].astype(BF16)
        vb = v_ref[0, pl.ds(start, KEY_BLOCK), :].astype(BF16)
        acc_, ll_ = _sb_block(qs, kb, vb, acc_, ll_, None)
        return j - 1, acc_, ll_, live(ll_)

    _, acc, _, _ = lax.while_loop(cond, body, (first, acc, ll, live(ll)))
    return acc


def _stack_heads(q):
    lane = lax.broadcasted_iota(jnp.int32, q.shape, 1)
    zero = jnp.zeros_like(q)
    return jnp.concatenate([jnp.where(lane < HEAD_DIM, q, zero), jnp.where(lane >= HEAD_DIM, q, zero)], axis=0)


def _row_in_head(rows, cols):
    r = lax.broadcasted_iota(jnp.int32, (rows, cols), 0)
    return jnp.where(r >= rows // 2, r - rows // 2, r)


def _unstack_heads(acc):
    r = acc.shape[0] // 2
    lane = lax.broadcasted_iota(jnp.int32, (r, LANES), 1)
    return jnp.where(lane < HEAD_DIM, acc[:r], acc[r:])


def _attn_prompt_body(q_ref, k_ref, v_ref, o_ref):
    tq = q_ref.shape[1]
    nq = tq // KEY_BLOCK
    g0 = pl.program_id(2) * nq
    row = _row_in_head(2 * KEY_BLOCK, KEY_BLOCK)
    col = lax.broadcasted_iota(jnp.int32, (2 * KEY_BLOCK, KEY_BLOCK), 1)
    vis = col < row

    def kv_block(j):
        start = pl.multiple_of(j * KEY_BLOCK, KEY_BLOCK)
        return (k_ref[0, pl.ds(start, KEY_BLOCK), :].astype(BF16),
                v_ref[0, pl.ds(start, KEY_BLOCK), :].astype(BF16))

    def any_live(lls, off):
        m = jnp.full((2 * KEY_BLOCK, 1), -jnp.inf, F32)
        for qi in range(nq):
            m = jnp.maximum(m, jnp.where(g0 + qi - off >= 0, lls[qi], -jnp.inf))
        return (jnp.max(m, axis=0, keepdims=True)[0, 0] > LOG_DEAD).astype(jnp.int32)

    qss, accs, lls = [], [], []
    for qi in range(nq):
        qs = _stack_heads(q_ref[0, qi * KEY_BLOCK:(qi + 1) * KEY_BLOCK, :])
        kb, vb = kv_block(g0 + qi)
        acc, ll = _sb_block(qs, kb, vb, jnp.zeros((2 * KEY_BLOCK, LANES), F32),
                            jnp.zeros((2 * KEY_BLOCK, 1), F32), vis)
        qss.append(qs)
        accs.append(acc)
        lls.append(ll)

    def cond(c):
        return c[3] > 0

    def body(c):
        off, accs_, lls_, _ = c
        new_accs, new_lls = [], []
        for qi in range(nq):
            j = g0 + qi - off
            kb, vb = kv_block(jnp.maximum(j, 0))
            ll_in = jnp.where(j >= 0, lls_[qi], -jnp.inf)
            acc, ll = _sb_block(qss[qi], kb, vb, accs_[qi], ll_in, None)
            new_accs.append(acc)
            new_lls.append(ll)
        return off + 1, tuple(new_accs), tuple(new_lls), any_live(new_lls, off + 1)

    one = jnp.int32(1)
    _, accs, _, _ = lax.while_loop(cond, body, (one, tuple(accs), tuple(lls), any_live(lls, one)))
    for qi in range(nq):
        o_ref[0, qi * KEY_BLOCK:(qi + 1) * KEY_BLOCK, :] = _unstack_heads(accs[qi]).astype(BF16)


def _attn_prompt(q, k, v, tq):
    bsz, t, d = q.shape
    pairs = d // LANES
    return pl.pallas_call(
        _attn_prompt_body,
        grid=(bsz, pairs, t // tq),
        in_specs=[
            pl.BlockSpec((1, tq, LANES), lambda b, p, i: (b, i, p)),
            pl.BlockSpec((1, t, LANES), lambda b, p, i: (b, 0, p)),
            pl.BlockSpec((1, t, LANES), lambda b, p, i: (b, 0, p)),
        ],
        out_specs=pl.BlockSpec((1, tq, LANES), lambda b, p, i: (b, i, p)),
        out_shape=jax.ShapeDtypeStruct((bsz, t, d), BF16),
        compiler_params=_params(("arbitrary", "arbitrary", "arbitrary")),
        name="attn_prompt",
    )(q, k, v)


def _attn_sample_body(q_ref, kn_ref, vn_ref, ck_ref, cv_ref, o_ref):
    tq = q_ref.shape[1]
    past_blocks = ck_ref.shape[1] // KEY_BLOCK
    qs = _stack_heads(q_ref[0])
    row = _row_in_head(2 * tq, KEY_BLOCK)
    col = lax.broadcasted_iota(jnp.int32, (2 * tq, KEY_BLOCK), 1)
    acc = jnp.zeros((2 * tq, LANES), F32)
    ll = jnp.zeros((2 * tq, 1), F32)
    acc, ll = _sb_block(qs, kn_ref[0].astype(BF16), vn_ref[0].astype(BF16), acc, ll, col < row)
    acc = _sb_older_blocks(qs, ck_ref, cv_ref, past_blocks - 1, acc, ll)
    o_ref[0] = _unstack_heads(acc).astype(BF16)


def _attn_sample(q, k_new, v_new, cache_k, cache_v):
    bsz, t, d = q.shape
    past = cache_k.shape[1]
    pairs = d // LANES
    return pl.pallas_call(
        _attn_sample_body,
        grid=(bsz, pairs),
        in_specs=[
            pl.BlockSpec((1, t, LANES), lambda b, p: (b, 0, p)),
            pl.BlockSpec((1, KEY_BLOCK, LANES), lambda b, p: (b, 0, p)),
            pl.BlockSpec((1, KEY_BLOCK, LANES), lambda b, p: (b, 0, p)),
            pl.BlockSpec((1, past, LANES), lambda b, p: (b, 0, p)),
            pl.BlockSpec((1, past, LANES), lambda b, p: (b, 0, p)),
        ],
        out_specs=pl.BlockSpec((1, t, LANES), lambda b, p: (b, 0, p)),
        out_shape=jax.ShapeDtypeStruct((bsz, t, d), BF16),
        compiler_params=_params(("arbitrary", "arbitrary")),
        name="attn_sample",
    )(q, k_new, v_new, cache_k, cache_v)


def _first_argmax(cur, iota_f, n):
    m = jnp.max(cur, axis=0, keepdims=True)
    i = jnp.min(jnp.where(cur == m, iota_f, float(n)), axis=0, keepdims=True)
    return m, i


def _post_attn_body(o_ref, gc_ref, sa_ref, x_ref, cnt_ref, lng_ref, lnb_ref, wao_ref, wo_ref, l1g_ref,
                    l1b_ref, wrh_ref, wrl_ref, rb_ref, wsgu_ref, wsd_ref,
                    x1_ref, part_ref, idx_ref, wts_ref, rank_ref, cnt_out_ref, run):
    tm = x_ref.shape[0]
    ff = wsd_ref.shape[0]

    @pl.when(pl.program_id(0) == 0)
    def _():
        run[...] = cnt_ref[...]

    h = _layer_norm(x_ref[...], lng_ref[...], lnb_ref[...])
    att = _dot(o_ref[...], wao_ref[...])
    merged = gc_ref[...].astype(F32) + sa_ref[...].astype(F32) * att
    mixed = _dot(merged.astype(BF16), wo_ref[...])
    x1 = _layer_norm(DEEPNORM_ALPHA * h + mixed, l1g_ref[...], l1b_ref[...])
    x1_ref[...] = x1

    x1h = x1.astype(BF16)
    gu = _dot(x1h, wsgu_ref[...])
    shared = _dot((_silu(gu[:, :ff]) * gu[:, ff:]).astype(BF16), wsd_ref[...])
    part_ref[...] = DEEPNORM_ALPHA * x1 + shared

    x1l = (x1 - x1h.astype(F32)).astype(BF16)
    logits = _dot_nt(wrh_ref[...], x1h) + _dot_nt(wrl_ref[...], x1h) + _dot_nt(wrh_ref[...], x1l)
    scores = _sigmoid(logits)
    choice = scores + rb_ref[...]
    neg = -jnp.inf

    giota = lax.broadcasted_iota(jnp.int32, (GROUP_SIZE, tm), 0).astype(F32)
    gs = []
    for g in range(N_GROUPS):
        blk = choice[g * GROUP_SIZE:(g + 1) * GROUP_SIZE, :]
        m1, i1 = _first_argmax(blk, giota, GROUP_SIZE)
        m2 = jnp.max(jnp.where(giota == i1, neg, blk), axis=0, keepdims=True)
        gs.append(m1 + m2)
    gscore = jnp.concatenate(gs, axis=0)

    g8 = lax.broadcasted_iota(jnp.int32, (N_GROUPS, tm), 0).astype(F32)
    gsel = jnp.zeros((N_GROUPS, tm), F32)
    cur = gscore
    for _ in range(TOPK_GROUPS):
        _, i = _first_argmax(cur, g8, N_GROUPS)
        hit = g8 == i
        gsel = jnp.where(hit, 1.0, gsel)
        cur = jnp.where(hit, neg, cur)
    emask = jnp.concatenate(
        [jnp.broadcast_to(gsel[g:g + 1, :], (GROUP_SIZE, tm)) for g in range(N_GROUPS)], axis=0)

    eiota = lax.broadcasted_iota(jnp.int32, (N_EXPERTS, tm), 0).astype(F32)
    cur = jnp.where(emask > 0.0, choice, neg)
    hits, sel_w = [], []
    for r in range(TOP_K):
        _, i = _first_argmax(cur, eiota, N_EXPERTS)
        hit = eiota == i
        hits.append(hit)
        sel_w.append(jnp.sum(jnp.where(hit, scores, 0.0), axis=0, keepdims=True))
        cur = jnp.where(hit, neg, cur)
        idx_ref[r:r + 1, :] = i.astype(jnp.int32)
    wsum = sel_w[0]
    for r in range(1, TOP_K):
        wsum = wsum + sel_w[r]
    for r in range(TOP_K):
        wts_ref[r:r + 1, :] = sel_w[r] / wsum * ROUTED_SCALE

    picked = jnp.zeros((N_EXPERTS, tm), F32)
    for r in range(TOP_K):
        picked = jnp.where(hits[r], 1.0, picked)
    picked_b = picked.astype(BF16)
    trow = lax.broadcasted_iota(jnp.int32, (tm, tm), 0)
    tcol = lax.broadcasted_iota(jnp.int32, (tm, tm), 1)
    earlier = jnp.where(trow < tcol, 1.0, 0.0).astype(BF16)
    before = _dot(picked_b, earlier) + run[:, 0:1]
    for r in range(TOP_K):
        rank_ref[r:r + 1, :] = jnp.sum(jnp.where(hits[r], before, 0.0), axis=0, keepdims=True).astype(jnp.int32)
    run[...] = run[...] + _dot(picked_b, jnp.ones((tm, LANES), BF16))
    cnt_out_ref[...] = run[...]


def _post_attn(o, gc, sa, x, cnt_in, ln_g, ln_b, w_ao, w_o, l1g, l1b, wr_hi, wr_lo, r_bias, ws_gu, ws_d, tm):
    n, d = x.shape
    ff = ws_d.shape[0]
    const = lambda shape: pl.BlockSpec(shape, lambda i: (0,) * len(shape))
    tile = lambda: pl.BlockSpec((tm, d), lambda i: (i, 0))
    small = lambda: pl.BlockSpec((TOP_K, tm), lambda i: (0, i))
    return pl.pallas_call(
        _post_attn_body,
        grid=(n // tm,),
        in_specs=[tile(), tile(), tile(), tile(), const((N_EXPERTS, LANES)),
                  const((1, d)), const((1, d)), const((d, d)), const((d, d)), const((1, d)), const((1, d)),
                  const((N_EXPERTS, d)), const((N_EXPERTS, d)), const((N_EXPERTS, 1)),
                  const((d, 2 * ff)), const((ff, d))],
        out_specs=[tile(), tile(), small(), small(), small(), const((N_EXPERTS, LANES))],
        out_shape=[
            jax.ShapeDtypeStruct((n, d), F32),
            jax.ShapeDtypeStruct((n, d), F32),
            jax.ShapeDtypeStruct((TOP_K, n), jnp.int32),
            jax.ShapeDtypeStruct((TOP_K, n), F32),
            jax.ShapeDtypeStruct((TOP_K, n), jnp.int32),
            jax.ShapeDtypeStruct((N_EXPERTS, LANES), F32),
        ],
        scratch_shapes=[pltpu.VMEM((N_EXPERTS, LANES), F32)],
        compiler_params=_params(("arbitrary",)),
        name="post_attn",
    )(o, gc, sa, x, cnt_in, ln_g, ln_b, w_ao, w_o, l1g, l1b, wr_hi, wr_lo, r_bias, ws_gu, ws_d)


def _dest_body(idx_ref, rank_ref, rs_ref, dest_ref):
    tm = idx_ref.shape[1]
    eiota = lax.broadcasted_iota(jnp.int32, (N_EXPERTS, tm), 0)
    rs = rs_ref[...]
    for r in range(TOP_K):
        base = jnp.sum(jnp.where(eiota == idx_ref[r:r + 1, :], rs, 0.0), axis=0, keepdims=True)
        dest_ref[r:r + 1, :] = base.astype(jnp.int32) + rank_ref[r:r + 1, :]


def _dest(idx, rank, row_start_f, tm):
    n = idx.shape[1]
    small = lambda: pl.BlockSpec((TOP_K, tm), lambda i: (0, i))
    return pl.pallas_call(
        _dest_body,
        grid=(n // tm,),
        in_specs=[small(), small(), pl.BlockSpec((N_EXPERTS, 1), lambda i: (0, 0))],
        out_specs=small(),
        out_shape=jax.ShapeDtypeStruct((TOP_K, n), jnp.int32),
        compiler_params=_params(("arbitrary",)),
        name="dest",
    )(idx, rank, row_start_f)


def _dispatch_body(meta_ref, lc_ref, dt_hbm, xp_ref, xsm_ref, xs_hbm, idx_smem, zbuf, isem, ssem, zsem):
    i = pl.program_id(0)
    n = pl.num_programs(0)
    slot = i % 2
    tm = xp_ref.shape[0]
    total_chunks = xs_hbm.shape[0] // GATHER_CHUNK

    def idx_copy(tile, s):
        return pltpu.make_async_copy(dt_hbm.at[pl.ds(tile, 1)], idx_smem.at[pl.ds(s, 1)], isem.at[s])

    @pl.when(i == 0)
    def _():
        idx_copy(0, 0).start()

    idx_copy(i, slot).wait()

    @pl.when(i + 1 < n)
    def _():
        idx_copy(i + 1, 1 - slot).start()

    def scatter(src_ref):
        cnt = src_ref.shape[0]
        for j in range(TOP_K):
            def body(t, carry):
                row = idx_smem[slot, j * tm + t]
                pltpu.make_async_copy(src_ref.at[pl.ds(t, 1)], xs_hbm.at[pl.ds(row, 1)], ssem).start()
                return carry
            lax.fori_loop(0, cnt, body, 0, unroll=8)
        for j in range(TOP_K):
            pltpu.make_async_copy(src_ref, xs_hbm.at[pl.ds(0, cnt)], ssem).wait()

    @pl.when(i == 0)
    def _():
        zbuf[...] = jnp.zeros(zbuf.shape, F32)

        def zero_chunk(c):
            return pltpu.make_async_copy(zbuf, xs_hbm.at[pl.ds(c * GATHER_CHUNK, GATHER_CHUNK)], zsem)

        def each_expert(fn):
            def body(e, carry):
                @pl.when(lc_ref[e] >= 0)
                def _():
                    fn(lc_ref[e])
                return carry
            lax.fori_loop(0, N_EXPERTS, body, 0)

        def each_tail(fn):
            def body(c, carry):
                fn(c)
                return carry
            lax.fori_loop(meta_ref[0], total_chunks, body, 0)

        each_expert(lambda c: zero_chunk(c).start())
        each_tail(lambda c: zero_chunk(c).start())
        each_expert(lambda c: zero_chunk(c).wait())
        each_tail(lambda c: zero_chunk(c).wait())

    @pl.when(i < n - 1)
    def _():
        scatter(xp_ref)

    @pl.when(i == n - 1)
    def _():
        scatter(xsm_ref)


def _dispatch(meta, last_chunk, tiles, x1p, x1s, n_rows, tm):
    n_p, d = x1p.shape
    n_s = x1s.shape[0]
    n_pt = n_p // tm
    grid_spec = pltpu.PrefetchScalarGridSpec(
        num_scalar_prefetch=2,
        grid=(tiles.shape[0],),
        in_specs=[
            pl.BlockSpec(memory_space=pl.ANY),
            pl.BlockSpec((tm, d), lambda i, *_: (jnp.minimum(i, n_pt - 1), 0)),
            pl.BlockSpec((n_s, d), lambda i, *_: (0, 0)),
        ],
        out_specs=pl.BlockSpec(memory_space=pl.ANY),
        scratch_shapes=[
            pltpu.SMEM((2, TOP_K * tm), jnp.int32),
            pltpu.VMEM((GATHER_CHUNK, d), F32),
            pltpu.SemaphoreType.DMA((2,)),
            pltpu.SemaphoreType.DMA(()),
            pltpu.SemaphoreType.DMA(()),
        ],
    )
    return pl.pallas_call(
        _dispatch_body,
        grid_spec=grid_spec,
        out_shape=jax.ShapeDtypeStruct((n_rows, d), F32),
        compiler_params=_dma_params(("arbitrary",)),
        name="dispatch",
    )(meta, last_chunk, tiles, x1p, x1s)


def _experts_body(be_ref, bx_ref, bn_ref, bz_ref, xs_hbm, wg_ref, wu_ref, wd_ref, ys_hbm,
                  xbuf, obuf, gsem, osem, wgu_bf, wd_bf):
    b = pl.program_id(0)
    nb = pl.num_programs(0)
    slot = b % 2
    ff = wd_ref.shape[1]

    def in_copy(blk, s, c):
        src = xs_hbm.at[pl.ds((bx_ref[blk] + c) * GATHER_CHUNK, GATHER_CHUNK)]
        return pltpu.make_async_copy(src, xbuf.at[s, pl.ds(c * GATHER_CHUNK, GATHER_CHUNK)], gsem.at[s])

    def out_copy(blk, s, c):
        dst = ys_hbm.at[pl.ds((bx_ref[blk] + c) * GATHER_CHUNK, GATHER_CHUNK)]
        return pltpu.make_async_copy(obuf.at[s, pl.ds(c * GATHER_CHUNK, GATHER_CHUNK)], dst, osem.at[s])

    def for_chunks(blk, count, fn):
        for c in range(BLOCK_CHUNKS):
            @pl.when(c < count)
            def _():
                fn(c)

    def n_in(blk):
        return jnp.where(bz_ref[blk] == 0, bn_ref[blk], 0)

    @pl.when(b == 0)
    def _():
        for_chunks(0, n_in(0), lambda c: in_copy(0, 0, c).start())

    @pl.when(b + 1 < nb)
    def _():
        for_chunks(b + 1, n_in(b + 1), lambda c: in_copy(b + 1, 1 - slot, c).start())

    @pl.when(b >= 2)
    def _():
        for_chunks(b - 2, bn_ref[b - 2], lambda c: out_copy(b - 2, slot, c).wait())

    for_chunks(b, n_in(b), lambda c: in_copy(b, slot, c).wait())

    e_now = be_ref[b]
    e_prev = be_ref[jnp.maximum(b - 1, 0)]

    @pl.when(jnp.logical_or(b == 0, e_now != e_prev))
    def _():
        wgu_bf[:, 0:ff] = wg_ref[0].astype(BF16)
        wgu_bf[:, ff:2 * ff] = wu_ref[0].astype(BF16)
        wd_bf[...] = wd_ref[0].astype(BF16)

    for nch in range(1, BLOCK_CHUNKS + 1):
        @pl.when(n_in(b) == nch)
        def _():
            rows = nch * GATHER_CHUNK
            gu = _dot(xbuf[slot, 0:rows, :].astype(BF16), wgu_bf[...])
            hid = _silu(gu[:, :ff]) * gu[:, ff:]
            obuf[slot, 0:rows, :] = _dot(hid.astype(BF16), wd_bf[...])

    @pl.when(bz_ref[b] != 0)
    def _():
        obuf[slot] = jnp.zeros(obuf.shape[1:], F32)

    for_chunks(b, bn_ref[b], lambda c: out_copy(b, slot, c).start())

    @pl.when(b == nb - 1)
    def _():
        for_chunks(b, bn_ref[b], lambda c: out_copy(b, slot, c).wait())

        @pl.when(nb > 1)
        def _():
            for_chunks(b - 1, bn_ref[b - 1], lambda c: out_copy(b - 1, 1 - slot, c).wait())


def _experts(block_e, block_x, block_n, block_z, xs, we_gate, we_up, we_down):
    nb = block_e.shape[0]
    n_rows, d = xs.shape
    ff = we_gate.shape[2]
    rows = BLOCK_CHUNKS * GATHER_CHUNK
    grid_spec = pltpu.PrefetchScalarGridSpec(
        num_scalar_prefetch=4,
        grid=(nb,),
        in_specs=[
            pl.BlockSpec(memory_space=pl.ANY),
            pl.BlockSpec((1, d, ff), lambda b, be, *_: (be[b], 0, 0)),
            pl.BlockSpec((1, d, ff), lambda b, be, *_: (be[b], 0, 0)),
            pl.BlockSpec((1, ff, d), lambda b, be, *_: (be[b], 0, 0)),
        ],
        out_specs=pl.BlockSpec(memory_space=pl.ANY),
        scratch_shapes=[
            pltpu.VMEM((2, rows, d), F32),
            pltpu.VMEM((2, rows, d), F32),
            pltpu.SemaphoreType.DMA((2,)),
            pltpu.SemaphoreType.DMA((2,)),
            pltpu.VMEM((d, 2 * ff), BF16),
            pltpu.VMEM((ff, d), BF16),
        ],
    )
    return pl.pallas_call(
        _experts_body,
        grid_spec=grid_spec,
        out_shape=jax.ShapeDtypeStruct((n_rows, d), F32),
        compiler_params=_dma_params(("arbitrary",)),
        name="experts",
    )(block_e, block_x, block_n, block_z, xs, we_gate, we_up, we_down)


def _combine_body(dt_hbm, ys_hbm, part_ref, w_ref, g_ref, b_ref, y_ref, idx_smem, gbuf, isem, gsem):
    i = pl.program_id(0)
    n = pl.num_programs(0)
    slot = i % 2
    tm = part_ref.shape[0]

    def idx_copy(tile, s):
        return pltpu.make_async_copy(dt_hbm.at[pl.ds(tile, 1)], idx_smem.at[pl.ds(s, 1)], isem.at[s])

    def issue_rows(s):
        for j in range(TOP_K):
            def body(t, carry):
                row = idx_smem[s, j * tm + t]
                pltpu.make_async_copy(ys_hbm.at[pl.ds(row, 1)], gbuf.at[s, j, pl.ds(t, 1)], gsem.at[s]).start()
                return carry
            lax.fori_loop(0, tm, body, 0, unroll=8)

    def wait_rows(s):
        for j in range(TOP_K):
            pltpu.make_async_copy(ys_hbm.at[pl.ds(0, tm)], gbuf.at[s, j], gsem.at[s]).wait()

    @pl.when(i == 0)
    def _():
        first = idx_copy(0, 0)
        first.start()
        first.wait()
        issue_rows(0)

        @pl.when(n > 1)
        def _():
            idx_copy(1, 1).start()

    @pl.when(i + 1 < n)
    def _():
        idx_copy(i + 1, 1 - slot).wait()
        issue_rows(1 - slot)

    wait_rows(slot)

    @pl.when(i + 2 < n)
    def _():
        idx_copy(i + 2, slot).start()

    f = part_ref[...]
    for j in range(TOP_K):
        f = f + w_ref[:, j:j + 1] * gbuf[slot, j]
    y_ref[...] = _layer_norm(f, g_ref[...], b_ref[...])


def _combine(dest_tiles, ys, part, wts, ln_g, ln_b, tm):
    n, d = part.shape
    return pl.pallas_call(
        _combine_body,
        grid=(n // tm,),
        in_specs=[
            pl.BlockSpec(memory_space=pl.ANY),
            pl.BlockSpec(memory_space=pl.ANY),
            pl.BlockSpec((tm, d), lambda i: (i, 0)),
            pl.BlockSpec((tm, TOP_K), lambda i: (i, 0)),
            pl.BlockSpec((1, d), lambda i: (0, 0)),
            pl.BlockSpec((1, d), lambda i: (0, 0)),
        ],
        out_specs=pl.BlockSpec((tm, d), lambda i: (i, 0)),
        out_shape=jax.ShapeDtypeStruct((n, d), F32),
        scratch_shapes=[
            pltpu.SMEM((2, TOP_K * tm), jnp.int32),
            pltpu.VMEM((2, TOP_K, tm, d), F32),
            pltpu.SemaphoreType.DMA((2,)),
            pltpu.SemaphoreType.DMA((2,)),
        ],
        compiler_params=_dma_params(("arbitrary",)),
        name="combine",
    )(dest_tiles, ys, part, wts, ln_g, ln_b)


def _dest_tiles(dest, tm):
    k, n = dest.shape
    return dest.reshape(k, n // tm, tm).transpose(1, 0, 2).reshape(n // tm, k * tm)


def kernel(x_prompt, x_sample, cache_k, cache_v, state_conv, ln_in_g, ln_in_b, w_in, b_in, conv_w, conv_b,
           conv_ln_g, conv_ln_b, w_conv_out, w_attn_out, w_out, ln1_g, ln1_b, w_router, router_bias,
           we_gate, we_up, we_down, ws_gate, ws_up, ws_down, ln2_g, ln2_b):
    bp, tp, d = x_prompt.shape
    bs, ts, _ = x_sample.shape
    past = cache_k.shape[2]
    assert w_in.shape[0] == DEPTH and d == N_HEADS * HEAD_DIM and conv_w.shape[2] == d
    assert tp % 1024 == 0 and ts <= KEY_BLOCK and ts % 16 == 0 and past % KEY_BLOCK == 0
    n_p, n_s = bp * tp, bs * ts
    assert n_s % LANES == 0

    row = lambda a: a.reshape(1, -1)
    lng, lnb = row(ln_in_g), row(ln_in_b)
    w_in_b = w_in[0].astype(BF16)
    w_co_b = w_conv_out[0].astype(BF16)
    conv_args = (lng, lnb, w_in_b, b_in, conv_w[0], conv_b, conv_ln_g, conv_ln_b, w_co_b)

    qp, kp, vp, gcp, sap, csp = _proj_conv(x_prompt, jnp.zeros((bp, CONV_WIDTH - 1, d), F32), *conv_args, tm=256)
    op = _attn_prompt(qp, kp, vp, tq=512)

    qs, ks, vs, gcs, sas, css = _proj_conv(x_sample, state_conv[0], *conv_args, tm=ts)
    pad_new = lambda a: jnp.pad(a, ((0, 0), (0, KEY_BLOCK - ts), (0, 0)))
    os_ = _attn_sample(qs, pad_new(ks), pad_new(vs), cache_k[0].reshape(bs, past, d), cache_v[0].reshape(bs, past, d))

    wr = w_router[0].T
    wr_hi = wr.astype(BF16)
    wr_lo = (wr - wr_hi.astype(F32)).astype(BF16)
    post_args = (lng, lnb, w_attn_out[0].astype(BF16), w_out[0].astype(BF16), ln1_g, ln1_b, wr_hi, wr_lo,
                 router_bias.reshape(N_EXPERTS, 1),
                 jnp.concatenate([ws_gate[0], ws_up[0]], axis=1).astype(BF16), ws_down[0].astype(BF16))
    flat = lambda a, n: a.reshape(n, d)
    zero_cnt = jnp.zeros((N_EXPERTS, LANES), F32)
    x1p, partp, idxp, wtp, rankp, cnt_p = _post_attn(
        flat(op, n_p), flat(gcp, n_p), flat(sap, n_p), flat(x_prompt, n_p), zero_cnt, *post_args, tm=256)
    x1s, parts, idxs, wts, ranks, cnt = _post_attn(
        flat(os_, n_s), flat(gcs, n_s), flat(sas, n_s), flat(x_sample, n_s), cnt_p, *post_args, tm=n_s)

    i32 = jnp.int32
    n_tok = n_p + n_s
    total_chunks = (n_tok * TOP_K + N_EXPERTS * (GATHER_CHUNK - 1) + GATHER_CHUNK - 1) // GATHER_CHUNK
    n_blocks = total_chunks // BLOCK_CHUNKS + N_EXPERTS
    counts = cnt[:, 0].astype(i32)
    chunks = (counts + GATHER_CHUNK - 1) // GATHER_CHUNK
    chunk_end = jnp.cumsum(chunks)
    chunk_start = chunk_end - chunks
    row_start = chunk_start * GATHER_CHUNK
    used_chunks = chunk_end[-1]
    blocks = (chunks + BLOCK_CHUNKS - 1) // BLOCK_CHUNKS
    blk_end = jnp.cumsum(blocks)
    blk_start = blk_end - blocks
    bid = jnp.arange(n_blocks, dtype=i32)
    block_e = jnp.minimum(jnp.sum(bid[:, None] >= blk_end[None, :], axis=1), N_EXPERTS - 1).astype(i32)
    k_in_e = bid - blk_start[block_e]
    is_tail = bid >= blk_end[-1]
    tail_x = used_chunks + BLOCK_CHUNKS * (bid - blk_end[-1])
    block_x = jnp.where(is_tail, tail_x, chunk_start[block_e] + BLOCK_CHUNKS * k_in_e)
    block_n = jnp.where(is_tail, total_chunks - tail_x, chunks[block_e] - BLOCK_CHUNKS * k_in_e)
    block_n = jnp.clip(block_n, 0, BLOCK_CHUNKS).astype(i32)
    block_x = jnp.where(block_n > 0, block_x, 0).astype(i32)
    block_z = is_tail.astype(i32)
    meta = jnp.stack([used_chunks, used_chunks]).astype(i32)
    last_chunk = jnp.where(chunks > 0, chunk_end - 1, -1).astype(i32)

    rs_f = row_start.astype(F32).reshape(N_EXPERTS, 1)
    dest_p = _dest(idxp, rankp, rs_f, tm=1024)
    dest_s = _dest(idxs, ranks, rs_f, tm=n_s)

    tm_d = 256
    pad_s = jnp.pad(dest_s.reshape(TOP_K, 1, n_s), ((0, 0), (0, 0), (0, tm_d - n_s))).reshape(1, TOP_K * tm_d)
    disp_tiles = jnp.concatenate([_dest_tiles(dest_p, tm_d), pad_s], axis=0)
    xs = _dispatch(meta, last_chunk, disp_tiles, x1p, x1s, total_chunks * GATHER_CHUNK, tm=tm_d)
    ys = _experts(block_e, block_x, block_n, block_z, xs, we_gate[0], we_up[0], we_down[0])

    yp = _combine(_dest_tiles(dest_p, 128), ys, partp, wtp.T, ln2_g, ln2_b, tm=128)
    ysm = _combine(_dest_tiles(dest_s, n_s), ys, parts, wts.T, ln2_g, ln2_b, tm=n_s)

    heads = lambda a, b, t: a.reshape(1, b, t, N_HEADS, HEAD_DIM)
    return (yp.reshape(bp, tp, d), ysm.reshape(bs, ts, d),
            heads(kp, bp, tp), heads(vp, bp, tp), csp[None],
            heads(ks, bs, ts), heads(vs, bs, ts), css[None])
```

```python
import functools

import jax
import jax.numpy as jnp
from jax import lax
from jax.experimental import pallas as pl
from jax.experimental.pallas import tpu as pltpu

F32 = jnp.float32
BF16 = jnp.bfloat16

N_HEADS = 16
HEAD_DIM = 64
CONV_WIDTH = 31
N_EXPERTS = 256
TOP_K = 8
N_GROUPS = 8
TOPK_GROUPS = 4
GROUP_SIZE = N_EXPERTS // N_GROUPS
ROUTED_SCALE = 2.5
LN_EPS = 1e-5
DEPTH = 1
DEEPNORM_ALPHA = (2 * DEPTH) ** 0.25
ATTN_SCALE = HEAD_DIM ** -0.5

LANES = 128
SUBLANES = 8
VMEM_LIMIT_BYTES = 56 * 1024 * 1024

KEY_BLOCK = 128
HIST_PAD = 32
GATHER_CHUNK = 128
BLOCK_CHUNKS = 4
LOG_DEAD = -110.0


def _layer_norm(x, g, b):
    mu = jnp.mean(x, axis=-1, keepdims=True)
    xc = x - mu
    var = jnp.mean(xc * xc, axis=-1, keepdims=True)
    return xc * lax.rsqrt(var + LN_EPS) * g + b


def _sigmoid(x):
    return 1.0 / (1.0 + jnp.exp(-x))


def _silu(x):
    return x * _sigmoid(x)


def _dot(a, b):
    return jnp.dot(a, b, preferred_element_type=F32)


def _dot_nt(a, b):
    return lax.dot_general(a, b, (((1,), (1,)), ((), ())), preferred_element_type=F32)


def _params(sem):
    return pltpu.CompilerParams(dimension_semantics=sem, vmem_limit_bytes=VMEM_LIMIT_BYTES)


def _dma_params(sem):
    return pltpu.CompilerParams(dimension_semantics=sem, vmem_limit_bytes=VMEM_LIMIT_BYTES,
                                disable_bounds_checks=True)


def _proj_conv_body(x_ref, hist_ref, lng_ref, lnb_ref, win_ref, bin_ref, cw_ref, cb_ref, clg_ref,
                    clb_ref, wco_ref, q_ref, k_ref, v_ref, gc_ref, sa_ref, cs_ref, ubuf, cbuf):
    tm = x_ref.shape[1]
    d = x_ref.shape[2]
    hist = CONV_WIDTH - 1
    lead = HIST_PAD - hist

    @pl.when(pl.program_id(1) == 0)
    def _():
        ubuf[0:lead, :] = jnp.zeros((lead, d), F32)
        ubuf[lead:HIST_PAD, :] = hist_ref[0]

    hb = _layer_norm(x_ref[0], lng_ref[...], lnb_ref[...]).astype(BF16)

    def proj(i):
        return _dot(hb, win_ref[:, i * d:(i + 1) * d]) + bin_ref[:, i * d:(i + 1) * d]

    ubuf[HIST_PAD:HIST_PAD + tm, :] = proj(0) * _sigmoid(proj(1))
    q_ref[0] = (proj(2) * ATTN_SCALE).astype(BF16)
    k_ref[0] = proj(3)
    v_ref[0] = proj(4)
    sa_ref[0] = _sigmoid(proj(6)).astype(BF16)

    rows = min(tm, 32)
    for r0 in range(0, tm, rows):
        acc = jnp.broadcast_to(cb_ref[...], (rows, d))
        for kk in range(CONV_WIDTH):
            acc = acc + cw_ref[kk:kk + 1, :] * ubuf[lead + kk + r0:lead + kk + r0 + rows, :]
        cbuf[r0:r0 + rows, :] = _silu(_layer_norm(acc, clg_ref[...], clb_ref[...])).astype(BF16)
    gc_ref[0] = (_sigmoid(proj(5)) * _dot(cbuf[...], wco_ref[...])).astype(BF16)

    cs_ref[0] = ubuf[tm + lead:tm + HIST_PAD, :]
    ubuf[0:HIST_PAD, :] = ubuf[tm:tm + HIST_PAD, :]


def _proj_conv(x, hist, ln_g, ln_b, w_in, b_in, conv_w, conv_b, cln_g, cln_b, w_co, tm):
    bsz, t, d = x.shape
    pw = w_in.shape[1]
    const = lambda shape: pl.BlockSpec(shape, lambda b, i: (0,) * len(shape))
    tile = lambda: pl.BlockSpec((1, tm, d), lambda b, i: (b, i, 0))
    return pl.pallas_call(
        _proj_conv_body,
        grid=(bsz, t // tm),
        in_specs=[
            tile(),
            pl.BlockSpec((1, CONV_WIDTH - 1, d), lambda b, i: (b, 0, 0)),
            const((1, d)), const((1, d)), const((d, pw)), const((1, pw)),
            const((CONV_WIDTH, d)), const((1, d)), const((1, d)), const((1, d)), const((d, d)),
        ],
        out_specs=[tile(), tile(), tile(), tile(), tile(),
                   pl.BlockSpec((1, CONV_WIDTH - 1, d), lambda b, i: (b, 0, 0))],
        out_shape=[
            jax.ShapeDtypeStruct((bsz, t, d), BF16),
            jax.ShapeDtypeStruct((bsz, t, d), F32),
            jax.ShapeDtypeStruct((bsz, t, d), F32),
            jax.ShapeDtypeStruct((bsz, t, d), BF16),
            jax.ShapeDtypeStruct((bsz, t, d), BF16),
            jax.ShapeDtypeStruct((bsz, CONV_WIDTH - 1, d), F32),
        ],
        scratch_shapes=[pltpu.VMEM((HIST_PAD + tm, d), F32), pltpu.VMEM((tm, d), BF16)],
        compiler_params=_params(("arbitrary", "arbitrary")),
        name="proj_conv",
    )(x, hist, ln_g, ln_b, w_in, b_in, conv_w, conv_b, cln_g, cln_b, w_co)


def _sb_block(qs, kb, vb, acc, ll, vis):
    kbn = kb.shape[0]
    z = _dot_nt(qs, kb)
    lk = -(jnp.maximum(z, 0.0) + jnp.log1p(jnp.exp(-jnp.abs(z))))
    if vis is not None:
        lk = jnp.where(vis, lk, 0.0)
    row = lax.broadcasted_iota(jnp.int32, (kbn, kbn), 0)
    col = lax.broadcasted_iota(jnp.int32, (kbn, kbn), 1)
    suffix = jnp.where(row >= col, 1.0, 0.0).astype(BF16)
    hi = lk.astype(BF16)
    lo = (lk - hi.astype(F32)).astype(BF16)
    inb = _dot(hi, suffix) + _dot(lo, suffix)
    arg = z + inb + ll
    if vis is not None:
        arg = jnp.where(vis, arg, -jnp.inf)
    acc = acc + _dot(jnp.exp(arg).astype(BF16), vb)
    return acc, ll + inb[:, 0:1]


def _sb_older_blocks(qs, k_ref, v_ref, first, acc, ll):
    def live(ll_):
        return (jnp.max(ll_, axis=0, keepdims=True)[0, 0] > LOG_DEAD).astype(jnp.int32)

    def cond(c):
        return jnp.logical_and(c[0] >= 0, c[3] > 0)

    def body(c):
        j, acc_, ll_, _ = c
        start = pl.multiple_of(j * KEY_BLOCK, KEY_BLOCK)
        kb = k_ref[0, pl.ds(start, KEY_BLOCK), :].astype(BF16)
        vb = v_ref[0, pl.ds(start, KEY_BLOCK), :].astype(BF16)
        acc_, ll_ = _sb_block(qs, kb, vb, acc_, ll_, None)
        return j - 1, acc_, ll_, live(ll_)

    _, acc, _, _ = lax.while_loop(cond, body, (first, acc, ll, live(ll)))
    return acc


def _stack_heads(q):
    lane = lax.broadcasted_iota(jnp.int32, q.shape, 1)
    zero = jnp.zeros_like(q)
    return jnp.concatenate([jnp.where(lane < HEAD_DIM, q, zero), jnp.where(lane >= HEAD_DIM, q, zero)], axis=0)


def _row_in_head(rows, cols):
    r = lax.broadcasted_iota(jnp.int32, (rows, cols), 0)
    return jnp.where(r >= rows // 2, r - rows // 2, r)


def _unstack_heads(acc):
    r = acc.shape[0] // 2
    lane = lax.broadcasted_iota(jnp.int32, (r, LANES), 1)
    return jnp.where(lane < HEAD_DIM, acc[:r], acc[r:])


def _attn_prompt_body(q_ref, k_ref, v_ref, o_ref):
    tq = q_ref.shape[1]
    nq = tq // KEY_BLOCK
    g0 = pl.program_id(2) * nq
    row = _row_in_head(2 * KEY_BLOCK, KEY_BLOCK)
    col = lax.broadcasted_iota(jnp.int32, (2 * KEY_BLOCK, KEY_BLOCK), 1)
    vis = col < row

    def kv_block(j):
        start = pl.multiple_of(j * KEY_BLOCK, KEY_BLOCK)
        return (k_ref[0, pl.ds(start, KEY_BLOCK), :].astype(BF16),
                v_ref[0, pl.ds(start, KEY_BLOCK), :].astype(BF16))

    def any_live(lls, off):
        m = jnp.full((2 * KEY_BLOCK, 1), -jnp.inf, F32)
        for qi in range(nq):
            m = jnp.maximum(m, jnp.where(g0 + qi - off >= 0, lls[qi], -jnp.inf))
        return (jnp.max(m, axis=0, keepdims=True)[0, 0] > LOG_DEAD).astype(jnp.int32)

    qss, accs, lls = [], [], []
    for qi in range(nq):
        qs = _stack_heads(q_ref[0, qi * KEY_BLOCK:(qi + 1) * KEY_BLOCK, :])
        kb, vb = kv_block(g0 + qi)
        acc, ll = _sb_block(qs, kb, vb, jnp.zeros((2 * KEY_BLOCK, LANES), F32),
                            jnp.zeros((2 * KEY_BLOCK, 1), F32), vis)
        qss.append(qs)
        accs.append(acc)
        lls.append(ll)

    def cond(c):
        return c[3] > 0

    def body(c):
        off, accs_, lls_, _ = c
        new_accs, new_lls = [], []
        for qi in range(nq):
            j = g0 + qi - off
            kb, vb = kv_block(jnp.maximum(j, 0))
            ll_in = jnp.where(j >= 0, lls_[qi], -jnp.inf)
            acc, ll = _sb_block(qss[qi], kb, vb, accs_[qi], ll_in, None)
            new_accs.append(acc)
            new_lls.append(ll)
        return off + 1, tuple(new_accs), tuple(new_lls), any_live(new_lls, off + 1)

    one = jnp.int32(1)
    _, accs, _, _ = lax.while_loop(cond, body, (one, tuple(accs), tuple(lls), any_live(lls, one)))
    for qi in range(nq):
        o_ref[0, qi * KEY_BLOCK:(qi + 1) * KEY_BLOCK, :] = _unstack_heads(accs[qi]).astype(BF16)


def _attn_prompt(q, k, v, tq):
    bsz, t, d = q.shape
    pairs = d // LANES
    return pl.pallas_call(
        _attn_prompt_body,
        grid=(bsz, pairs, t // tq),
        in_specs=[
            pl.BlockSpec((1, tq, LANES), lambda b, p, i: (b, i, p)),
            pl.BlockSpec((1, t, LANES), lambda b, p, i: (b, 0, p)),
            pl.BlockSpec((1, t, LANES), lambda b, p, i: (b, 0, p)),
        ],
        out_specs=pl.BlockSpec((1, tq, LANES), lambda b, p, i: (b, i, p)),
        out_shape=jax.ShapeDtypeStruct((bsz, t, d), BF16),
        compiler_params=_params(("arbitrary", "arbitrary", "arbitrary")),
        name="attn_prompt",
    )(q, k, v)


def _attn_sample_body(q_ref, kn_ref, vn_ref, ck_ref, cv_ref, o_ref):
    tq = q_ref.shape[1]
    past_blocks = ck_ref.shape[1] // KEY_BLOCK
    qs = _stack_heads(q_ref[0])
    row = _row_in_head(2 * tq, KEY_BLOCK)
    col = lax.broadcasted_iota(jnp.int32, (2 * tq, KEY_BLOCK), 1)
    acc = jnp.zeros((2 * tq, LANES), F32)
    ll = jnp.zeros((2 * tq, 1), F32)
    acc, ll = _sb_block(qs, kn_ref[0].astype(BF16), vn_ref[0].astype(BF16), acc, ll, col < row)
    acc = _sb_older_blocks(qs, ck_ref, cv_ref, past_blocks - 1, acc, ll)
    o_ref[0] = _unstack_heads(acc).astype(BF16)


def _attn_sample(q, k_new, v_new, cache_k, cache_v):
    bsz, t, d = q.shape
    past = cache_k.shape[1]
    pairs = d // LANES
    return pl.pallas_call(
        _attn_sample_body,
        grid=(bsz, pairs),
        in_specs=[
            pl.BlockSpec((1, t, LANES), lambda b, p: (b, 0, p)),
            pl.BlockSpec((1, KEY_BLOCK, LANES), lambda b, p: (b, 0, p)),
            pl.BlockSpec((1, KEY_BLOCK, LANES), lambda b, p: (b, 0, p)),
            pl.BlockSpec((1, past, LANES), lambda b, p: (b, 0, p)),
            pl.BlockSpec((1, past, LANES), lambda b, p: (b, 0, p)),
        ],
        out_specs=pl.BlockSpec((1, t, LANES), lambda b, p: (b, 0, p)),
        out_shape=jax.ShapeDtypeStruct((bsz, t, d), BF16),
        compiler_params=_params(("arbitrary", "arbitrary")),
        name="attn_sample",
    )(q, k_new, v_new, cache_k, cache_v)


def _first_argmax(cur, iota_f, n):
    m = jnp.max(cur, axis=0, keepdims=True)
    i = jnp.min(jnp.where(cur == m, iota_f, float(n)), axis=0, keepdims=True)
    return m, i


def _store_row_tiles(ref, val):
    for s in range(val.shape[1] // LANES):
        ref[:, s, :] = val[:, s * LANES:(s + 1) * LANES]


def _post_attn_body(o_ref, gc_ref, sa_ref, x_ref, cnt_ref, lng_ref, lnb_ref, wao_ref, wo_ref, l1g_ref,
                    l1b_ref, wrh_ref, wrl_ref, rb_ref, wsgu_ref, wsd_ref,
                    x1_ref, part_ref, idx_ref, wts_ref, rank_ref, cnt_out_ref, run):
    tm = x_ref.shape[0]
    ff = wsd_ref.shape[0]

    @pl.when(pl.program_id(0) == 0)
    def _():
        run[...] = cnt_ref[...]

    h = _layer_norm(x_ref[...], lng_ref[...], lnb_ref[...])
    att = _dot(o_ref[...], wao_ref[...])
    merged = gc_ref[...].astype(F32) + sa_ref[...].astype(F32) * att
    mixed = _dot(merged.astype(BF16), wo_ref[...])
    x1 = _layer_norm(DEEPNORM_ALPHA * h + mixed, l1g_ref[...], l1b_ref[...])
    _store_row_tiles(x1_ref, x1)

    x1h = x1.astype(BF16)
    gu = _dot(x1h, wsgu_ref[...])
    shared = _dot((_silu(gu[:, :ff]) * gu[:, ff:]).astype(BF16), wsd_ref[...])
    _store_row_tiles(part_ref, DEEPNORM_ALPHA * x1 + shared)

    x1l = (x1 - x1h.astype(F32)).astype(BF16)
    logits = _dot_nt(wrh_ref[...], x1h) + _dot_nt(wrl_ref[...], x1h) + _dot_nt(wrh_ref[...], x1l)
    scores = _sigmoid(logits)
    choice = scores + rb_ref[...]
    neg = -jnp.inf

    giota = lax.broadcasted_iota(jnp.int32, (GROUP_SIZE, tm), 0).astype(F32)
    gs = []
    for g in range(N_GROUPS):
        blk = choice[g * GROUP_SIZE:(g + 1) * GROUP_SIZE, :]
        m1, i1 = _first_argmax(blk, giota, GROUP_SIZE)
        m2 = jnp.max(jnp.where(giota == i1, neg, blk), axis=0, keepdims=True)
        gs.append(m1 + m2)
    gscore = jnp.concatenate(gs, axis=0)

    g8 = lax.broadcasted_iota(jnp.int32, (N_GROUPS, tm), 0).astype(F32)
    gsel = jnp.zeros((N_GROUPS, tm), F32)
    cur = gscore
    for _ in range(TOPK_GROUPS):
        _, i = _first_argmax(cur, g8, N_GROUPS)
        hit = g8 == i
        gsel = jnp.where(hit, 1.0, gsel)
        cur = jnp.where(hit, neg, cur)
    emask = jnp.concatenate(
        [jnp.broadcast_to(gsel[g:g + 1, :], (GROUP_SIZE, tm)) for g in range(N_GROUPS)], axis=0)

    eiota = lax.broadcasted_iota(jnp.int32, (N_EXPERTS, tm), 0).astype(F32)
    cur = jnp.where(emask > 0.0, choice, neg)
    hits, sel_w = [], []
    for r in range(TOP_K):
        _, i = _first_argmax(cur, eiota, N_EXPERTS)
        hit = eiota == i
        hits.append(hit)
        sel_w.append(jnp.sum(jnp.where(hit, scores, 0.0), axis=0, keepdims=True))
        cur = jnp.where(hit, neg, cur)
        idx_ref[r:r + 1, :] = i.astype(jnp.int32)
    wsum = sel_w[0]
    for r in range(1, TOP_K):
        wsum = wsum + sel_w[r]
    for r in range(TOP_K):
        wts_ref[r:r + 1, :] = sel_w[r] / wsum * ROUTED_SCALE

    picked = jnp.zeros((N_EXPERTS, tm), F32)
    for r in range(TOP_K):
        picked = jnp.where(hits[r], 1.0, picked)
    picked_b = picked.astype(BF16)
    trow = lax.broadcasted_iota(jnp.int32, (tm, tm), 0)
    tcol = lax.broadcasted_iota(jnp.int32, (tm, tm), 1)
    earlier = jnp.where(trow < tcol, 1.0, 0.0).astype(BF16)
    before = _dot(picked_b, earlier) + run[:, 0:1]
    for r in range(TOP_K):
        rank_ref[r:r + 1, :] = jnp.sum(jnp.where(hits[r], before, 0.0), axis=0, keepdims=True).astype(jnp.int32)
    run[...] = run[...] + _dot(picked_b, jnp.ones((tm, LANES), BF16))
    cnt_out_ref[...] = run[...]


def _post_attn(o, gc, sa, x, cnt_in, ln_g, ln_b, w_ao, w_o, l1g, l1b, wr_hi, wr_lo, r_bias, ws_gu, ws_d, tm):
    n, d = x.shape
    ff = ws_d.shape[0]
    const = lambda shape: pl.BlockSpec(shape, lambda i: (0,) * len(shape))
    tile = lambda: pl.BlockSpec((tm, d), lambda i: (i, 0))
    small = lambda: pl.BlockSpec((TOP_K, tm), lambda i: (0, i))
    row_tiles = lambda: pl.BlockSpec((tm, d // LANES, LANES), lambda i: (i, 0, 0))
    return pl.pallas_call(
        _post_attn_body,
        grid=(n // tm,),
        in_specs=[tile(), tile(), tile(), tile(), const((N_EXPERTS, LANES)),
                  const((1, d)), const((1, d)), const((d, d)), const((d, d)), const((1, d)), const((1, d)),
                  const((N_EXPERTS, d)), const((N_EXPERTS, d)), const((N_EXPERTS, 1)),
                  const((d, 2 * ff)), const((ff, d))],
        out_specs=[row_tiles(), row_tiles(), small(), small(), small(), const((N_EXPERTS, LANES))],
        out_shape=[
            jax.ShapeDtypeStruct((n, d // LANES, LANES), F32),
            jax.ShapeDtypeStruct((n, d // LANES, LANES), F32),
            jax.ShapeDtypeStruct((TOP_K, n), jnp.int32),
            jax.ShapeDtypeStruct((TOP_K, n), F32),
            jax.ShapeDtypeStruct((TOP_K, n), jnp.int32),
            jax.ShapeDtypeStruct((N_EXPERTS, LANES), F32),
        ],
        scratch_shapes=[pltpu.VMEM((N_EXPERTS, LANES), F32)],
        compiler_params=_params(("arbitrary",)),
        name="post_attn",
    )(o, gc, sa, x, cnt_in, ln_g, ln_b, w_ao, w_o, l1g, l1b, wr_hi, wr_lo, r_bias, ws_gu, ws_d)


def _dest_body(idx_ref, rank_ref, rs_ref, dest_ref):
    tm = idx_ref.shape[1]
    eiota = lax.broadcasted_iota(jnp.int32, (N_EXPERTS, tm), 0)
    rs = rs_ref[...]
    for r in range(TOP_K):
        base = jnp.sum(jnp.where(eiota == idx_ref[r:r + 1, :], rs, 0.0), axis=0, keepdims=True)
        dest_ref[r:r + 1, :] = base.astype(jnp.int32) + rank_ref[r:r + 1, :]


def _dest(idx, rank, row_start_f, tm):
    n = idx.shape[1]
    small = lambda: pl.BlockSpec((TOP_K, tm), lambda i: (0, i))
    return pl.pallas_call(
        _dest_body,
        grid=(n // tm,),
        in_specs=[small(), small(), pl.BlockSpec((N_EXPERTS, 1), lambda i: (0, 0))],
        out_specs=small(),
        out_shape=jax.ShapeDtypeStruct((TOP_K, n), jnp.int32),
        compiler_params=_params(("arbitrary",)),
        name="dest",
    )(idx, rank, row_start_f)


def _slot_index_copies(src_hbm, col0, width, dst_smem, s, stride, sem):
    return [pltpu.make_async_copy(src_hbm.at[pl.ds(j, 1), pl.ds(col0, width)],
                                  dst_smem.at[pl.ds(s, 1), pl.ds(j * stride, width)], sem)
            for j in range(TOP_K)]


def _dispatch_body(meta_ref, lc_ref, dp_hbm, ds_hbm, xp_ref, xsm_ref, xs_hbm, idx_smem, zbuf, isem, ssem, zsem):
    i = pl.program_id(0)
    n = pl.num_programs(0)
    slot = i % 2
    tm = xp_ref.shape[0]
    n_s = xsm_ref.shape[0]
    total_chunks = xs_hbm.shape[0] // GATHER_CHUNK

    def idx_copies(tile, s, fn):
        @pl.when(tile < n - 1)
        def _():
            for c in _slot_index_copies(dp_hbm, tile * tm, tm, idx_smem, s, tm, isem.at[s]):
                fn(c)

        @pl.when(tile == n - 1)
        def _():
            for c in _slot_index_copies(ds_hbm, 0, n_s, idx_smem, s, tm, isem.at[s]):
                fn(c)

    @pl.when(i == 0)
    def _():
        idx_copies(0, 0, lambda c: c.start())

    idx_copies(i, slot, lambda c: c.wait())

    @pl.when(i + 1 < n)
    def _():
        idx_copies(i + 1, 1 - slot, lambda c: c.start())

    def scatter(src_ref):
        cnt = src_ref.shape[0]
        for j in range(TOP_K):
            def body(t, carry):
                row = idx_smem[slot, j * tm + t]
                pltpu.make_async_copy(src_ref.at[t], xs_hbm.at[row], ssem).start()
                return carry
            lax.fori_loop(0, cnt, body, 0, unroll=8)
        for j in range(TOP_K):
            pltpu.make_async_copy(src_ref, xs_hbm.at[pl.ds(0, cnt)], ssem).wait()

    @pl.when(i == 0)
    def _():
        zbuf[...] = jnp.zeros(zbuf.shape, F32)

        def zero_chunk(c):
            return pltpu.make_async_copy(zbuf, xs_hbm.at[pl.ds(c * GATHER_CHUNK, GATHER_CHUNK)], zsem)

        def each_expert(fn):
            def body(e, carry):
                @pl.when(lc_ref[e] >= 0)
                def _():
                    fn(lc_ref[e])
                return carry
            lax.fori_loop(0, N_EXPERTS, body, 0)

        def each_tail(fn):
            def body(c, carry):
                fn(c)
                return carry
            lax.fori_loop(meta_ref[0], total_chunks, body, 0)

        each_expert(lambda c: zero_chunk(c).start())
        each_tail(lambda c: zero_chunk(c).start())
        each_expert(lambda c: zero_chunk(c).wait())
        each_tail(lambda c: zero_chunk(c).wait())

    @pl.when(i < n - 1)
    def _():
        scatter(xp_ref)

    @pl.when(i == n - 1)
    def _():
        scatter(xsm_ref)


def _dispatch(meta, last_chunk, dest_p, dest_s, x1p, x1s, n_rows, tm):
    n_p, sub, lanes = x1p.shape
    n_s = x1s.shape[0]
    n_pt = n_p // tm
    grid_spec = pltpu.PrefetchScalarGridSpec(
        num_scalar_prefetch=2,
        grid=(n_pt + 1,),
        in_specs=[
            pl.BlockSpec(memory_space=pl.ANY),
            pl.BlockSpec(memory_space=pl.ANY),
            pl.BlockSpec((tm, sub, lanes), lambda i, *_: (jnp.minimum(i, n_pt - 1), 0, 0)),
            pl.BlockSpec((n_s, sub, lanes), lambda i, *_: (0, 0, 0)),
        ],
        out_specs=pl.BlockSpec(memory_space=pl.ANY),
        scratch_shapes=[
            pltpu.SMEM((2, TOP_K * tm), jnp.int32),
            pltpu.VMEM((GATHER_CHUNK, sub, lanes), F32),
            pltpu.SemaphoreType.DMA((2,)),
            pltpu.SemaphoreType.DMA(()),
            pltpu.SemaphoreType.DMA(()),
        ],
    )
    return pl.pallas_call(
        _dispatch_body,
        grid_spec=grid_spec,
        out_shape=jax.ShapeDtypeStruct((n_rows, sub, lanes), F32),
        compiler_params=_dma_params(("arbitrary",)),
        name="dispatch",
    )(meta, last_chunk, dest_p, dest_s, x1p, x1s)


def _experts_body(be_ref, bx_ref, bn_ref, bz_ref, xs_hbm, wg_ref, wu_ref, wd_ref, ys_hbm,
                  xbuf, obuf, gsem, osem, wgu_bf, wd_bf):
    b = pl.program_id(0)
    nb = pl.num_programs(0)
    slot = b % 2
    ff = wd_ref.shape[1]

    def in_copies(blk, s, nch):
        rows = pl.ds(bx_ref[blk] * GATHER_CHUNK, nch * GATHER_CHUNK)
        return [pltpu.make_async_copy(xs_hbm.at[rows, c, :],
                                      xbuf.at[s, pl.ds(0, nch * GATHER_CHUNK), pl.ds(c * LANES, LANES)],
                                      gsem.at[s]) for c in range(xs_hbm.shape[1])]

    def out_copies(blk, s, nch):
        rows = pl.ds(bx_ref[blk] * GATHER_CHUNK, nch * GATHER_CHUNK)
        return [pltpu.make_async_copy(obuf.at[s, pl.ds(0, nch * GATHER_CHUNK), pl.ds(c * LANES, LANES)],
                                      ys_hbm.at[rows, c, :], osem.at[s]) for c in range(ys_hbm.shape[1])]

    def for_block(count, copies, fn):
        for nch in range(1, BLOCK_CHUNKS + 1):
            @pl.when(count == nch)
            def _():
                for c in copies(nch):
                    fn(c)

    def n_in(blk):
        return jnp.where(bz_ref[blk] == 0, bn_ref[blk], 0)

    @pl.when(b == 0)
    def _():
        for_block(n_in(0), lambda k: in_copies(0, 0, k), lambda c: c.start())

    @pl.when(b + 1 < nb)
    def _():
        for_block(n_in(b + 1), lambda k: in_copies(b + 1, 1 - slot, k), lambda c: c.start())

    @pl.when(b >= 2)
    def _():
        for_block(bn_ref[b - 2], lambda k: out_copies(b - 2, slot, k), lambda c: c.wait())

    for_block(n_in(b), lambda k: in_copies(b, slot, k), lambda c: c.wait())

    e_now = be_ref[b]
    e_prev = be_ref[jnp.maximum(b - 1, 0)]

    @pl.when(jnp.logical_or(b == 0, e_now != e_prev))
    def _():
        wgu_bf[:, 0:ff] = wg_ref[0].astype(BF16)
        wgu_bf[:, ff:2 * ff] = wu_ref[0].astype(BF16)
        wd_bf[...] = wd_ref[0].astype(BF16)

    for nch in range(1, BLOCK_CHUNKS + 1):
        @pl.when(n_in(b) == nch)
        def _():
            rows = nch * GATHER_CHUNK
            gu = _dot(xbuf[slot, 0:rows, :].astype(BF16), wgu_bf[...])
            hid = _silu(gu[:, :ff]) * gu[:, ff:]
            obuf[slot, 0:rows, :] = _dot(hid.astype(BF16), wd_bf[...])

    @pl.when(bz_ref[b] != 0)
    def _():
        obuf[slot] = jnp.zeros(obuf.shape[1:], F32)

    for_block(bn_ref[b], lambda k: out_copies(b, slot, k), lambda c: c.start())

    @pl.when(b == nb - 1)
    def _():
        for_block(bn_ref[b], lambda k: out_copies(b, slot, k), lambda c: c.wait())

        @pl.when(nb > 1)
        def _():
            for_block(bn_ref[b - 1], lambda k: out_copies(b - 1, 1 - slot, k), lambda c: c.wait())


def _experts(block_e, block_x, block_n, block_z, xs, we_gate, we_up, we_down):
    nb = block_e.shape[0]
    n_rows = xs.shape[0]
    d = we_gate.shape[1]
    ff = we_gate.shape[2]
    rows = BLOCK_CHUNKS * GATHER_CHUNK
    grid_spec = pltpu.PrefetchScalarGridSpec(
        num_scalar_prefetch=4,
        grid=(nb,),
        in_specs=[
            pl.BlockSpec(memory_space=pl.ANY),
            pl.BlockSpec((1, d, ff), lambda b, be, *_: (be[b], 0, 0)),
            pl.BlockSpec((1, d, ff), lambda b, be, *_: (be[b], 0, 0)),
            pl.BlockSpec((1, ff, d), lambda b, be, *_: (be[b], 0, 0)),
        ],
        out_specs=pl.BlockSpec(memory_space=pl.ANY),
        scratch_shapes=[
            pltpu.VMEM((2, rows, d), F32),
            pltpu.VMEM((2, rows, d), F32),
            pltpu.SemaphoreType.DMA((2,)),
            pltpu.SemaphoreType.DMA((2,)),
            pltpu.VMEM((d, 2 * ff), BF16),
            pltpu.VMEM((ff, d), BF16),
        ],
    )
    return pl.pallas_call(
        _experts_body,
        grid_spec=grid_spec,
        out_shape=jax.ShapeDtypeStruct(xs.shape, F32),
        compiler_params=_dma_params(("arbitrary",)),
        name="experts",
    )(block_e, block_x, block_n, block_z, xs, we_gate, we_up, we_down)


def _combine_body(dt_hbm, wt_hbm, ys_hbm, part_ref, g_ref, b_ref, y_ref, idx_smem, w_smem, gbuf, isem, wsem, gsem):
    i = pl.program_id(0)
    n = pl.num_programs(0)
    slot = i % 2
    tm = part_ref.shape[0]

    def meta_copies(tile, s):
        return (_slot_index_copies(dt_hbm, tile * tm, tm, idx_smem, s, tm, isem.at[s]) +
                _slot_index_copies(wt_hbm, tile * tm, tm, w_smem, s, tm, wsem.at[s]))

    def issue_rows(s):
        for j in range(TOP_K):
            def body(t, carry):
                row = idx_smem[s, j * tm + t]
                pltpu.make_async_copy(ys_hbm.at[row], gbuf.at[s, j, t], gsem.at[s]).start()
                return carry
            lax.fori_loop(0, tm, body, 0, unroll=8)

    def wait_rows(s):
        for j in range(TOP_K):
            pltpu.make_async_copy(ys_hbm.at[pl.ds(0, tm)], gbuf.at[s, j], gsem.at[s]).wait()

    @pl.when(i == 0)
    def _():
        for c in meta_copies(0, 0):
            c.start()
        for c in meta_copies(0, 0):
            c.wait()
        issue_rows(0)

        @pl.when(n > 1)
        def _():
            for c in meta_copies(1, 1):
                c.start()

    @pl.when(i + 1 < n)
    def _():
        for c in meta_copies(i + 1, 1 - slot):
            c.wait()
        issue_rows(1 - slot)

    wait_rows(slot)

    def token(t, carry):
        f = part_ref[t]
        for j in range(TOP_K):
            f = f + w_smem[slot, j * tm + t] * gbuf[slot, j, t]
        y_ref[t] = f
        return carry
    lax.fori_loop(0, tm, token, 0, unroll=8)

    @pl.when(i + 2 < n)
    def _():
        for c in meta_copies(i + 2, slot):
            c.start()

    f = y_ref[...]
    inv_d = 1.0 / (f.shape[1] * f.shape[2])
    tile_sum = lambda a: jnp.sum(jnp.sum(a, axis=2, keepdims=True), axis=1, keepdims=True)
    fc = f - tile_sum(f) * inv_d
    var = tile_sum(fc * fc) * inv_d
    y_ref[...] = fc * lax.rsqrt(var + LN_EPS) * g_ref[...] + b_ref[...]


def _combine(dest, wts, ys, part, ln_g, ln_b, tm):
    n, sub, lanes = part.shape
    return pl.pallas_call(
        _combine_body,
        grid=(n // tm,),
        in_specs=[
            pl.BlockSpec(memory_space=pl.ANY),
            pl.BlockSpec(memory_space=pl.ANY),
            pl.BlockSpec(memory_space=pl.ANY),
            pl.BlockSpec((tm, sub, lanes), lambda i: (i, 0, 0)),
            pl.BlockSpec((sub, lanes), lambda i: (0, 0)),
            pl.BlockSpec((sub, lanes), lambda i: (0, 0)),
        ],
        out_specs=pl.BlockSpec((tm, sub, lanes), lambda i: (i, 0, 0)),
        out_shape=jax.ShapeDtypeStruct((n, sub, lanes), F32),
        scratch_shapes=[
            pltpu.SMEM((2, TOP_K * tm), jnp.int32),
            pltpu.SMEM((2, TOP_K * tm), F32),
            pltpu.VMEM((2, TOP_K, tm, sub, lanes), F32),
            pltpu.SemaphoreType.DMA((2,)),
            pltpu.SemaphoreType.DMA((2,)),
            pltpu.SemaphoreType.DMA((2,)),
        ],
        compiler_params=_dma_params(("arbitrary",)),
        name="combine",
    )(dest, wts, ys, part, ln_g.reshape(sub, lanes), ln_b.reshape(sub, lanes))


def kernel(x_prompt, x_sample, cache_k, cache_v, state_conv, ln_in_g, ln_in_b, w_in, b_in, conv_w, conv_b,
           conv_ln_g, conv_ln_b, w_conv_out, w_attn_out, w_out, ln1_g, ln1_b, w_router, router_bias,
           we_gate, we_up, we_down, ws_gate, ws_up, ws_down, ln2_g, ln2_b):
    bp, tp, d = x_prompt.shape
    bs, ts, _ = x_sample.shape
    past = cache_k.shape[2]
    assert w_in.shape[0] == DEPTH and d == N_HEADS * HEAD_DIM and conv_w.shape[2] == d
    assert tp % 1024 == 0 and ts <= KEY_BLOCK and ts % 16 == 0 and past % KEY_BLOCK == 0
    n_p, n_s = bp * tp, bs * ts
    assert n_s % LANES == 0

    row = lambda a: a.reshape(1, -1)
    lng, lnb = row(ln_in_g), row(ln_in_b)
    w_in_b = w_in[0].astype(BF16)
    w_co_b = w_conv_out[0].astype(BF16)
    conv_args = (lng, lnb, w_in_b, b_in, conv_w[0], conv_b, conv_ln_g, conv_ln_b, w_co_b)

    qp, kp, vp, gcp, sap, csp = _proj_conv(x_prompt, jnp.zeros((bp, CONV_WIDTH - 1, d), F32), *conv_args, tm=256)
    op = _attn_prompt(qp, kp, vp, tq=512)

    qs, ks, vs, gcs, sas, css = _proj_conv(x_sample, state_conv[0], *conv_args, tm=ts)
    pad_new = lambda a: jnp.pad(a, ((0, 0), (0, KEY_BLOCK - ts), (0, 0)))
    os_ = _attn_sample(qs, pad_new(ks), pad_new(vs), cache_k[0].reshape(bs, past, d), cache_v[0].reshape(bs, past, d))

    wr = w_router[0].T
    wr_hi = wr.astype(BF16)
    wr_lo = (wr - wr_hi.astype(F32)).astype(BF16)
    post_args = (lng, lnb, w_attn_out[0].astype(BF16), w_out[0].astype(BF16), ln1_g, ln1_b, wr_hi, wr_lo,
                 router_bias.reshape(N_EXPERTS, 1),
                 jnp.concatenate([ws_gate[0], ws_up[0]], axis=1).astype(BF16), ws_down[0].astype(BF16))
    flat = lambda a, n: a.reshape(n, d)
    zero_cnt = jnp.zeros((N_EXPERTS, LANES), F32)
    x1p, partp, idxp, wtp, rankp, cnt_p = _post_attn(
        flat(op, n_p), flat(gcp, n_p), flat(sap, n_p), flat(x_prompt, n_p), zero_cnt, *post_args, tm=256)
    x1s, parts, idxs, wts, ranks, cnt = _post_attn(
        flat(os_, n_s), flat(gcs, n_s), flat(sas, n_s), flat(x_sample, n_s), cnt_p, *post_args, tm=n_s)

    i32 = jnp.int32
    n_tok = n_p + n_s
    total_chunks = (n_tok * TOP_K + N_EXPERTS * (GATHER_CHUNK - 1) + GATHER_CHUNK - 1) // GATHER_CHUNK
    n_blocks = total_chunks // BLOCK_CHUNKS + N_EXPERTS
    counts = cnt[:, 0].astype(i32)
    chunks = (counts + GATHER_CHUNK - 1) // GATHER_CHUNK
    chunk_end = jnp.cumsum(chunks)
    chunk_start = chunk_end - chunks
    row_start = chunk_start * GATHER_CHUNK
    used_chunks = chunk_end[-1]
    blocks = (chunks + BLOCK_CHUNKS - 1) // BLOCK_CHUNKS
    blk_end = jnp.cumsum(blocks)
    blk_start = blk_end - blocks
    bid = jnp.arange(n_blocks, dtype=i32)
    block_e = jnp.minimum(jnp.sum(bid[:, None] >= blk_end[None, :], axis=1), N_EXPERTS - 1).astype(i32)
    k_in_e = bid - blk_start[block_e]
    is_tail = bid >= blk_end[-1]
    tail_x = used_chunks + BLOCK_CHUNKS * (bid - blk_end[-1])
    block_x = jnp.where(is_tail, tail_x, chunk_start[block_e] + BLOCK_CHUNKS * k_in_e)
    block_n = jnp.where(is_tail, total_chunks - tail_x, chunks[block_e] - BLOCK_CHUNKS * k_in_e)
    block_n = jnp.clip(block_n, 0, BLOCK_CHUNKS).astype(i32)
    block_x = jnp.where(block_n > 0, block_x, 0).astype(i32)
    block_z = is_tail.astype(i32)
    meta = jnp.stack([used_chunks, used_chunks]).astype(i32)
    last_chunk = jnp.where(chunks > 0, chunk_end - 1, -1).astype(i32)

    rs_f = row_start.astype(F32).reshape(N_EXPERTS, 1)
    dest_p = _dest(idxp, rankp, rs_f, tm=1024)
    dest_s = _dest(idxs, ranks, rs_f, tm=n_s)

    xs = _dispatch(meta, last_chunk, dest_p, dest_s, x1p, x1s, total_chunks * GATHER_CHUNK, tm=256)
    ys = _experts(block_e, block_x, block_n, block_z, xs, we_gate[0], we_up[0], we_down[0])

    yp = _combine(dest_p, wtp, ys, partp, ln2_g, ln2_b, tm=128)
    ysm = _combine(dest_s, wts, ys, parts, ln2_g, ln2_b, tm=n_s)

    heads = lambda a, b, t: a.reshape(1, b, t, N_HEADS, HEAD_DIM)
    return (yp.reshape(bp, tp, d), ysm.reshape(bs, ts, d),
            heads(kp, bp, tp), heads(vp, bp, tp), csp[None],
            heads(ks, bs, ts), heads(vs, bs, ts), css[None])
```

```python
import functools

import jax
import jax.numpy as jnp
from jax import lax
from jax.experimental import pallas as pl
from jax.experimental.pallas import tpu as pltpu

F32 = jnp.float32
BF16 = jnp.bfloat16

N_HEADS = 16
HEAD_DIM = 64
CONV_WIDTH = 31
N_EXPERTS = 256
TOP_K = 8
N_GROUPS = 8
TOPK_GROUPS = 4
GROUP_SIZE = N_EXPERTS // N_GROUPS
ROUTED_SCALE = 2.5
LN_EPS = 1e-5
DEPTH = 1
DEEPNORM_ALPHA = (2 * DEPTH) ** 0.25
ATTN_SCALE = HEAD_DIM ** -0.5

LANES = 128
SUBLANES = 8
VMEM_LIMIT_BYTES = 56 * 1024 * 1024

KEY_BLOCK = 128
HIST_PAD = 32
GATHER_CHUNK = 128
BLOCK_CHUNKS = 4
DMA_THREADS = 2
LOG_DEAD = -110.0


def _layer_norm(x, g, b):
    mu = jnp.mean(x, axis=-1, keepdims=True)
    xc = x - mu
    var = jnp.mean(xc * xc, axis=-1, keepdims=True)
    return xc * lax.rsqrt(var + LN_EPS) * g + b


def _sigmoid(x):
    return 1.0 / (1.0 + jnp.exp(-x))


def _silu(x):
    return x * _sigmoid(x)


def _dot(a, b):
    return jnp.dot(a, b, preferred_element_type=F32)


def _dot_nt(a, b):
    return lax.dot_general(a, b, (((1,), (1,)), ((), ())), preferred_element_type=F32)


def _params(sem):
    return pltpu.CompilerParams(dimension_semantics=sem, vmem_limit_bytes=VMEM_LIMIT_BYTES)


def _dma_params(sem):
    return pltpu.CompilerParams(dimension_semantics=sem, vmem_limit_bytes=VMEM_LIMIT_BYTES,
                                disable_bounds_checks=True)


def _proj_conv_body(x_ref, hist_ref, lng_ref, lnb_ref, win_ref, bin_ref, cw_ref, cb_ref, clg_ref,
                    clb_ref, wco_ref, q_ref, k_ref, v_ref, gc_ref, sa_ref, cs_ref, ubuf, cbuf):
    tm = x_ref.shape[1]
    d = x_ref.shape[2]
    hist = CONV_WIDTH - 1
    lead = HIST_PAD - hist

    @pl.when(pl.program_id(1) == 0)
    def _():
        ubuf[0:lead, :] = jnp.zeros((lead, d), F32)
        ubuf[lead:HIST_PAD, :] = hist_ref[0]

    hb = _layer_norm(x_ref[0], lng_ref[...], lnb_ref[...]).astype(BF16)

    def proj(i):
        return _dot(hb, win_ref[:, i * d:(i + 1) * d]) + bin_ref[:, i * d:(i + 1) * d]

    ubuf[HIST_PAD:HIST_PAD + tm, :] = proj(0) * _sigmoid(proj(1))
    q_ref[0] = (proj(2) * ATTN_SCALE).astype(BF16)
    k_ref[0] = proj(3)
    v_ref[0] = proj(4)
    sa_ref[0] = _sigmoid(proj(6)).astype(BF16)

    rows = min(tm, 32)
    for r0 in range(0, tm, rows):
        acc = jnp.broadcast_to(cb_ref[...], (rows, d))
        for kk in range(CONV_WIDTH):
            acc = acc + cw_ref[kk:kk + 1, :] * ubuf[lead + kk + r0:lead + kk + r0 + rows, :]
        cbuf[r0:r0 + rows, :] = _silu(_layer_norm(acc, clg_ref[...], clb_ref[...])).astype(BF16)
    gc_ref[0] = (_sigmoid(proj(5)) * _dot(cbuf[...], wco_ref[...])).astype(BF16)

    cs_ref[0] = ubuf[tm + lead:tm + HIST_PAD, :]
    ubuf[0:HIST_PAD, :] = ubuf[tm:tm + HIST_PAD, :]


def _proj_conv(x, hist, ln_g, ln_b, w_in, b_in, conv_w, conv_b, cln_g, cln_b, w_co, tm):
    bsz, t, d = x.shape
    pw = w_in.shape[1]
    const = lambda shape: pl.BlockSpec(shape, lambda b, i: (0,) * len(shape))
    tile = lambda: pl.BlockSpec((1, tm, d), lambda b, i: (b, i, 0))
    return pl.pallas_call(
        _proj_conv_body,
        grid=(bsz, t // tm),
        in_specs=[
            tile(),
            pl.BlockSpec((1, CONV_WIDTH - 1, d), lambda b, i: (b, 0, 0)),
            const((1, d)), const((1, d)), const((d, pw)), const((1, pw)),
            const((CONV_WIDTH, d)), const((1, d)), const((1, d)), const((1, d)), const((d, d)),
        ],
        out_specs=[tile(), tile(), tile(), tile(), tile(),
                   pl.BlockSpec((1, CONV_WIDTH - 1, d), lambda b, i: (b, 0, 0))],
        out_shape=[
            jax.ShapeDtypeStruct((bsz, t, d), BF16),
            jax.ShapeDtypeStruct((bsz, t, d), F32),
            jax.ShapeDtypeStruct((bsz, t, d), F32),
            jax.ShapeDtypeStruct((bsz, t, d), BF16),
            jax.ShapeDtypeStruct((bsz, t, d), BF16),
            jax.ShapeDtypeStruct((bsz, CONV_WIDTH - 1, d), F32),
        ],
        scratch_shapes=[pltpu.VMEM((HIST_PAD + tm, d), F32), pltpu.VMEM((tm, d), BF16)],
        compiler_params=_params(("arbitrary", "arbitrary")),
        name="proj_conv",
    )(x, hist, ln_g, ln_b, w_in, b_in, conv_w, conv_b, cln_g, cln_b, w_co)


def _suffix_pair():
    row = lax.broadcasted_iota(jnp.int32, (2 * KEY_BLOCK, KEY_BLOCK), 0)
    col = lax.broadcasted_iota(jnp.int32, (2 * KEY_BLOCK, KEY_BLOCK), 1)
    row = jnp.where(row >= KEY_BLOCK, row - KEY_BLOCK, row)
    return jnp.where(row >= col, 1.0, 0.0).astype(BF16)


def _sb_blocks(qss, kbs, vbs, acc, ll, vis, suffix2):
    r = qss[0].shape[0]
    z = jnp.concatenate([_dot_nt(q, kb) for q, kb in zip(qss, kbs)], axis=0)
    lk = -(jnp.maximum(z, 0.0) + jnp.log1p(jnp.exp(-jnp.abs(z))))
    if vis is not None:
        lk = jnp.where(vis, lk, 0.0)
    hi = lk.astype(BF16)
    lo = (lk - hi.astype(F32)).astype(BF16)
    inb = _dot(jnp.concatenate([hi, lo], axis=1), suffix2)
    arg = z + inb + ll
    if vis is not None:
        arg = jnp.where(vis, arg, -jnp.inf)
    a = jnp.exp(arg).astype(BF16)
    av = jnp.concatenate([_dot(a[i * r:(i + 1) * r], vb) for i, vb in enumerate(vbs)], axis=0)
    return acc + av, ll + inb[:, 0:1]


def _sb_block(qs, kb, vb, acc, ll, vis, suffix2):
    return _sb_blocks([qs], [kb], [vb], acc, ll, vis, suffix2)


def _sb_older_blocks(qs, k_ref, v_ref, first, acc, ll, suffix2):
    def live(ll_):
        return (jnp.max(ll_, axis=0, keepdims=True)[0, 0] > LOG_DEAD).astype(jnp.int32)

    def cond(c):
        return jnp.logical_and(c[0] >= 0, c[3] > 0)

    def body(c):
        j, acc_, ll_, _ = c
        start = pl.multiple_of(j * KEY_BLOCK, KEY_BLOCK)
        kb = k_ref[0, pl.ds(start, KEY_BLOCK), :].astype(BF16)
        vb = v_ref[0, pl.ds(start, KEY_BLOCK), :].astype(BF16)
        acc_, ll_ = _sb_block(qs, kb, vb, acc_, ll_, None, suffix2)
        return j - 1, acc_, ll_, live(ll_)

    _, acc, _, _ = lax.while_loop(cond, body, (first, acc, ll, live(ll)))
    return acc


def _stack_heads(q):
    lane = lax.broadcasted_iota(jnp.int32, q.shape, 1)
    zero = jnp.zeros_like(q)
    return jnp.concatenate([jnp.where(lane < HEAD_DIM, q, zero), jnp.where(lane >= HEAD_DIM, q, zero)], axis=0)


def _row_in_head(rows, cols):
    r = lax.broadcasted_iota(jnp.int32, (rows, cols), 0)
    return jnp.where(r >= rows // 2, r - rows // 2, r)


def _unstack_heads(acc):
    r = acc.shape[0] // 2
    lane = lax.broadcasted_iota(jnp.int32, (r, LANES), 1)
    return jnp.where(lane < HEAD_DIM, acc[:r], acc[r:])


def _attn_prompt_body(q_ref, k_ref, v_ref, o_ref):
    tq = q_ref.shape[1]
    nq = tq // KEY_BLOCK
    r = 2 * KEY_BLOCK
    g0 = pl.program_id(2) * nq
    col = lax.broadcasted_iota(jnp.int32, (nq * r, KEY_BLOCK), 1)
    row = lax.broadcasted_iota(jnp.int32, (nq * r, KEY_BLOCK), 0) & (KEY_BLOCK - 1)
    vis = col < row
    suffix2 = _suffix_pair()

    def kv_blocks(js):
        starts = [pl.multiple_of(j * KEY_BLOCK, KEY_BLOCK) for j in js]
        return ([k_ref[0, pl.ds(s, KEY_BLOCK), :].astype(BF16) for s in starts],
                [v_ref[0, pl.ds(s, KEY_BLOCK), :].astype(BF16) for s in starts])

    def mask_finished(ll, off):
        return jnp.concatenate([jnp.where(g0 + qi - off >= 0, ll[qi * r:(qi + 1) * r], -jnp.inf)
                                for qi in range(nq)], axis=0)

    def any_live(ll, off):
        m = jnp.max(mask_finished(ll, off), axis=0, keepdims=True)
        return (m[0, 0] > LOG_DEAD).astype(jnp.int32)

    qss = [_stack_heads(q_ref[0, qi * KEY_BLOCK:(qi + 1) * KEY_BLOCK, :]) for qi in range(nq)]
    kbs, vbs = kv_blocks([g0 + qi for qi in range(nq)])
    acc, ll = _sb_blocks(qss, kbs, vbs, jnp.zeros((nq * r, LANES), F32), jnp.zeros((nq * r, 1), F32), vis,
                         suffix2)

    def cond(c):
        return c[3] > 0

    def body(c):
        off, acc_, ll_, _ = c
        kbs_, vbs_ = kv_blocks([jnp.maximum(g0 + qi - off, 0) for qi in range(nq)])
        acc_, ll_ = _sb_blocks(qss, kbs_, vbs_, acc_, mask_finished(ll_, off), None, suffix2)
        return off + 1, acc_, ll_, any_live(ll_, off + 1)

    one = jnp.int32(1)
    _, acc, _, _ = lax.while_loop(cond, body, (one, acc, ll, any_live(ll, one)))
    for qi in range(nq):
        o_ref[0, qi * KEY_BLOCK:(qi + 1) * KEY_BLOCK, :] = _unstack_heads(acc[qi * r:(qi + 1) * r]).astype(BF16)


def _attn_prompt(q, k, v, tq):
    bsz, t, d = q.shape
    pairs = d // LANES
    return pl.pallas_call(
        _attn_prompt_body,
        grid=(bsz, pairs, t // tq),
        in_specs=[
            pl.BlockSpec((1, tq, LANES), lambda b, p, i: (b, i, p)),
            pl.BlockSpec((1, t, LANES), lambda b, p, i: (b, 0, p)),
            pl.BlockSpec((1, t, LANES), lambda b, p, i: (b, 0, p)),
        ],
        out_specs=pl.BlockSpec((1, tq, LANES), lambda b, p, i: (b, i, p)),
        out_shape=jax.ShapeDtypeStruct((bsz, t, d), BF16),
        compiler_params=_params(("arbitrary", "arbitrary", "arbitrary")),
        name="attn_prompt",
    )(q, k, v)


def _attn_sample_body(q_ref, kn_ref, vn_ref, ck_ref, cv_ref, o_ref):
    tq = q_ref.shape[1]
    past_blocks = ck_ref.shape[1] // KEY_BLOCK
    qs = _stack_heads(q_ref[0])
    row = _row_in_head(2 * tq, KEY_BLOCK)
    col = lax.broadcasted_iota(jnp.int32, (2 * tq, KEY_BLOCK), 1)
    acc = jnp.zeros((2 * tq, LANES), F32)
    ll = jnp.zeros((2 * tq, 1), F32)
    suffix2 = _suffix_pair()
    acc, ll = _sb_block(qs, kn_ref[0].astype(BF16), vn_ref[0].astype(BF16), acc, ll, col < row, suffix2)
    acc = _sb_older_blocks(qs, ck_ref, cv_ref, past_blocks - 1, acc, ll, suffix2)
    o_ref[0] = _unstack_heads(acc).astype(BF16)


def _attn_sample(q, k_new, v_new, cache_k, cache_v):
    bsz, t, d = q.shape
    past = cache_k.shape[1]
    pairs = d // LANES
    return pl.pallas_call(
        _attn_sample_body,
        grid=(bsz, pairs),
        in_specs=[
            pl.BlockSpec((1, t, LANES), lambda b, p: (b, 0, p)),
            pl.BlockSpec((1, KEY_BLOCK, LANES), lambda b, p: (b, 0, p)),
            pl.BlockSpec((1, KEY_BLOCK, LANES), lambda b, p: (b, 0, p)),
            pl.BlockSpec((1, past, LANES), lambda b, p: (b, 0, p)),
            pl.BlockSpec((1, past, LANES), lambda b, p: (b, 0, p)),
        ],
        out_specs=pl.BlockSpec((1, t, LANES), lambda b, p: (b, 0, p)),
        out_shape=jax.ShapeDtypeStruct((bsz, t, d), BF16),
        compiler_params=_params(("arbitrary", "arbitrary")),
        name="attn_sample",
    )(q, k_new, v_new, cache_k, cache_v)


def _first_argmax(cur, iota_f, n):
    m = jnp.max(cur, axis=0, keepdims=True)
    i = jnp.min(jnp.where(cur == m, iota_f, float(n)), axis=0, keepdims=True)
    return m, i


def _store_row_tiles(ref, val):
    for s in range(val.shape[1] // LANES):
        ref[:, s, :] = val[:, s * LANES:(s + 1) * LANES]


def _post_attn_body(o_ref, gc_ref, sa_ref, x_ref, cnt_ref, lng_ref, lnb_ref, wao_ref, wo_ref, l1g_ref,
                    l1b_ref, wrh_ref, wrl_ref, rb_ref, wsgu_ref, wsd_ref,
                    x1_ref, part_ref, idx_ref, wts_ref, rank_ref, cnt_out_ref, run):
    tm = x_ref.shape[0]
    ff = wsd_ref.shape[0]

    @pl.when(pl.program_id(0) == 0)
    def _():
        run[...] = cnt_ref[...]

    h = _layer_norm(x_ref[...], lng_ref[...], lnb_ref[...])
    att = _dot(o_ref[...], wao_ref[...])
    merged = gc_ref[...].astype(F32) + sa_ref[...].astype(F32) * att
    mixed = _dot(merged.astype(BF16), wo_ref[...])
    x1 = _layer_norm(DEEPNORM_ALPHA * h + mixed, l1g_ref[...], l1b_ref[...])
    _store_row_tiles(x1_ref, x1)

    x1h = x1.astype(BF16)
    gu = _dot(x1h, wsgu_ref[...])
    shared = _dot((_silu(gu[:, :ff]) * gu[:, ff:]).astype(BF16), wsd_ref[...])
    _store_row_tiles(part_ref, DEEPNORM_ALPHA * x1 + shared)

    x1l = (x1 - x1h.astype(F32)).astype(BF16)
    logits = _dot_nt(wrh_ref[...], x1h) + _dot_nt(wrl_ref[...], x1h) + _dot_nt(wrh_ref[...], x1l)
    scores = _sigmoid(logits)
    choice = scores + rb_ref[...]
    neg = -jnp.inf

    giota = lax.broadcasted_iota(jnp.int32, (GROUP_SIZE, tm), 0).astype(F32)
    gs = []
    for g in range(N_GROUPS):
        blk = choice[g * GROUP_SIZE:(g + 1) * GROUP_SIZE, :]
        m1, i1 = _first_argmax(blk, giota, GROUP_SIZE)
        m2 = jnp.max(jnp.where(giota == i1, neg, blk), axis=0, keepdims=True)
        gs.append(m1 + m2)
    gscore = jnp.concatenate(gs, axis=0)

    g8 = lax.broadcasted_iota(jnp.int32, (N_GROUPS, tm), 0).astype(F32)
    gsel = jnp.zeros((N_GROUPS, tm), F32)
    cur = gscore
    for _ in range(TOPK_GROUPS):
        _, i = _first_argmax(cur, g8, N_GROUPS)
        hit = g8 == i
        gsel = jnp.where(hit, 1.0, gsel)
        cur = jnp.where(hit, neg, cur)
    emask = jnp.concatenate(
        [jnp.broadcast_to(gsel[g:g + 1, :], (GROUP_SIZE, tm)) for g in range(N_GROUPS)], axis=0)

    eiota = lax.broadcasted_iota(jnp.int32, (N_EXPERTS, tm), 0).astype(F32)
    cur = jnp.where(emask > 0.0, choice, neg)
    hits, sel_w = [], []
    for r in range(TOP_K):
        _, i = _first_argmax(cur, eiota, N_EXPERTS)
        hit = eiota == i
        hits.append(hit)
        sel_w.append(jnp.sum(jnp.where(hit, scores, 0.0), axis=0, keepdims=True))
        cur = jnp.where(hit, neg, cur)
        idx_ref[r:r + 1, :] = i.astype(jnp.int32)
    wsum = sel_w[0]
    for r in range(1, TOP_K):
        wsum = wsum + sel_w[r]
    for r in range(TOP_K):
        wts_ref[r:r + 1, :] = sel_w[r] / wsum * ROUTED_SCALE

    picked = jnp.zeros((N_EXPERTS, tm), F32)
    for r in range(TOP_K):
        picked = jnp.where(hits[r], 1.0, picked)
    picked_b = picked.astype(BF16)
    trow = lax.broadcasted_iota(jnp.int32, (tm, tm), 0)
    tcol = lax.broadcasted_iota(jnp.int32, (tm, tm), 1)
    earlier = jnp.where(trow < tcol, 1.0, 0.0).astype(BF16)
    before = _dot(picked_b, earlier) + run[:, 0:1]
    for r in range(TOP_K):
        rank_ref[r:r + 1, :] = jnp.sum(jnp.where(hits[r], before, 0.0), axis=0, keepdims=True).astype(jnp.int32)
    run[...] = run[...] + _dot(picked_b, jnp.ones((tm, LANES), BF16))
    cnt_out_ref[...] = run[...]


def _post_attn(o, gc, sa, x, cnt_in, ln_g, ln_b, w_ao, w_o, l1g, l1b, wr_hi, wr_lo, r_bias, ws_gu, ws_d, tm):
    n, d = x.shape
    ff = ws_d.shape[0]
    const = lambda shape: pl.BlockSpec(shape, lambda i: (0,) * len(shape))
    tile = lambda: pl.BlockSpec((tm, d), lambda i: (i, 0))
    small = lambda: pl.BlockSpec((TOP_K, tm), lambda i: (0, i))
    row_tiles = lambda: pl.BlockSpec((tm, d // LANES, LANES), lambda i: (i, 0, 0))
    return pl.pallas_call(
        _post_attn_body,
        grid=(n // tm,),
        in_specs=[tile(), tile(), tile(), tile(), const((N_EXPERTS, LANES)),
                  const((1, d)), const((1, d)), const((d, d)), const((d, d)), const((1, d)), const((1, d)),
                  const((N_EXPERTS, d)), const((N_EXPERTS, d)), const((N_EXPERTS, 1)),
                  const((d, 2 * ff)), const((ff, d))],
        out_specs=[row_tiles(), row_tiles(), small(), small(), small(), const((N_EXPERTS, LANES))],
        out_shape=[
            jax.ShapeDtypeStruct((n, d // LANES, LANES), F32),
            jax.ShapeDtypeStruct((n, d // LANES, LANES), F32),
            jax.ShapeDtypeStruct((TOP_K, n), jnp.int32),
            jax.ShapeDtypeStruct((TOP_K, n), F32),
            jax.ShapeDtypeStruct((TOP_K, n), jnp.int32),
            jax.ShapeDtypeStruct((N_EXPERTS, LANES), F32),
        ],
        scratch_shapes=[pltpu.VMEM((N_EXPERTS, LANES), F32)],
        compiler_params=_params(("arbitrary",)),
        name="post_attn",
    )(o, gc, sa, x, cnt_in, ln_g, ln_b, w_ao, w_o, l1g, l1b, wr_hi, wr_lo, r_bias, ws_gu, ws_d)


def _dest_body(idx_ref, rank_ref, rs_ref, dest_ref):
    tm = idx_ref.shape[1]
    eiota = lax.broadcasted_iota(jnp.int32, (N_EXPERTS, tm), 0)
    rs = rs_ref[...]
    for r in range(TOP_K):
        base = jnp.sum(jnp.where(eiota == idx_ref[r:r + 1, :], rs, 0.0), axis=0, keepdims=True)
        dest_ref[r:r + 1, :] = base.astype(jnp.int32) + rank_ref[r:r + 1, :]


def _dest(idx, rank, row_start_f, tm):
    n = idx.shape[1]
    small = lambda: pl.BlockSpec((TOP_K, tm), lambda i: (0, i))
    return pl.pallas_call(
        _dest_body,
        grid=(n // tm,),
        in_specs=[small(), small(), pl.BlockSpec((N_EXPERTS, 1), lambda i: (0, 0))],
        out_specs=small(),
        out_shape=jax.ShapeDtypeStruct((TOP_K, n), jnp.int32),
        compiler_params=_params(("arbitrary",)),
        name="dest",
    )(idx, rank, row_start_f)


def _slot_index_copies(src_hbm, col0, width, dst_smem, s, stride, sem):
    return [pltpu.make_async_copy(src_hbm.at[pl.ds(j, 1), pl.ds(col0, width)],
                                  dst_smem.at[pl.ds(s, 1), pl.ds(j * stride, width)], sem)
            for j in range(TOP_K)]


def _dispatch_body(meta_ref, lc_ref, dp_hbm, ds_hbm, xp_ref, xsm_ref, xs_hbm, idx_smem, zbuf, isem, ssem, zsem):
    i = pl.program_id(0)
    n = pl.num_programs(0)
    slot = i % 2
    tm = xp_ref.shape[0]
    n_s = xsm_ref.shape[0]
    total_chunks = xs_hbm.shape[0] // GATHER_CHUNK

    def idx_copies(tile, s, fn):
        @pl.when(tile < n - 1)
        def _():
            for c in _slot_index_copies(dp_hbm, tile * tm, tm, idx_smem, s, tm, isem.at[s]):
                fn(c)

        @pl.when(tile == n - 1)
        def _():
            for c in _slot_index_copies(ds_hbm, 0, n_s, idx_smem, s, tm, isem.at[s]):
                fn(c)

    @pl.when(i == 0)
    def _():
        idx_copies(0, 0, lambda c: c.start())

    idx_copies(i, slot, lambda c: c.wait())

    @pl.when(i + 1 < n)
    def _():
        idx_copies(i + 1, 1 - slot, lambda c: c.start())

    def scatter(src_ref):
        cnt = src_ref.shape[0]
        for j in range(TOP_K):
            def body(p, carry):
                for u in range(DMA_THREADS):
                    t = DMA_THREADS * p + u
                    row = idx_smem[slot, j * tm + t]
                    pltpu.make_async_copy(src_ref.at[t], xs_hbm.at[row], ssem).start(priority=u)
                return carry
            lax.fori_loop(0, cnt // DMA_THREADS, body, 0, unroll=4)
        for j in range(TOP_K):
            pltpu.make_async_copy(src_ref, xs_hbm.at[pl.ds(0, cnt)], ssem).wait()

    @pl.when(i == 0)
    def _():
        zbuf[...] = jnp.zeros(zbuf.shape, F32)

        def zero_chunk(c):
            return pltpu.make_async_copy(zbuf, xs_hbm.at[pl.ds(c * GATHER_CHUNK, GATHER_CHUNK)], zsem)

        def each_expert(fn):
            def body(e, carry):
                @pl.when(lc_ref[e] >= 0)
                def _():
                    fn(lc_ref[e])
                return carry
            lax.fori_loop(0, N_EXPERTS, body, 0)

        def each_tail(fn):
            def body(c, carry):
                fn(c)
                return carry
            lax.fori_loop(meta_ref[0], total_chunks, body, 0)

        each_expert(lambda c: zero_chunk(c).start())
        each_tail(lambda c: zero_chunk(c).start())
        each_expert(lambda c: zero_chunk(c).wait())
        each_tail(lambda c: zero_chunk(c).wait())

    @pl.when(i < n - 1)
    def _():
        scatter(xp_ref)

    @pl.when(i == n - 1)
    def _():
        scatter(xsm_ref)


def _dispatch(meta, last_chunk, dest_p, dest_s, x1p, x1s, n_rows, tm):
    n_p, sub, lanes = x1p.shape
    n_s = x1s.shape[0]
    n_pt = n_p // tm
    grid_spec = pltpu.PrefetchScalarGridSpec(
        num_scalar_prefetch=2,
        grid=(n_pt + 1,),
        in_specs=[
            pl.BlockSpec(memory_space=pl.ANY),
            pl.BlockSpec(memory_space=pl.ANY),
            pl.BlockSpec((tm, sub, lanes), lambda i, *_: (jnp.minimum(i, n_pt - 1), 0, 0)),
            pl.BlockSpec((n_s, sub, lanes), lambda i, *_: (0, 0, 0)),
        ],
        out_specs=pl.BlockSpec(memory_space=pl.ANY),
        scratch_shapes=[
            pltpu.SMEM((2, TOP_K * tm), jnp.int32),
            pltpu.VMEM((GATHER_CHUNK, sub, lanes), F32),
            pltpu.SemaphoreType.DMA((2,)),
            pltpu.SemaphoreType.DMA(()),
            pltpu.SemaphoreType.DMA(()),
        ],
    )
    return pl.pallas_call(
        _dispatch_body,
        grid_spec=grid_spec,
        out_shape=jax.ShapeDtypeStruct((n_rows, sub, lanes), F32),
        compiler_params=_dma_params(("arbitrary",)),
        name="dispatch",
    )(meta, last_chunk, dest_p, dest_s, x1p, x1s)


def _experts_body(be_ref, bx_ref, bn_ref, bz_ref, xs_hbm, wg_ref, wu_ref, wd_ref, ys_hbm,
                  xbuf, obuf, gsem, osem, wgu_bf, wd_bf):
    b = pl.program_id(0)
    nb = pl.num_programs(0)
    slot = b % 2
    ff = wd_ref.shape[1]

    def in_copies(blk, s, nch):
        rows = pl.ds(bx_ref[blk] * GATHER_CHUNK, nch * GATHER_CHUNK)
        return [pltpu.make_async_copy(xs_hbm.at[rows, c, :],
                                      xbuf.at[s, pl.ds(0, nch * GATHER_CHUNK), pl.ds(c * LANES, LANES)],
                                      gsem.at[s]) for c in range(xs_hbm.shape[1])]

    def out_copies(blk, s, nch):
        rows = pl.ds(bx_ref[blk] * GATHER_CHUNK, nch * GATHER_CHUNK)
        return [pltpu.make_async_copy(obuf.at[s, pl.ds(0, nch * GATHER_CHUNK), pl.ds(c * LANES, LANES)],
                                      ys_hbm.at[rows, c, :], osem.at[s]) for c in range(ys_hbm.shape[1])]

    def for_block(count, copies, fn):
        for nch in range(1, BLOCK_CHUNKS + 1):
            @pl.when(count == nch)
            def _():
                for c in copies(nch):
                    fn(c)

    def n_in(blk):
        return jnp.where(bz_ref[blk] == 0, bn_ref[blk], 0)

    @pl.when(b == 0)
    def _():
        for_block(n_in(0), lambda k: in_copies(0, 0, k), lambda c: c.start())

    @pl.when(b + 1 < nb)
    def _():
        for_block(n_in(b + 1), lambda k: in_copies(b + 1, 1 - slot, k), lambda c: c.start())

    @pl.when(b >= 2)
    def _():
        for_block(bn_ref[b - 2], lambda k: out_copies(b - 2, slot, k), lambda c: c.wait())

    for_block(n_in(b), lambda k: in_copies(b, slot, k), lambda c: c.wait())

    e_now = be_ref[b]
    e_prev = be_ref[jnp.maximum(b - 1, 0)]

    @pl.when(jnp.logical_or(b == 0, e_now != e_prev))
    def _():
        wgu_bf[:, 0:ff] = wg_ref[0].astype(BF16)
        wgu_bf[:, ff:2 * ff] = wu_ref[0].astype(BF16)
        wd_bf[...] = wd_ref[0].astype(BF16)

    for nch in range(1, BLOCK_CHUNKS + 1):
        @pl.when(n_in(b) == nch)
        def _():
            rows = nch * GATHER_CHUNK
            gu = _dot(xbuf[slot, 0:rows, :].astype(BF16), wgu_bf[...])
            hid = _silu(gu[:, :ff]) * gu[:, ff:]
            obuf[slot, 0:rows, :] = _dot(hid.astype(BF16), wd_bf[...])

    @pl.when(bz_ref[b] != 0)
    def _():
        obuf[slot] = jnp.zeros(obuf.shape[1:], F32)

    for_block(bn_ref[b], lambda k: out_copies(b, slot, k), lambda c: c.start())

    @pl.when(b == nb - 1)
    def _():
        for_block(bn_ref[b], lambda k: out_copies(b, slot, k), lambda c: c.wait())

        @pl.when(nb > 1)
        def _():
            for_block(bn_ref[b - 1], lambda k: out_copies(b - 1, 1 - slot, k), lambda c: c.wait())


def _experts(block_e, block_x, block_n, block_z, xs, we_gate, we_up, we_down):
    nb = block_e.shape[0]
    n_rows = xs.shape[0]
    d = we_gate.shape[1]
    ff = we_gate.shape[2]
    rows = BLOCK_CHUNKS * GATHER_CHUNK
    grid_spec = pltpu.PrefetchScalarGridSpec(
        num_scalar_prefetch=4,
        grid=(nb,),
        in_specs=[
            pl.BlockSpec(memory_space=pl.ANY),
            pl.BlockSpec((1, d, ff), lambda b, be, *_: (be[b], 0, 0)),
            pl.BlockSpec((1, d, ff), lambda b, be, *_: (be[b], 0, 0)),
            pl.BlockSpec((1, ff, d), lambda b, be, *_: (be[b], 0, 0)),
        ],
        out_specs=pl.BlockSpec(memory_space=pl.ANY),
        scratch_shapes=[
            pltpu.VMEM((2, rows, d), F32),
            pltpu.VMEM((2, rows, d), F32),
            pltpu.SemaphoreType.DMA((2,)),
            pltpu.SemaphoreType.DMA((2,)),
            pltpu.VMEM((d, 2 * ff), BF16),
            pltpu.VMEM((ff, d), BF16),
        ],
    )
    return pl.pallas_call(
        _experts_body,
        grid_spec=grid_spec,
        out_shape=jax.ShapeDtypeStruct(xs.shape, F32),
        compiler_params=_dma_params(("arbitrary",)),
        name="experts",
    )(block_e, block_x, block_n, block_z, xs, we_gate, we_up, we_down)


def _combine_body(dt_hbm, wt_hbm, ys_hbm, part_ref, g_ref, b_ref, y_ref, idx_smem, w_smem, gbuf, isem, wsem, gsem):
    i = pl.program_id(0)
    n = pl.num_programs(0)
    slot = i % 2
    tm = part_ref.shape[0]

    def meta_copies(tile, s):
        return (_slot_index_copies(dt_hbm, tile * tm, tm, idx_smem, s, tm, isem.at[s]) +
                _slot_index_copies(wt_hbm, tile * tm, tm, w_smem, s, tm, wsem.at[s]))

    def issue_rows(s):
        for j in range(TOP_K):
            def body(p, carry):
                for u in range(DMA_THREADS):
                    t = DMA_THREADS * p + u
                    row = idx_smem[s, j * tm + t]
                    pltpu.make_async_copy(ys_hbm.at[row], gbuf.at[s, j, t], gsem.at[s]).start(priority=u)
                return carry
            lax.fori_loop(0, tm // DMA_THREADS, body, 0, unroll=4)

    def wait_rows(s):
        for j in range(TOP_K):
            pltpu.make_async_copy(ys_hbm.at[pl.ds(0, tm)], gbuf.at[s, j], gsem.at[s]).wait()

    @pl.when(i == 0)
    def _():
        for c in meta_copies(0, 0):
            c.start()
        for c in meta_copies(0, 0):
            c.wait()
        issue_rows(0)

        @pl.when(n > 1)
        def _():
            for c in meta_copies(1, 1):
                c.start()

    @pl.when(i + 1 < n)
    def _():
        for c in meta_copies(i + 1, 1 - slot):
            c.wait()
        issue_rows(1 - slot)

    wait_rows(slot)

    def token(t, carry):
        f = part_ref[t]
        for j in range(TOP_K):
            f = f + w_smem[slot, j * tm + t] * gbuf[slot, j, t]
        y_ref[t] = f
        return carry
    lax.fori_loop(0, tm, token, 0, unroll=8)

    @pl.when(i + 2 < n)
    def _():
        for c in meta_copies(i + 2, slot):
            c.start()

    f = y_ref[...]
    inv_d = 1.0 / (f.shape[1] * f.shape[2])
    tile_sum = lambda a: jnp.sum(jnp.sum(a, axis=2, keepdims=True), axis=1, keepdims=True)
    fc = f - tile_sum(f) * inv_d
    var = tile_sum(fc * fc) * inv_d
    y_ref[...] = fc * lax.rsqrt(var + LN_EPS) * g_ref[...] + b_ref[...]


def _combine(dest, wts, ys, part, ln_g, ln_b, tm):
    n, sub, lanes = part.shape
    return pl.pallas_call(
        _combine_body,
        grid=(n // tm,),
        in_specs=[
            pl.BlockSpec(memory_space=pl.ANY),
            pl.BlockSpec(memory_space=pl.ANY),
            pl.BlockSpec(memory_space=pl.ANY),
            pl.BlockSpec((tm, sub, lanes), lambda i: (i, 0, 0)),
            pl.BlockSpec((sub, lanes), lambda i: (0, 0)),
            pl.BlockSpec((sub, lanes), lambda i: (0, 0)),
        ],
        out_specs=pl.BlockSpec((tm, sub, lanes), lambda i: (i, 0, 0)),
        out_shape=jax.ShapeDtypeStruct((n, sub, lanes), F32),
        scratch_shapes=[
            pltpu.SMEM((2, TOP_K * tm), jnp.int32),
            pltpu.SMEM((2, TOP_K * tm), F32),
            pltpu.VMEM((2, TOP_K, tm, sub, lanes), F32),
            pltpu.SemaphoreType.DMA((2,)),
            pltpu.SemaphoreType.DMA((2,)),
            pltpu.SemaphoreType.DMA((2,)),
        ],
        compiler_params=_dma_params(("arbitrary",)),
        name="combine",
    )(dest, wts, ys, part, ln_g.reshape(sub, lanes), ln_b.reshape(sub, lanes))


def kernel(x_prompt, x_sample, cache_k, cache_v, state_conv, ln_in_g, ln_in_b, w_in, b_in, conv_w, conv_b,
           conv_ln_g, conv_ln_b, w_conv_out, w_attn_out, w_out, ln1_g, ln1_b, w_router, router_bias,
           we_gate, we_up, we_down, ws_gate, ws_up, ws_down, ln2_g, ln2_b):
    bp, tp, d = x_prompt.shape
    bs, ts, _ = x_sample.shape
    past = cache_k.shape[2]
    assert w_in.shape[0] == DEPTH and d == N_HEADS * HEAD_DIM and conv_w.shape[2] == d
    assert tp % 1024 == 0 and ts <= KEY_BLOCK and ts % 16 == 0 and past % KEY_BLOCK == 0
    n_p, n_s = bp * tp, bs * ts
    assert n_s % LANES == 0

    row = lambda a: a.reshape(1, -1)
    lng, lnb = row(ln_in_g), row(ln_in_b)
    w_in_b = w_in[0].astype(BF16)
    w_co_b = w_conv_out[0].astype(BF16)
    conv_args = (lng, lnb, w_in_b, b_in, conv_w[0], conv_b, conv_ln_g, conv_ln_b, w_co_b)

    qp, kp, vp, gcp, sap, csp = _proj_conv(x_prompt, jnp.zeros((bp, CONV_WIDTH - 1, d), F32), *conv_args, tm=256)
    op = _attn_prompt(qp, kp, vp, tq=512)

    qs, ks, vs, gcs, sas, css = _proj_conv(x_sample, state_conv[0], *conv_args, tm=ts)
    pad_new = lambda a: jnp.pad(a, ((0, 0), (0, KEY_BLOCK - ts), (0, 0)))
    os_ = _attn_sample(qs, pad_new(ks), pad_new(vs), cache_k[0].reshape(bs, past, d), cache_v[0].reshape(bs, past, d))

    wr = w_router[0].T
    wr_hi = wr.astype(BF16)
    wr_lo = (wr - wr_hi.astype(F32)).astype(BF16)
    post_args = (lng, lnb, w_attn_out[0].astype(BF16), w_out[0].astype(BF16), ln1_g, ln1_b, wr_hi, wr_lo,
                 router_bias.reshape(N_EXPERTS, 1),
                 jnp.concatenate([ws_gate[0], ws_up[0]], axis=1).astype(BF16), ws_down[0].astype(BF16))
    flat = lambda a, n: a.reshape(n, d)
    zero_cnt = jnp.zeros((N_EXPERTS, LANES), F32)
    x1p, partp, idxp, wtp, rankp, cnt_p = _post_attn(
        flat(op, n_p), flat(gcp, n_p), flat(sap, n_p), flat(x_prompt, n_p), zero_cnt, *post_args, tm=256)
    x1s, parts, idxs, wts, ranks, cnt = _post_attn(
        flat(os_, n_s), flat(gcs, n_s), flat(sas, n_s), flat(x_sample, n_s), cnt_p, *post_args, tm=n_s)

    i32 = jnp.int32
    n_tok = n_p + n_s
    total_chunks = (n_tok * TOP_K + N_EXPERTS * (GATHER_CHUNK - 1) + GATHER_CHUNK - 1) // GATHER_CHUNK
    n_blocks = total_chunks // BLOCK_CHUNKS + N_EXPERTS
    counts = cnt[:, 0].astype(i32)
    chunks = (counts + GATHER_CHUNK - 1) // GATHER_CHUNK
    chunk_end = jnp.cumsum(chunks)
    chunk_start = chunk_end - chunks
    row_start = chunk_start * GATHER_CHUNK
    used_chunks = chunk_end[-1]
    blocks = (chunks + BLOCK_CHUNKS - 1) // BLOCK_CHUNKS
    blk_end = jnp.cumsum(blocks)
    blk_start = blk_end - blocks
    bid = jnp.arange(n_blocks, dtype=i32)
    block_e = jnp.minimum(jnp.sum(bid[:, None] >= blk_end[None, :], axis=1), N_EXPERTS - 1).astype(i32)
    k_in_e = bid - blk_start[block_e]
    is_tail = bid >= blk_end[-1]
    tail_x = used_chunks + BLOCK_CHUNKS * (bid - blk_end[-1])
    block_x = jnp.where(is_tail, tail_x, chunk_start[block_e] + BLOCK_CHUNKS * k_in_e)
    block_n = jnp.where(is_tail, total_chunks - tail_x, chunks[block_e] - BLOCK_CHUNKS * k_in_e)
    block_n = jnp.clip(block_n, 0, BLOCK_CHUNKS).astype(i32)
    block_x = jnp.where(block_n > 0, block_x, 0).astype(i32)
    block_z = is_tail.astype(i32)
    meta = jnp.stack([used_chunks, used_chunks]).astype(i32)
    last_chunk = jnp.where(chunks > 0, chunk_end - 1, -1).astype(i32)

    rs_f = row_start.astype(F32).reshape(N_EXPERTS, 1)
    dest_p = _dest(idxp, rankp, rs_f, tm=1024)
    dest_s = _dest(idxs, ranks, rs_f, tm=n_s)

    xs = _dispatch(meta, last_chunk, dest_p, dest_s, x1p, x1s, total_chunks * GATHER_CHUNK, tm=256)
    ys = _experts(block_e, block_x, block_n, block_z, xs, we_gate[0], we_up[0], we_down[0])

    yp = _combine(dest_p, wtp, ys, partp, ln2_g, ln2_b, tm=128)
    ysm = _combine(dest_s, wts, ys, parts, ln2_g, ln2_b, tm=n_s)

    heads = lambda a, b, t: a.reshape(1, b, t, N_HEADS, HEAD_DIM)
    return (yp.reshape(bp, tp, d), ysm.reshape(bs, ts, d),
            heads(kp, bp, tp), heads(vp, bp, tp), csp[None],
            heads(ks, bs, ts), heads(vs, bs, ts), css[None])
```

```python
import functools

import jax
import jax.numpy as jnp
from jax import lax
from jax.experimental import pallas as pl
from jax.experimental.pallas import tpu as pltpu

F32 = jnp.float32
BF16 = jnp.bfloat16

N_HEADS = 16
HEAD_DIM = 64
CONV_WIDTH = 31
N_EXPERTS = 256
TOP_K = 8
N_GROUPS = 8
TOPK_GROUPS = 4
GROUP_SIZE = N_EXPERTS // N_GROUPS
ROUTED_SCALE = 2.5
LN_EPS = 1e-5
DEPTH = 1
DEEPNORM_ALPHA = (2 * DEPTH) ** 0.25
ATTN_SCALE = HEAD_DIM ** -0.5

LANES = 128
SUBLANES = 8
VMEM_LIMIT_BYTES = 56 * 1024 * 1024

KEY_BLOCK = 128
HIST_PAD = 32
GATHER_CHUNK = 128
BLOCK_CHUNKS = 4
DMA_THREADS = 2
LOG_DEAD = -110.0


def _layer_norm(x, g, b):
    mu = jnp.mean(x, axis=-1, keepdims=True)
    xc = x - mu
    var = jnp.mean(xc * xc, axis=-1, keepdims=True)
    return xc * lax.rsqrt(var + LN_EPS) * g + b


def _sigmoid(x):
    return 1.0 / (1.0 + jnp.exp(-x))


def _silu(x):
    return x * _sigmoid(x)


def _dot(a, b):
    return jnp.dot(a, b, preferred_element_type=F32)


def _dot_nt(a, b):
    return lax.dot_general(a, b, (((1,), (1,)), ((), ())), preferred_element_type=F32)


def _params(sem):
    return pltpu.CompilerParams(dimension_semantics=sem, vmem_limit_bytes=VMEM_LIMIT_BYTES)


def _dma_params(sem):
    return pltpu.CompilerParams(dimension_semantics=sem, vmem_limit_bytes=VMEM_LIMIT_BYTES,
                                disable_bounds_checks=True)


def _proj_conv_body(x_ref, hist_ref, lng_ref, lnb_ref, win_ref, bin_ref, cw_ref, cb_ref, clg_ref,
                    clb_ref, wco_ref, q_ref, k_ref, v_ref, gc_ref, sa_ref, cs_ref, ubuf, cbuf, ush):
    tm = x_ref.shape[1]
    d = x_ref.shape[2]
    hist = CONV_WIDTH - 1
    lead = HIST_PAD - hist

    @pl.when(pl.program_id(1) == 0)
    def _():
        ubuf[0:lead, :] = jnp.zeros((lead, d), F32)
        ubuf[lead:HIST_PAD, :] = hist_ref[0]

    hb = _layer_norm(x_ref[0], lng_ref[...], lnb_ref[...]).astype(BF16)

    def proj(i):
        return _dot(hb, win_ref[:, i * d:(i + 1) * d]) + bin_ref[:, i * d:(i + 1) * d]

    ubuf[HIST_PAD:HIST_PAD + tm, :] = proj(0) * _sigmoid(proj(1))
    q_ref[0] = (proj(2) * ATTN_SCALE).astype(BF16)
    k_ref[0] = proj(3)
    v_ref[0] = proj(4)
    sa_ref[0] = _sigmoid(proj(6)).astype(BF16)

    span = tm + HIST_PAD - SUBLANES
    for ph in range(1, SUBLANES):
        ush[ph - 1, 0:span, :] = ubuf[ph:ph + span, :]
    rows = min(tm, 32)
    for r0 in range(0, tm, rows):
        acc = jnp.broadcast_to(cb_ref[...], (rows, d))
        for kk in range(CONV_WIDTH):
            base, ph = divmod(lead + kk, SUBLANES)
            lo = base * SUBLANES + r0
            tap = ubuf[lo:lo + rows, :] if ph == 0 else ush[ph - 1, lo:lo + rows, :]
            acc = acc + cw_ref[kk:kk + 1, :] * tap
        cbuf[r0:r0 + rows, :] = _silu(_layer_norm(acc, clg_ref[...], clb_ref[...])).astype(BF16)
    gc_ref[0] = (_sigmoid(proj(5)) * _dot(cbuf[...], wco_ref[...])).astype(BF16)

    cs_ref[0] = ubuf[tm + lead:tm + HIST_PAD, :]
    ubuf[0:HIST_PAD, :] = ubuf[tm:tm + HIST_PAD, :]


def _proj_conv(x, hist, ln_g, ln_b, w_in, b_in, conv_w, conv_b, cln_g, cln_b, w_co, tm):
    bsz, t, d = x.shape
    pw = w_in.shape[1]
    const = lambda shape: pl.BlockSpec(shape, lambda b, i: (0,) * len(shape))
    resident = lambda shape: pl.BlockSpec(shape, lambda b, i: (0,) * len(shape), pipeline_mode=pl.Buffered(1))
    tile = lambda: pl.BlockSpec((1, tm, d), lambda b, i: (b, i, 0))
    return pl.pallas_call(
        _proj_conv_body,
        grid=(bsz, t // tm),
        in_specs=[
            tile(),
            pl.BlockSpec((1, CONV_WIDTH - 1, d), lambda b, i: (b, 0, 0)),
            const((1, d)), const((1, d)), resident((d, pw)), const((1, pw)),
            const((CONV_WIDTH, d)), const((1, d)), const((1, d)), const((1, d)), resident((d, d)),
        ],
        out_specs=[tile(), tile(), tile(), tile(), tile(),
                   pl.BlockSpec((1, CONV_WIDTH - 1, d), lambda b, i: (b, 0, 0))],
        out_shape=[
            jax.ShapeDtypeStruct((bsz, t, d), BF16),
            jax.ShapeDtypeStruct((bsz, t, d), F32),
            jax.ShapeDtypeStruct((bsz, t, d), F32),
            jax.ShapeDtypeStruct((bsz, t, d), BF16),
            jax.ShapeDtypeStruct((bsz, t, d), BF16),
            jax.ShapeDtypeStruct((bsz, CONV_WIDTH - 1, d), F32),
        ],
        scratch_shapes=[pltpu.VMEM((HIST_PAD + tm, d), F32), pltpu.VMEM((tm, d), BF16),
                        pltpu.VMEM((SUBLANES - 1, tm + HIST_PAD - SUBLANES, d), F32)],
        compiler_params=_params(("arbitrary", "arbitrary")),
        name="proj_conv",
    )(x, hist, ln_g, ln_b, w_in, b_in, conv_w, conv_b, cln_g, cln_b, w_co)


def _suffix_pair():
    row = lax.broadcasted_iota(jnp.int32, (2 * KEY_BLOCK, KEY_BLOCK), 0)
    col = lax.broadcasted_iota(jnp.int32, (2 * KEY_BLOCK, KEY_BLOCK), 1)
    row = jnp.where(row >= KEY_BLOCK, row - KEY_BLOCK, row)
    return jnp.where(row >= col, 1.0, 0.0).astype(BF16)


def _sb_blocks(qss, kbs, vbs, acc, ll, vis, suffix2):
    r = qss[0].shape[0]
    z = jnp.concatenate([_dot_nt(q, kb) for q, kb in zip(qss, kbs)], axis=0)
    lk = -(jnp.maximum(z, 0.0) + jnp.log1p(jnp.exp(-jnp.abs(z))))
    if vis is not None:
        lk = jnp.where(vis, lk, 0.0)
    hi = lk.astype(BF16)
    lo = (lk - hi.astype(F32)).astype(BF16)
    inb = _dot(jnp.concatenate([hi, lo], axis=1), suffix2)
    arg = z + inb + ll
    if vis is not None:
        arg = jnp.where(vis, arg, -jnp.inf)
    a = jnp.exp(arg).astype(BF16)
    av = jnp.concatenate([_dot(a[i * r:(i + 1) * r], vb) for i, vb in enumerate(vbs)], axis=0)
    return acc + av, ll + inb[:, 0:1]


def _sb_block(qs, kb, vb, acc, ll, vis, suffix2):
    return _sb_blocks([qs], [kb], [vb], acc, ll, vis, suffix2)


def _sb_older_blocks(qs, k_ref, v_ref, first, acc, ll, suffix2):
    def live(ll_):
        return (jnp.max(ll_, axis=0, keepdims=True)[0, 0] > LOG_DEAD).astype(jnp.int32)

    def cond(c):
        return jnp.logical_and(c[0] >= 0, c[3] > 0)

    def body(c):
        j, acc_, ll_, _ = c
        start = pl.multiple_of(j * KEY_BLOCK, KEY_BLOCK)
        kb = k_ref[0, pl.ds(start, KEY_BLOCK), :].astype(BF16)
        vb = v_ref[0, pl.ds(start, KEY_BLOCK), :].astype(BF16)
        acc_, ll_ = _sb_block(qs, kb, vb, acc_, ll_, None, suffix2)
        return j - 1, acc_, ll_, live(ll_)

    _, acc, _, _ = lax.while_loop(cond, body, (first, acc, ll, live(ll)))
    return acc


def _stack_heads(q):
    lane = lax.broadcasted_iota(jnp.int32, q.shape, 1)
    zero = jnp.zeros_like(q)
    return jnp.concatenate([jnp.where(lane < HEAD_DIM, q, zero), jnp.where(lane >= HEAD_DIM, q, zero)], axis=0)


def _row_in_head(rows, cols):
    r = lax.broadcasted_iota(jnp.int32, (rows, cols), 0)
    return jnp.where(r >= rows // 2, r - rows // 2, r)


def _unstack_heads(acc):
    r = acc.shape[0] // 2
    lane = lax.broadcasted_iota(jnp.int32, (r, LANES), 1)
    return jnp.where(lane < HEAD_DIM, acc[:r], acc[r:])


def _attn_prompt_body(q_ref, k_ref, v_ref, o_ref):
    tq = q_ref.shape[1]
    nq = tq // KEY_BLOCK
    r = 2 * KEY_BLOCK
    g0 = pl.program_id(2) * nq
    col = lax.broadcasted_iota(jnp.int32, (nq * r, KEY_BLOCK), 1)
    row = lax.broadcasted_iota(jnp.int32, (nq * r, KEY_BLOCK), 0) & (KEY_BLOCK - 1)
    vis = col < row
    suffix2 = _suffix_pair()

    def kv_blocks(js):
        starts = [pl.multiple_of(j * KEY_BLOCK, KEY_BLOCK) for j in js]
        return ([k_ref[0, pl.ds(s, KEY_BLOCK), :].astype(BF16) for s in starts],
                [v_ref[0, pl.ds(s, KEY_BLOCK), :].astype(BF16) for s in starts])

    def mask_finished(ll, off):
        return jnp.concatenate([jnp.where(g0 + qi - off >= 0, ll[qi * r:(qi + 1) * r], -jnp.inf)
                                for qi in range(nq)], axis=0)

    def any_live(ll, off):
        m = jnp.max(mask_finished(ll, off), axis=0, keepdims=True)
        return (m[0, 0] > LOG_DEAD).astype(jnp.int32)

    qss = [_stack_heads(q_ref[0, qi * KEY_BLOCK:(qi + 1) * KEY_BLOCK, :]) for qi in range(nq)]
    kbs, vbs = kv_blocks([g0 + qi for qi in range(nq)])
    acc, ll = _sb_blocks(qss, kbs, vbs, jnp.zeros((nq * r, LANES), F32), jnp.zeros((nq * r, 1), F32), vis,
                         suffix2)

    def cond(c):
        return c[3] > 0

    def body(c):
        off, acc_, ll_, _ = c
        kbs_, vbs_ = kv_blocks([jnp.maximum(g0 + qi - off, 0) for qi in range(nq)])
        acc_, ll_ = _sb_blocks(qss, kbs_, vbs_, acc_, mask_finished(ll_, off), None, suffix2)
        return off + 1, acc_, ll_, any_live(ll_, off + 1)

    one = jnp.int32(1)
    _, acc, _, _ = lax.while_loop(cond, body, (one, acc, ll, any_live(ll, one)))
    for qi in range(nq):
        o_ref[0, qi * KEY_BLOCK:(qi + 1) * KEY_BLOCK, :] = _unstack_heads(acc[qi * r:(qi + 1) * r]).astype(BF16)


def _attn_prompt(q, k, v, tq):
    bsz, t, d = q.shape
    pairs = d // LANES
    return pl.pallas_call(
        _attn_prompt_body,
        grid=(bsz, pairs, t // tq),
        in_specs=[
            pl.BlockSpec((1, tq, LANES), lambda b, p, i: (b, i, p)),
            pl.BlockSpec((1, t, LANES), lambda b, p, i: (b, 0, p)),
            pl.BlockSpec((1, t, LANES), lambda b, p, i: (b, 0, p)),
        ],
        out_specs=pl.BlockSpec((1, tq, LANES), lambda b, p, i: (b, i, p)),
        out_shape=jax.ShapeDtypeStruct((bsz, t, d), BF16),
        compiler_params=_params(("arbitrary", "arbitrary", "arbitrary")),
        name="attn_prompt",
    )(q, k, v)


def _attn_sample_body(q_ref, kn_ref, vn_ref, ck_ref, cv_ref, o_ref):
    tq = q_ref.shape[1]
    past_blocks = ck_ref.shape[1] // KEY_BLOCK
    qs = _stack_heads(q_ref[0])
    row = _row_in_head(2 * tq, KEY_BLOCK)
    col = lax.broadcasted_iota(jnp.int32, (2 * tq, KEY_BLOCK), 1)
    acc = jnp.zeros((2 * tq, LANES), F32)
    ll = jnp.zeros((2 * tq, 1), F32)
    suffix2 = _suffix_pair()
    acc, ll = _sb_block(qs, kn_ref[0].astype(BF16), vn_ref[0].astype(BF16), acc, ll, col < row, suffix2)
    acc = _sb_older_blocks(qs, ck_ref, cv_ref, past_blocks - 1, acc, ll, suffix2)
    o_ref[0] = _unstack_heads(acc).astype(BF16)


def _attn_sample(q, k_new, v_new, cache_k, cache_v):
    bsz, t, d = q.shape
    past = cache_k.shape[1]
    pairs = d // LANES
    return pl.pallas_call(
        _attn_sample_body,
        grid=(bsz, pairs),
        in_specs=[
            pl.BlockSpec((1, t, LANES), lambda b, p: (b, 0, p)),
            pl.BlockSpec((1, KEY_BLOCK, LANES), lambda b, p: (b, 0, p)),
            pl.BlockSpec((1, KEY_BLOCK, LANES), lambda b, p: (b, 0, p)),
            pl.BlockSpec((1, past, LANES), lambda b, p: (b, 0, p)),
            pl.BlockSpec((1, past, LANES), lambda b, p: (b, 0, p)),
        ],
        out_specs=pl.BlockSpec((1, t, LANES), lambda b, p: (b, 0, p)),
        out_shape=jax.ShapeDtypeStruct((bsz, t, d), BF16),
        compiler_params=_params(("arbitrary", "arbitrary")),
        name="attn_sample",
    )(q, k_new, v_new, cache_k, cache_v)


def _first_argmax(cur, iota_f, n):
    m = jnp.max(cur, axis=0, keepdims=True)
    i = jnp.min(jnp.where(cur == m, iota_f, float(n)), axis=0, keepdims=True)
    return m, i


def _store_row_tiles(ref, val):
    for s in range(val.shape[1] // LANES):
        ref[:, s, :] = val[:, s * LANES:(s + 1) * LANES]


def _post_attn_body(o_ref, gc_ref, sa_ref, x_ref, cnt_ref, lng_ref, lnb_ref, wao_ref, wo_ref, l1g_ref,
                    l1b_ref, wrh_ref, wrl_ref, rb_ref, wsgu_ref, wsd_ref,
                    x1_ref, part_ref, idx_ref, wts_ref, rank_ref, cnt_out_ref, run):
    tm = x_ref.shape[0]
    ff = wsd_ref.shape[0]

    @pl.when(pl.program_id(0) == 0)
    def _():
        run[...] = cnt_ref[...]

    h = _layer_norm(x_ref[...], lng_ref[...], lnb_ref[...])
    att = _dot(o_ref[...], wao_ref[...])
    merged = gc_ref[...].astype(F32) + sa_ref[...].astype(F32) * att
    mixed = _dot(merged.astype(BF16), wo_ref[...])
    x1 = _layer_norm(DEEPNORM_ALPHA * h + mixed, l1g_ref[...], l1b_ref[...])
    _store_row_tiles(x1_ref, x1)

    x1h = x1.astype(BF16)
    gu = _dot(x1h, wsgu_ref[...])
    shared = _dot((_silu(gu[:, :ff]) * gu[:, ff:]).astype(BF16), wsd_ref[...])
    _store_row_tiles(part_ref, DEEPNORM_ALPHA * x1 + shared)

    x1l = (x1 - x1h.astype(F32)).astype(BF16)
    logits = _dot_nt(wrh_ref[...], x1h) + _dot_nt(wrl_ref[...], x1h) + _dot_nt(wrh_ref[...], x1l)
    scores = _sigmoid(logits)
    choice = scores + rb_ref[...]
    neg = -jnp.inf

    giota = lax.broadcasted_iota(jnp.int32, (GROUP_SIZE, tm), 0).astype(F32)
    gs = []
    for g in range(N_GROUPS):
        blk = choice[g * GROUP_SIZE:(g + 1) * GROUP_SIZE, :]
        m1, i1 = _first_argmax(blk, giota, GROUP_SIZE)
        m2 = jnp.max(jnp.where(giota == i1, neg, blk), axis=0, keepdims=True)
        gs.append(m1 + m2)
    gscore = jnp.concatenate(gs, axis=0)

    g8 = lax.broadcasted_iota(jnp.int32, (N_GROUPS, tm), 0).astype(F32)
    gsel = jnp.zeros((N_GROUPS, tm), F32)
    cur = gscore
    for _ in range(TOPK_GROUPS):
        _, i = _first_argmax(cur, g8, N_GROUPS)
        hit = g8 == i
        gsel = jnp.where(hit, 1.0, gsel)
        cur = jnp.where(hit, neg, cur)
    emask = jnp.concatenate(
        [jnp.broadcast_to(gsel[g:g + 1, :], (GROUP_SIZE, tm)) for g in range(N_GROUPS)], axis=0)

    eiota = lax.broadcasted_iota(jnp.int32, (N_EXPERTS, tm), 0).astype(F32)
    cur = jnp.where(emask > 0.0, choice, neg)
    hits, sel_w = [], []
    for r in range(TOP_K):
        _, i = _first_argmax(cur, eiota, N_EXPERTS)
        hit = eiota == i
        hits.append(hit)
        sel_w.append(jnp.sum(jnp.where(hit, scores, 0.0), axis=0, keepdims=True))
        cur = jnp.where(hit, neg, cur)
        idx_ref[r:r + 1, :] = i.astype(jnp.int32)
    wsum = sel_w[0]
    for r in range(1, TOP_K):
        wsum = wsum + sel_w[r]
    for r in range(TOP_K):
        wts_ref[r:r + 1, :] = sel_w[r] / wsum * ROUTED_SCALE

    picked = jnp.zeros((N_EXPERTS, tm), F32)
    for r in range(TOP_K):
        picked = jnp.where(hits[r], 1.0, picked)
    picked_b = picked.astype(BF16)
    trow = lax.broadcasted_iota(jnp.int32, (tm, tm), 0)
    tcol = lax.broadcasted_iota(jnp.int32, (tm, tm), 1)
    earlier = jnp.where(trow < tcol, 1.0, 0.0).astype(BF16)
    before = _dot(picked_b, earlier) + run[:, 0:1]
    for r in range(TOP_K):
        rank_ref[r:r + 1, :] = jnp.sum(jnp.where(hits[r], before, 0.0), axis=0, keepdims=True).astype(jnp.int32)
    run[...] = run[...] + _dot(picked_b, jnp.ones((tm, LANES), BF16))
    cnt_out_ref[...] = run[...]


def _post_attn(o, gc, sa, x, cnt_in, ln_g, ln_b, w_ao, w_o, l1g, l1b, wr_hi, wr_lo, r_bias, ws_gu, ws_d, tm):
    n, d = x.shape
    ff = ws_d.shape[0]
    const = lambda shape: pl.BlockSpec(shape, lambda i: (0,) * len(shape))
    tile = lambda: pl.BlockSpec((tm, d), lambda i: (i, 0))
    small = lambda: pl.BlockSpec((TOP_K, tm), lambda i: (0, i))
    row_tiles = lambda: pl.BlockSpec((tm, d // LANES, LANES), lambda i: (i, 0, 0))
    return pl.pallas_call(
        _post_attn_body,
        grid=(n // tm,),
        in_specs=[tile(), tile(), tile(), tile(), const((N_EXPERTS, LANES)),
                  const((1, d)), const((1, d)), const((d, d)), const((d, d)), const((1, d)), const((1, d)),
                  const((N_EXPERTS, d)), const((N_EXPERTS, d)), const((N_EXPERTS, 1)),
                  const((d, 2 * ff)), const((ff, d))],
        out_specs=[row_tiles(), row_tiles(), small(), small(), small(), const((N_EXPERTS, LANES))],
        out_shape=[
            jax.ShapeDtypeStruct((n, d // LANES, LANES), F32),
            jax.ShapeDtypeStruct((n, d // LANES, LANES), F32),
            jax.ShapeDtypeStruct((TOP_K, n), jnp.int32),
            jax.ShapeDtypeStruct((TOP_K, n), F32),
            jax.ShapeDtypeStruct((TOP_K, n), jnp.int32),
            jax.ShapeDtypeStruct((N_EXPERTS, LANES), F32),
        ],
        scratch_shapes=[pltpu.VMEM((N_EXPERTS, LANES), F32)],
        compiler_params=_params(("arbitrary",)),
        name="post_attn",
    )(o, gc, sa, x, cnt_in, ln_g, ln_b, w_ao, w_o, l1g, l1b, wr_hi, wr_lo, r_bias, ws_gu, ws_d)


def _dest_body(idx_ref, rank_ref, rs_ref, dest_ref):
    tm = idx_ref.shape[1]
    eiota = lax.broadcasted_iota(jnp.int32, (N_EXPERTS, tm), 0)
    rs = rs_ref[...]
    for r in range(TOP_K):
        base = jnp.sum(jnp.where(eiota == idx_ref[r:r + 1, :], rs, 0.0), axis=0, keepdims=True)
        dest_ref[r:r + 1, :] = base.astype(jnp.int32) + rank_ref[r:r + 1, :]


def _dest(idx, rank, row_start_f, tm):
    n = idx.shape[1]
    small = lambda: pl.BlockSpec((TOP_K, tm), lambda i: (0, i))
    return pl.pallas_call(
        _dest_body,
        grid=(n // tm,),
        in_specs=[small(), small(), pl.BlockSpec((N_EXPERTS, 1), lambda i: (0, 0))],
        out_specs=small(),
        out_shape=jax.ShapeDtypeStruct((TOP_K, n), jnp.int32),
        compiler_params=_params(("arbitrary",)),
        name="dest",
    )(idx, rank, row_start_f)


def _slot_index_copies(src_hbm, col0, width, dst_smem, s, stride, sem):
    return [pltpu.make_async_copy(src_hbm.at[pl.ds(j, 1), pl.ds(col0, width)],
                                  dst_smem.at[pl.ds(s, 1), pl.ds(j * stride, width)], sem)
            for j in range(TOP_K)]


def _dispatch_body(meta_ref, lc_ref, dp_hbm, ds_hbm, xp_ref, xsm_ref, xs_hbm, idx_smem, zbuf, isem, ssem, zsem):
    i = pl.program_id(0)
    n = pl.num_programs(0)
    slot = i % 2
    tm = xp_ref.shape[0]
    n_s = xsm_ref.shape[0]
    total_chunks = xs_hbm.shape[0] // GATHER_CHUNK

    def idx_copies(tile, s, fn):
        @pl.when(tile < n - 1)
        def _():
            for c in _slot_index_copies(dp_hbm, tile * tm, tm, idx_smem, s, tm, isem.at[s]):
                fn(c)

        @pl.when(tile == n - 1)
        def _():
            for c in _slot_index_copies(ds_hbm, 0, n_s, idx_smem, s, tm, isem.at[s]):
                fn(c)

    @pl.when(i == 0)
    def _():
        idx_copies(0, 0, lambda c: c.start())

    idx_copies(i, slot, lambda c: c.wait())

    @pl.when(i + 1 < n)
    def _():
        idx_copies(i + 1, 1 - slot, lambda c: c.start())

    def scatter(src_ref):
        cnt = src_ref.shape[0]
        for j in range(TOP_K):
            def body(p, carry):
                for u in range(DMA_THREADS):
                    t = DMA_THREADS * p + u
                    row = idx_smem[slot, j * tm + t]
                    pltpu.make_async_copy(src_ref.at[t], xs_hbm.at[row], ssem).start(priority=u)
                return carry
            lax.fori_loop(0, cnt // DMA_THREADS, body, 0, unroll=4)
        for j in range(TOP_K):
            pltpu.make_async_copy(src_ref, xs_hbm.at[pl.ds(0, cnt)], ssem).wait()

    @pl.when(i == 0)
    def _():
        zbuf[...] = jnp.zeros(zbuf.shape, F32)

        def zero_chunk(c):
            return pltpu.make_async_copy(zbuf, xs_hbm.at[pl.ds(c * GATHER_CHUNK, GATHER_CHUNK)], zsem)

        def each_expert(fn):
            def body(e, carry):
                @pl.when(lc_ref[e] >= 0)
                def _():
                    fn(lc_ref[e])
                return carry
            lax.fori_loop(0, N_EXPERTS, body, 0)

        def each_tail(fn):
            def body(c, carry):
                fn(c)
                return carry
            lax.fori_loop(meta_ref[0], total_chunks, body, 0)

        each_expert(lambda c: zero_chunk(c).start())
        each_tail(lambda c: zero_chunk(c).start())
        each_expert(lambda c: zero_chunk(c).wait())
        each_tail(lambda c: zero_chunk(c).wait())

    @pl.when(i < n - 1)
    def _():
        scatter(xp_ref)

    @pl.when(i == n - 1)
    def _():
        scatter(xsm_ref)


def _dispatch(meta, last_chunk, dest_p, dest_s, x1p, x1s, n_rows, tm):
    n_p, sub, lanes = x1p.shape
    n_s = x1s.shape[0]
    n_pt = n_p // tm
    grid_spec = pltpu.PrefetchScalarGridSpec(
        num_scalar_prefetch=2,
        grid=(n_pt + 1,),
        in_specs=[
            pl.BlockSpec(memory_space=pl.ANY),
            pl.BlockSpec(memory_space=pl.ANY),
            pl.BlockSpec((tm, sub, lanes), lambda i, *_: (jnp.minimum(i, n_pt - 1), 0, 0)),
            pl.BlockSpec((n_s, sub, lanes), lambda i, *_: (0, 0, 0)),
        ],
        out_specs=pl.BlockSpec(memory_space=pl.ANY),
        scratch_shapes=[
            pltpu.SMEM((2, TOP_K * tm), jnp.int32),
            pltpu.VMEM((GATHER_CHUNK, sub, lanes), F32),
            pltpu.SemaphoreType.DMA((2,)),
            pltpu.SemaphoreType.DMA(()),
            pltpu.SemaphoreType.DMA(()),
        ],
    )
    return pl.pallas_call(
        _dispatch_body,
        grid_spec=grid_spec,
        out_shape=jax.ShapeDtypeStruct((n_rows, sub, lanes), F32),
        compiler_params=_dma_params(("arbitrary",)),
        name="dispatch",
    )(meta, last_chunk, dest_p, dest_s, x1p, x1s)


def _experts_body(be_ref, bx_ref, bn_ref, bz_ref, xs_hbm, wg_ref, wu_ref, wd_ref, ys_hbm,
                  xbuf, obuf, gsem, osem, wgu_bf, wd_bf):
    b = pl.program_id(0)
    nb = pl.num_programs(0)
    slot = b % 2
    ff = wd_ref.shape[1]

    def in_copies(blk, s, nch):
        rows = pl.ds(bx_ref[blk] * GATHER_CHUNK, nch * GATHER_CHUNK)
        return [pltpu.make_async_copy(xs_hbm.at[rows, c, :],
                                      xbuf.at[s, pl.ds(0, nch * GATHER_CHUNK), pl.ds(c * LANES, LANES)],
                                      gsem.at[s]) for c in range(xs_hbm.shape[1])]

    def out_copies(blk, s, nch):
        rows = pl.ds(bx_ref[blk] * GATHER_CHUNK, nch * GATHER_CHUNK)
        return [pltpu.make_async_copy(obuf.at[s, pl.ds(0, nch * GATHER_CHUNK), pl.ds(c * LANES, LANES)],
                                      ys_hbm.at[rows, c, :], osem.at[s]) for c in range(ys_hbm.shape[1])]

    def for_block(count, copies, fn):
        for nch in range(1, BLOCK_CHUNKS + 1):
            @pl.when(count == nch)
            def _():
                for c in copies(nch):
                    fn(c)

    def n_in(blk):
        return jnp.where(bz_ref[blk] == 0, bn_ref[blk], 0)

    @pl.when(b == 0)
    def _():
        for_block(n_in(0), lambda k: in_copies(0, 0, k), lambda c: c.start())

    @pl.when(b + 1 < nb)
    def _():
        for_block(n_in(b + 1), lambda k: in_copies(b + 1, 1 - slot, k), lambda c: c.start())

    @pl.when(b >= 2)
    def _():
        for_block(bn_ref[b - 2], lambda k: out_copies(b - 2, slot, k), lambda c: c.wait())

    for_block(n_in(b), lambda k: in_copies(b, slot, k), lambda c: c.wait())

    e_now = be_ref[b]
    e_prev = be_ref[jnp.maximum(b - 1, 0)]

    @pl.when(jnp.logical_or(b == 0, e_now != e_prev))
    def _():
        wgu_bf[:, 0:ff] = wg_ref[0].astype(BF16)
        wgu_bf[:, ff:2 * ff] = wu_ref[0].astype(BF16)
        wd_bf[...] = wd_ref[0].astype(BF16)

    for nch in range(1, BLOCK_CHUNKS + 1):
        @pl.when(n_in(b) == nch)
        def _():
            rows = nch * GATHER_CHUNK
            gu = _dot(xbuf[slot, 0:rows, :].astype(BF16), wgu_bf[...])
            hid = _silu(gu[:, :ff]) * gu[:, ff:]
            obuf[slot, 0:rows, :] = _dot(hid.astype(BF16), wd_bf[...])

    @pl.when(bz_ref[b] != 0)
    def _():
        obuf[slot] = jnp.zeros(obuf.shape[1:], F32)

    for_block(bn_ref[b], lambda k: out_copies(b, slot, k), lambda c: c.start())

    @pl.when(b == nb - 1)
    def _():
        for_block(bn_ref[b], lambda k: out_copies(b, slot, k), lambda c: c.wait())

        @pl.when(nb > 1)
        def _():
            for_block(bn_ref[b - 1], lambda k: out_copies(b - 1, 1 - slot, k), lambda c: c.wait())


def _experts(block_e, block_x, block_n, block_z, xs, we_gate, we_up, we_down):
    nb = block_e.shape[0]
    n_rows = xs.shape[0]
    d = we_gate.shape[1]
    ff = we_gate.shape[2]
    rows = BLOCK_CHUNKS * GATHER_CHUNK
    grid_spec = pltpu.PrefetchScalarGridSpec(
        num_scalar_prefetch=4,
        grid=(nb,),
        in_specs=[
            pl.BlockSpec(memory_space=pl.ANY),
            pl.BlockSpec((1, d, ff), lambda b, be, *_: (be[b], 0, 0)),
            pl.BlockSpec((1, d, ff), lambda b, be, *_: (be[b], 0, 0)),
            pl.BlockSpec((1, ff, d), lambda b, be, *_: (be[b], 0, 0)),
        ],
        out_specs=pl.BlockSpec(memory_space=pl.ANY),
        scratch_shapes=[
            pltpu.VMEM((2, rows, d), F32),
            pltpu.VMEM((2, rows, d), F32),
            pltpu.SemaphoreType.DMA((2,)),
            pltpu.SemaphoreType.DMA((2,)),
            pltpu.VMEM((d, 2 * ff), BF16),
            pltpu.VMEM((ff, d), BF16),
        ],
    )
    return pl.pallas_call(
        _experts_body,
        grid_spec=grid_spec,
        out_shape=jax.ShapeDtypeStruct(xs.shape, F32),
        compiler_params=_dma_params(("arbitrary",)),
        name="experts",
    )(block_e, block_x, block_n, block_z, xs, we_gate, we_up, we_down)


def _combine_body(dt_hbm, wt_hbm, ys_hbm, part_ref, g_ref, b_ref, y_ref, idx_smem, w_smem, gbuf, isem, wsem, gsem):
    i = pl.program_id(0)
    n = pl.num_programs(0)
    tm = gbuf.shape[2]
    per_step = part_ref.shape[0] // tm
    more = i + 1 < n

    def idx_copies(tile, s):
        return _slot_index_copies(dt_hbm, tile * tm, tm, idx_smem, s, tm, isem.at[s])

    def w_copies(tile, s):
        return _slot_index_copies(wt_hbm, tile * tm, tm, w_smem, s, tm, wsem.at[s])

    def start(copies):
        for c in copies:
            c.start()

    def wait(copies):
        for c in copies:
            c.wait()

    def issue_rows(s):
        for j in range(TOP_K):
            def body(p, carry):
                for u in range(DMA_THREADS):
                    t = DMA_THREADS * p + u
                    row = idx_smem[s, j * tm + t]
                    pltpu.make_async_copy(ys_hbm.at[row], gbuf.at[s, j, t], gsem.at[s]).start(priority=u)
                return carry
            lax.fori_loop(0, tm // DMA_THREADS, body, 0, unroll=4)

    def finish_tile(s):
        for j in range(TOP_K):
            pltpu.make_async_copy(ys_hbm.at[pl.ds(0, tm)], gbuf.at[s, j], gsem.at[s]).wait()
        wait(w_copies(0, s))
        base = s * tm

        def token(t, carry):
            f = part_ref[base + t]
            for j in range(TOP_K):
                f = f + w_smem[s, j * tm + t] * gbuf[s, j, t]
            y_ref[base + t] = f
            return carry
        lax.fori_loop(0, tm, token, 0, unroll=8)

        f = y_ref[base:base + tm]
        inv_d = 1.0 / (f.shape[1] * f.shape[2])
        tile_sum = lambda a: jnp.sum(jnp.sum(a, axis=2, keepdims=True), axis=1, keepdims=True)
        fc = f - tile_sum(f) * inv_d
        var = tile_sum(fc * fc) * inv_d
        y_ref[base:base + tm] = fc * lax.rsqrt(var + LN_EPS) * g_ref[...] + b_ref[...]

    first = i * per_step

    @pl.when(i == 0)
    def _():
        start(idx_copies(0, 0))
        start(w_copies(0, 0))
        wait(idx_copies(0, 0))
        issue_rows(0)
        if per_step == 2:
            start(idx_copies(1, 1))
            start(w_copies(1, 1))

    if per_step == 1:
        finish_tile(0)

        @pl.when(more)
        def _():
            start(idx_copies(first + 1, 0))
            start(w_copies(first + 1, 0))
            wait(idx_copies(first + 1, 0))
            issue_rows(0)
    else:
        wait(idx_copies(first + 1, 1))
        issue_rows(1)

        @pl.when(more)
        def _():
            start(idx_copies(first + 2, 0))

        finish_tile(0)

        @pl.when(more)
        def _():
            start(w_copies(first + 2, 0))
            wait(idx_copies(first + 2, 0))
            issue_rows(0)
            start(idx_copies(first + 3, 1))

        finish_tile(1)

        @pl.when(more)
        def _():
            start(w_copies(first + 3, 1))


def _combine(dest, wts, ys, part, ln_g, ln_b, tm, per_step):
    n, sub, lanes = part.shape
    rows = tm * per_step
    return pl.pallas_call(
        _combine_body,
        grid=(n // rows,),
        in_specs=[
            pl.BlockSpec(memory_space=pl.ANY),
            pl.BlockSpec(memory_space=pl.ANY),
            pl.BlockSpec(memory_space=pl.ANY),
            pl.BlockSpec((rows, sub, lanes), lambda i: (i, 0, 0)),
            pl.BlockSpec((sub, lanes), lambda i: (0, 0)),
            pl.BlockSpec((sub, lanes), lambda i: (0, 0)),
        ],
        out_specs=pl.BlockSpec((rows, sub, lanes), lambda i: (i, 0, 0)),
        out_shape=jax.ShapeDtypeStruct((n, sub, lanes), F32),
        scratch_shapes=[
            pltpu.SMEM((2, TOP_K * tm), jnp.int32),
            pltpu.SMEM((2, TOP_K * tm), F32),
            pltpu.VMEM((2, TOP_K, tm, sub, lanes), F32),
            pltpu.SemaphoreType.DMA((2,)),
            pltpu.SemaphoreType.DMA((2,)),
            pltpu.SemaphoreType.DMA((2,)),
        ],
        compiler_params=_dma_params(("arbitrary",)),
        name="combine",
    )(dest, wts, ys, part, ln_g.reshape(sub, lanes), ln_b.reshape(sub, lanes))


def kernel(x_prompt, x_sample, cache_k, cache_v, state_conv, ln_in_g, ln_in_b, w_in, b_in, conv_w, conv_b,
           conv_ln_g, conv_ln_b, w_conv_out, w_attn_out, w_out, ln1_g, ln1_b, w_router, router_bias,
           we_gate, we_up, we_down, ws_gate, ws_up, ws_down, ln2_g, ln2_b):
    bp, tp, d = x_prompt.shape
    bs, ts, _ = x_sample.shape
    past = cache_k.shape[2]
    assert w_in.shape[0] == DEPTH and d == N_HEADS * HEAD_DIM and conv_w.shape[2] == d
    assert tp % 1024 == 0 and ts <= KEY_BLOCK and ts % 16 == 0 and past % KEY_BLOCK == 0
    n_p, n_s = bp * tp, bs * ts
    assert n_s % LANES == 0

    row = lambda a: a.reshape(1, -1)
    lng, lnb = row(ln_in_g), row(ln_in_b)
    w_in_b = w_in[0].astype(BF16)
    w_co_b = w_conv_out[0].astype(BF16)
    conv_args = (lng, lnb, w_in_b, b_in, conv_w[0], conv_b, conv_ln_g, conv_ln_b, w_co_b)

    qp, kp, vp, gcp, sap, csp = _proj_conv(x_prompt, jnp.zeros((bp, CONV_WIDTH - 1, d), F32), *conv_args, tm=256)
    op = _attn_prompt(qp, kp, vp, tq=512)

    qs, ks, vs, gcs, sas, css = _proj_conv(x_sample, state_conv[0], *conv_args, tm=ts)
    pad_new = lambda a: jnp.pad(a, ((0, 0), (0, KEY_BLOCK - ts), (0, 0)))
    os_ = _attn_sample(qs, pad_new(ks), pad_new(vs), cache_k[0].reshape(bs, past, d), cache_v[0].reshape(bs, past, d))

    wr = w_router[0].T
    wr_hi = wr.astype(BF16)
    wr_lo = (wr - wr_hi.astype(F32)).astype(BF16)
    post_args = (lng, lnb, w_attn_out[0].astype(BF16), w_out[0].astype(BF16), ln1_g, ln1_b, wr_hi, wr_lo,
                 router_bias.reshape(N_EXPERTS, 1),
                 jnp.concatenate([ws_gate[0], ws_up[0]], axis=1).astype(BF16), ws_down[0].astype(BF16))
    flat = lambda a, n: a.reshape(n, d)
    zero_cnt = jnp.zeros((N_EXPERTS, LANES), F32)
    x1p, partp, idxp, wtp, rankp, cnt_p = _post_attn(
        flat(op, n_p), flat(gcp, n_p), flat(sap, n_p), flat(x_prompt, n_p), zero_cnt, *post_args, tm=256)
    x1s, parts, idxs, wts, ranks, cnt = _post_attn(
        flat(os_, n_s), flat(gcs, n_s), flat(sas, n_s), flat(x_sample, n_s), cnt_p, *post_args, tm=n_s)

    i32 = jnp.int32
    n_tok = n_p + n_s
    total_chunks = (n_tok * TOP_K + N_EXPERTS * (GATHER_CHUNK - 1) + GATHER_CHUNK - 1) // GATHER_CHUNK
    n_blocks = total_chunks // BLOCK_CHUNKS + N_EXPERTS
    counts = cnt[:, 0].astype(i32)
    chunks = (counts + GATHER_CHUNK - 1) // GATHER_CHUNK
    chunk_end = jnp.cumsum(chunks)
    chunk_start = chunk_end - chunks
    row_start = chunk_start * GATHER_CHUNK
    used_chunks = chunk_end[-1]
    blocks = (chunks + BLOCK_CHUNKS - 1) // BLOCK_CHUNKS
    blk_end = jnp.cumsum(blocks)
    blk_start = blk_end - blocks
    bid = jnp.arange(n_blocks, dtype=i32)
    block_e = jnp.minimum(jnp.sum(bid[:, None] >= blk_end[None, :], axis=1), N_EXPERTS - 1).astype(i32)
    k_in_e = bid - blk_start[block_e]
    is_tail = bid >= blk_end[-1]
    tail_x = used_chunks + BLOCK_CHUNKS * (bid - blk_end[-1])
    block_x = jnp.where(is_tail, tail_x, chunk_start[block_e] + BLOCK_CHUNKS * k_in_e)
    block_n = jnp.where(is_tail, total_chunks - tail_x, chunks[block_e] - BLOCK_CHUNKS * k_in_e)
    block_n = jnp.clip(block_n, 0, BLOCK_CHUNKS).astype(i32)
    block_x = jnp.where(block_n > 0, block_x, 0).astype(i32)
    block_z = is_tail.astype(i32)
    meta = jnp.stack([used_chunks, used_chunks]).astype(i32)
    last_chunk = jnp.where(chunks > 0, chunk_end - 1, -1).astype(i32)

    rs_f = row_start.astype(F32).reshape(N_EXPERTS, 1)
    dest_p = _dest(idxp, rankp, rs_f, tm=1024)
    dest_s = _dest(idxs, ranks, rs_f, tm=n_s)

    xs = _dispatch(meta, last_chunk, dest_p, dest_s, x1p, x1s, total_chunks * GATHER_CHUNK, tm=512)
    ys = _experts(block_e, block_x, block_n, block_z, xs, we_gate[0], we_up[0], we_down[0])

    yp = _combine(dest_p, wtp, ys, partp, ln2_g, ln2_b, tm=128, per_step=2)
    ysm = _combine(dest_s, wts, ys, parts, ln2_g, ln2_b, tm=n_s, per_step=1)

    heads = lambda a, b, t: a.reshape(1, b, t, N_HEADS, HEAD_DIM)
    return (yp.reshape(bp, tp, d), ysm.reshape(bs, ts, d),
            heads(kp, bp, tp), heads(vp, bp, tp), csp[None],
            heads(ks, bs, ts), heads(vs, bs, ts), css[None])
```

```python
import functools

import jax
import jax.numpy as jnp
from jax import lax
from jax.experimental import pallas as pl
from jax.experimental.pallas import tpu as pltpu

F32 = jnp.float32
BF16 = jnp.bfloat16

N_HEADS = 16
HEAD_DIM = 64
CONV_WIDTH = 31
N_EXPERTS = 256
TOP_K = 8
N_GROUPS = 8
TOPK_GROUPS = 4
GROUP_SIZE = N_EXPERTS // N_GROUPS
ROUTED_SCALE = 2.5
LN_EPS = 1e-5
DEPTH = 1
DEEPNORM_ALPHA = (2 * DEPTH) ** 0.25
ATTN_SCALE = HEAD_DIM ** -0.5

LANES = 128
SUBLANES = 8
VMEM_LIMIT_BYTES = 56 * 1024 * 1024

KEY_BLOCK = 128
HIST_PAD = 32
GATHER_CHUNK = 128
BLOCK_CHUNKS = 4
DMA_THREADS = 2
LOG_DEAD = -110.0


def _layer_norm(x, g, b):
    mu = jnp.mean(x, axis=-1, keepdims=True)
    xc = x - mu
    var = jnp.mean(xc * xc, axis=-1, keepdims=True)
    return xc * lax.rsqrt(var + LN_EPS) * g + b


def _sigmoid(x):
    return 1.0 / (1.0 + jnp.exp(-x))


def _silu(x):
    return x * _sigmoid(x)


def _dot(a, b):
    return jnp.dot(a, b, preferred_element_type=F32)


def _dot_nt(a, b):
    return lax.dot_general(a, b, (((1,), (1,)), ((), ())), preferred_element_type=F32)


def _params(sem):
    return pltpu.CompilerParams(dimension_semantics=sem, vmem_limit_bytes=VMEM_LIMIT_BYTES)


def _dma_params(sem):
    return pltpu.CompilerParams(dimension_semantics=sem, vmem_limit_bytes=VMEM_LIMIT_BYTES,
                                disable_bounds_checks=True)


def _proj_conv_body(x_ref, hist_ref, lng_ref, lnb_ref, win_ref, bin_ref, cw_ref, cb_ref, clg_ref,
                    clb_ref, wco_ref, q_ref, k_ref, v_ref, gc_ref, sa_ref, cs_ref, ubuf, cbuf, ush):
    tm = x_ref.shape[1]
    d = x_ref.shape[2]
    hist = CONV_WIDTH - 1
    lead = HIST_PAD - hist

    @pl.when(pl.program_id(1) == 0)
    def _():
        ubuf[0:lead, :] = jnp.zeros((lead, d), F32)
        ubuf[lead:HIST_PAD, :] = hist_ref[0]

    hb = _layer_norm(x_ref[0], lng_ref[...], lnb_ref[...]).astype(BF16)

    def proj(i):
        return _dot(hb, win_ref[:, i * d:(i + 1) * d]) + bin_ref[:, i * d:(i + 1) * d]

    ubuf[HIST_PAD:HIST_PAD + tm, :] = proj(0) * _sigmoid(proj(1))
    q_ref[0] = (proj(2) * ATTN_SCALE).astype(BF16)
    k_ref[0] = proj(3)
    v_ref[0] = proj(4)
    sa_ref[0] = _sigmoid(proj(6)).astype(BF16)

    span = tm + HIST_PAD - SUBLANES
    for ph in range(1, SUBLANES):
        ush[ph - 1, 0:span, :] = ubuf[ph:ph + span, :]
    rows = min(tm, 32)
    for r0 in range(0, tm, rows):
        acc = jnp.broadcast_to(cb_ref[...], (rows, d))
        for kk in range(CONV_WIDTH):
            base, ph = divmod(lead + kk, SUBLANES)
            lo = base * SUBLANES + r0
            tap = ubuf[lo:lo + rows, :] if ph == 0 else ush[ph - 1, lo:lo + rows, :]
            acc = acc + cw_ref[kk:kk + 1, :] * tap
        cbuf[r0:r0 + rows, :] = _silu(_layer_norm(acc, clg_ref[...], clb_ref[...])).astype(BF16)
    gc_ref[0] = (_sigmoid(proj(5)) * _dot(cbuf[...], wco_ref[...])).astype(BF16)

    cs_ref[0] = ubuf[tm + lead:tm + HIST_PAD, :]
    ubuf[0:HIST_PAD, :] = ubuf[tm:tm + HIST_PAD, :]


def _proj_conv(x, hist, ln_g, ln_b, w_in, b_in, conv_w, conv_b, cln_g, cln_b, w_co, tm):
    bsz, t, d = x.shape
    pw = w_in.shape[1]
    const = lambda shape: pl.BlockSpec(shape, lambda b, i: (0,) * len(shape))
    resident = lambda shape: pl.BlockSpec(shape, lambda b, i: (0,) * len(shape), pipeline_mode=pl.Buffered(1))
    tile = lambda: pl.BlockSpec((1, tm, d), lambda b, i: (b, i, 0))
    return pl.pallas_call(
        _proj_conv_body,
        grid=(bsz, t // tm),
        in_specs=[
            tile(),
            pl.BlockSpec((1, CONV_WIDTH - 1, d), lambda b, i: (b, 0, 0)),
            const((1, d)), const((1, d)), resident((d, pw)), const((1, pw)),
            const((CONV_WIDTH, d)), const((1, d)), const((1, d)), const((1, d)), resident((d, d)),
        ],
        out_specs=[tile(), tile(), tile(), tile(), tile(),
                   pl.BlockSpec((1, CONV_WIDTH - 1, d), lambda b, i: (b, 0, 0))],
        out_shape=[
            jax.ShapeDtypeStruct((bsz, t, d), BF16),
            jax.ShapeDtypeStruct((bsz, t, d), F32),
            jax.ShapeDtypeStruct((bsz, t, d), F32),
            jax.ShapeDtypeStruct((bsz, t, d), BF16),
            jax.ShapeDtypeStruct((bsz, t, d), BF16),
            jax.ShapeDtypeStruct((bsz, CONV_WIDTH - 1, d), F32),
        ],
        scratch_shapes=[pltpu.VMEM((HIST_PAD + tm, d), F32), pltpu.VMEM((tm, d), BF16),
                        pltpu.VMEM((SUBLANES - 1, tm + HIST_PAD - SUBLANES, d), F32)],
        compiler_params=_params(("arbitrary", "arbitrary")),
        name="proj_conv",
    )(x, hist, ln_g, ln_b, w_in, b_in, conv_w, conv_b, cln_g, cln_b, w_co)


def _suffix_pair():
    row = lax.broadcasted_iota(jnp.int32, (2 * KEY_BLOCK, KEY_BLOCK), 0)
    col = lax.broadcasted_iota(jnp.int32, (2 * KEY_BLOCK, KEY_BLOCK), 1)
    row = jnp.where(row >= KEY_BLOCK, row - KEY_BLOCK, row)
    return jnp.where(row >= col, 1.0, 0.0).astype(BF16)


def _sb_blocks(qss, kbs, vbs, acc, ll, vis, suffix2):
    r = qss[0].shape[0]
    z = jnp.concatenate([_dot_nt(q, kb) for q, kb in zip(qss, kbs)], axis=0)
    lk = -(jnp.maximum(z, 0.0) + jnp.log(1.0 + jnp.exp(-jnp.abs(z))))
    if vis is not None:
        lk = jnp.where(vis, lk, 0.0)
    hi = lk.astype(BF16)
    lo = (lk - hi.astype(F32)).astype(BF16)
    inb = _dot(jnp.concatenate([hi, lo], axis=1), suffix2)
    arg = z + inb + ll
    if vis is not None:
        arg = jnp.where(vis, arg, -jnp.inf)
    a = jnp.exp(arg).astype(BF16)
    av = jnp.concatenate([_dot(a[i * r:(i + 1) * r], vb) for i, vb in enumerate(vbs)], axis=0)
    return acc + av, ll + inb[:, 0:1]


def _sb_block(qs, kb, vb, acc, ll, vis, suffix2):
    return _sb_blocks([qs], [kb], [vb], acc, ll, vis, suffix2)


def _sb_older_blocks(qs, k_ref, v_ref, first, acc, ll, suffix2):
    def live(ll_):
        return (jnp.max(ll_, axis=0, keepdims=True)[0, 0] > LOG_DEAD).astype(jnp.int32)

    def cond(c):
        return jnp.logical_and(c[0] >= 0, c[3] > 0)

    def body(c):
        j, acc_, ll_, _ = c
        start = pl.multiple_of(j * KEY_BLOCK, KEY_BLOCK)
        kb = k_ref[0, pl.ds(start, KEY_BLOCK), :].astype(BF16)
        vb = v_ref[0, pl.ds(start, KEY_BLOCK), :].astype(BF16)
        acc_, ll_ = _sb_block(qs, kb, vb, acc_, ll_, None, suffix2)
        return j - 1, acc_, ll_, live(ll_)

    _, acc, _, _ = lax.while_loop(cond, body, (first, acc, ll, live(ll)))
    return acc


def _stack_heads(q):
    lane = lax.broadcasted_iota(jnp.int32, q.shape, 1)
    zero = jnp.zeros_like(q)
    return jnp.concatenate([jnp.where(lane < HEAD_DIM, q, zero), jnp.where(lane >= HEAD_DIM, q, zero)], axis=0)


def _row_in_head(rows, cols):
    r = lax.broadcasted_iota(jnp.int32, (rows, cols), 0)
    return jnp.where(r >= rows // 2, r - rows // 2, r)


def _unstack_heads(acc):
    r = acc.shape[0] // 2
    lane = lax.broadcasted_iota(jnp.int32, (r, LANES), 1)
    return jnp.where(lane < HEAD_DIM, acc[:r], acc[r:])


def _attn_prompt_body(q_ref, k_ref, v_ref, o_ref):
    tq = q_ref.shape[1]
    nq = tq // KEY_BLOCK
    r = 2 * KEY_BLOCK
    g0 = pl.program_id(2) * nq
    col = lax.broadcasted_iota(jnp.int32, (nq * r, KEY_BLOCK), 1)
    row = lax.broadcasted_iota(jnp.int32, (nq * r, KEY_BLOCK), 0) & (KEY_BLOCK - 1)
    vis = col < row
    suffix2 = _suffix_pair()

    def kv_blocks(js):
        starts = [pl.multiple_of(j * KEY_BLOCK, KEY_BLOCK) for j in js]
        return ([k_ref[0, pl.ds(s, KEY_BLOCK), :].astype(BF16) for s in starts],
                [v_ref[0, pl.ds(s, KEY_BLOCK), :].astype(BF16) for s in starts])

    def mask_finished(ll, off):
        return jnp.concatenate([jnp.where(g0 + qi - off >= 0, ll[qi * r:(qi + 1) * r], -jnp.inf)
                                for qi in range(nq)], axis=0)

    def any_live(ll, off):
        m = jnp.max(mask_finished(ll, off), axis=0, keepdims=True)
        return (m[0, 0] > LOG_DEAD).astype(jnp.int32)

    qss = [_stack_heads(q_ref[0, qi * KEY_BLOCK:(qi + 1) * KEY_BLOCK, :]) for qi in range(nq)]
    kbs, vbs = kv_blocks([g0 + qi for qi in range(nq)])
    acc, ll = _sb_blocks(qss, kbs, vbs, jnp.zeros((nq * r, LANES), F32), jnp.zeros((nq * r, 1), F32), vis,
                         suffix2)

    def cond(c):
        return c[3] > 0

    def body(c):
        off, acc_, ll_, _ = c
        kbs_, vbs_ = kv_blocks([jnp.maximum(g0 + qi - off, 0) for qi in range(nq)])
        acc_, ll_ = _sb_blocks(qss, kbs_, vbs_, acc_, mask_finished(ll_, off), None, suffix2)
        return off + 1, acc_, ll_, any_live(ll_, off + 1)

    one = jnp.int32(1)
    _, acc, _, _ = lax.while_loop(cond, body, (one, acc, ll, any_live(ll, one)))
    for qi in range(nq):
        o_ref[0, qi * KEY_BLOCK:(qi + 1) * KEY_BLOCK, :] = _unstack_heads(acc[qi * r:(qi + 1) * r]).astype(BF16)


def _attn_prompt(q, k, v, tq):
    bsz, t, d = q.shape
    pairs = d // LANES
    return pl.pallas_call(
        _attn_prompt_body,
        grid=(bsz, pairs, t // tq),
        in_specs=[
            pl.BlockSpec((1, tq, LANES), lambda b, p, i: (b, i, p)),
            pl.BlockSpec((1, t, LANES), lambda b, p, i: (b, 0, p)),
            pl.BlockSpec((1, t, LANES), lambda b, p, i: (b, 0, p)),
        ],
        out_specs=pl.BlockSpec((1, tq, LANES), lambda b, p, i: (b, i, p)),
        out_shape=jax.ShapeDtypeStruct((bsz, t, d), BF16),
        compiler_params=_params(("arbitrary", "arbitrary", "arbitrary")),
        name="attn_prompt",
    )(q, k, v)


def _attn_sample_body(q_ref, kn_ref, vn_ref, ck_ref, cv_ref, o_ref):
    tq = q_ref.shape[1]
    past_blocks = ck_ref.shape[1] // KEY_BLOCK
    qs = _stack_heads(q_ref[0])
    row = _row_in_head(2 * tq, KEY_BLOCK)
    col = lax.broadcasted_iota(jnp.int32, (2 * tq, KEY_BLOCK), 1)
    acc = jnp.zeros((2 * tq, LANES), F32)
    ll = jnp.zeros((2 * tq, 1), F32)
    suffix2 = _suffix_pair()
    acc, ll = _sb_block(qs, kn_ref[0].astype(BF16), vn_ref[0].astype(BF16), acc, ll, col < row, suffix2)
    acc = _sb_older_blocks(qs, ck_ref, cv_ref, past_blocks - 1, acc, ll, suffix2)
    o_ref[0] = _unstack_heads(acc).astype(BF16)


def _attn_sample(q, k_new, v_new, cache_k, cache_v):
    bsz, t, d = q.shape
    past = cache_k.shape[1]
    pairs = d // LANES
    return pl.pallas_call(
        _attn_sample_body,
        grid=(bsz, pairs),
        in_specs=[
            pl.BlockSpec((1, t, LANES), lambda b, p: (b, 0, p)),
            pl.BlockSpec((1, KEY_BLOCK, LANES), lambda b, p: (b, 0, p)),
            pl.BlockSpec((1, KEY_BLOCK, LANES), lambda b, p: (b, 0, p)),
            pl.BlockSpec((1, past, LANES), lambda b, p: (b, 0, p)),
            pl.BlockSpec((1, past, LANES), lambda b, p: (b, 0, p)),
        ],
        out_specs=pl.BlockSpec((1, t, LANES), lambda b, p: (b, 0, p)),
        out_shape=jax.ShapeDtypeStruct((bsz, t, d), BF16),
        compiler_params=_params(("arbitrary", "arbitrary")),
        name="attn_sample",
    )(q, k_new, v_new, cache_k, cache_v)


def _first_argmax(cur, iota_f, n):
    m = jnp.max(cur, axis=0, keepdims=True)
    i = jnp.min(jnp.where(cur == m, iota_f, float(n)), axis=0, keepdims=True)
    return m, i


def _store_row_tiles(ref, val):
    for s in range(val.shape[1] // LANES):
        ref[:, s, :] = val[:, s * LANES:(s + 1) * LANES]


def _post_attn_body(o_ref, gc_ref, sa_ref, x_ref, cnt_ref, lng_ref, lnb_ref, wao_ref, wo_ref, l1g_ref,
                    l1b_ref, wrh_ref, wrl_ref, rb_ref, wsgu_ref, wsd_ref,
                    x1_ref, part_ref, idx_ref, wts_ref, rank_ref, cnt_out_ref, run):
    tm = x_ref.shape[0]
    ff = wsd_ref.shape[0]

    @pl.when(pl.program_id(0) == 0)
    def _():
        run[...] = cnt_ref[...]

    h = _layer_norm(x_ref[...], lng_ref[...], lnb_ref[...])
    att = _dot(o_ref[...], wao_ref[...])
    merged = gc_ref[...].astype(F32) + sa_ref[...].astype(F32) * att
    mixed = _dot(merged.astype(BF16), wo_ref[...])
    x1 = _layer_norm(DEEPNORM_ALPHA * h + mixed, l1g_ref[...], l1b_ref[...])
    _store_row_tiles(x1_ref, x1)

    x1h = x1.astype(BF16)
    gu = _dot(x1h, wsgu_ref[...])
    shared = _dot((_silu(gu[:, :ff]) * gu[:, ff:]).astype(BF16), wsd_ref[...])
    _store_row_tiles(part_ref, DEEPNORM_ALPHA * x1 + shared)

    x1l = (x1 - x1h.astype(F32)).astype(BF16)
    logits = _dot_nt(wrh_ref[...], x1h) + _dot_nt(wrl_ref[...], x1h) + _dot_nt(wrh_ref[...], x1l)
    scores = _sigmoid(logits)
    choice = scores + rb_ref[...]
    neg = -jnp.inf

    giota = lax.broadcasted_iota(jnp.int32, (GROUP_SIZE, tm), 0).astype(F32)
    gs = []
    for g in range(N_GROUPS):
        blk = choice[g * GROUP_SIZE:(g + 1) * GROUP_SIZE, :]
        m1, i1 = _first_argmax(blk, giota, GROUP_SIZE)
        m2 = jnp.max(jnp.where(giota == i1, neg, blk), axis=0, keepdims=True)
        gs.append(m1 + m2)
    gscore = jnp.concatenate(gs, axis=0)

    g8 = lax.broadcasted_iota(jnp.int32, (N_GROUPS, tm), 0).astype(F32)
    gsel = jnp.zeros((N_GROUPS, tm), F32)
    cur = gscore
    for _ in range(TOPK_GROUPS):
        _, i = _first_argmax(cur, g8, N_GROUPS)
        hit = g8 == i
        gsel = jnp.where(hit, 1.0, gsel)
        cur = jnp.where(hit, neg, cur)
    emask = jnp.concatenate(
        [jnp.broadcast_to(gsel[g:g + 1, :], (GROUP_SIZE, tm)) for g in range(N_GROUPS)], axis=0)

    eiota = lax.broadcasted_iota(jnp.int32, (N_EXPERTS, tm), 0).astype(F32)
    cur = jnp.where(emask > 0.0, choice, neg)
    hits, sel_w = [], []
    for r in range(TOP_K):
        _, i = _first_argmax(cur, eiota, N_EXPERTS)
        hit = eiota == i
        hits.append(hit)
        sel_w.append(jnp.sum(jnp.where(hit, scores, 0.0), axis=0, keepdims=True))
        cur = jnp.where(hit, neg, cur)
        idx_ref[r:r + 1, :] = i.astype(jnp.int32)
    wsum = sel_w[0]
    for r in range(1, TOP_K):
        wsum = wsum + sel_w[r]
    for r in range(TOP_K):
        wts_ref[r:r + 1, :] = sel_w[r] / wsum * ROUTED_SCALE

    picked = jnp.zeros((N_EXPERTS, tm), F32)
    for r in range(TOP_K):
        picked = jnp.where(hits[r], 1.0, picked)
    picked_b = picked.astype(BF16)
    trow = lax.broadcasted_iota(jnp.int32, (tm, tm), 0)
    tcol = lax.broadcasted_iota(jnp.int32, (tm, tm), 1)
    earlier = jnp.where(trow < tcol, 1.0, 0.0).astype(BF16)
    before = _dot(picked_b, earlier) + run[:, 0:1]
    for r in range(TOP_K):
        rank_ref[r:r + 1, :] = jnp.sum(jnp.where(hits[r], before, 0.0), axis=0, keepdims=True).astype(jnp.int32)
    run[...] = run[...] + _dot(picked_b, jnp.ones((tm, LANES), BF16))
    cnt_out_ref[...] = run[...]


def _post_attn(o, gc, sa, x, cnt_in, ln_g, ln_b, w_ao, w_o, l1g, l1b, wr_hi, wr_lo, r_bias, ws_gu, ws_d, tm):
    n, d = x.shape
    ff = ws_d.shape[0]
    const = lambda shape: pl.BlockSpec(shape, lambda i: (0,) * len(shape))
    tile = lambda: pl.BlockSpec((tm, d), lambda i: (i, 0))
    small = lambda: pl.BlockSpec((TOP_K, tm), lambda i: (0, i))
    row_tiles = lambda: pl.BlockSpec((tm, d // LANES, LANES), lambda i: (i, 0, 0))
    return pl.pallas_call(
        _post_attn_body,
        grid=(n // tm,),
        in_specs=[tile(), tile(), tile(), tile(), const((N_EXPERTS, LANES)),
                  const((1, d)), const((1, d)), const((d, d)), const((d, d)), const((1, d)), const((1, d)),
                  const((N_EXPERTS, d)), const((N_EXPERTS, d)), const((N_EXPERTS, 1)),
                  const((d, 2 * ff)), const((ff, d))],
        out_specs=[row_tiles(), row_tiles(), small(), small(), small(), const((N_EXPERTS, LANES))],
        out_shape=[
            jax.ShapeDtypeStruct((n, d // LANES, LANES), F32),
            jax.ShapeDtypeStruct((n, d // LANES, LANES), F32),
            jax.ShapeDtypeStruct((TOP_K, n), jnp.int32),
            jax.ShapeDtypeStruct((TOP_K, n), F32),
            jax.ShapeDtypeStruct((TOP_K, n), jnp.int32),
            jax.ShapeDtypeStruct((N_EXPERTS, LANES), F32),
        ],
        scratch_shapes=[pltpu.VMEM((N_EXPERTS, LANES), F32)],
        compiler_params=_params(("arbitrary",)),
        name="post_attn",
    )(o, gc, sa, x, cnt_in, ln_g, ln_b, w_ao, w_o, l1g, l1b, wr_hi, wr_lo, r_bias, ws_gu, ws_d)


def _dest_body(idx_ref, rank_ref, rs_ref, dest_ref):
    tm = idx_ref.shape[1]
    eiota = lax.broadcasted_iota(jnp.int32, (N_EXPERTS, tm), 0)
    rs = rs_ref[...]
    for r in range(TOP_K):
        base = jnp.sum(jnp.where(eiota == idx_ref[r:r + 1, :], rs, 0.0), axis=0, keepdims=True)
        dest_ref[r:r + 1, :] = base.astype(jnp.int32) + rank_ref[r:r + 1, :]


def _dest(idx, rank, row_start_f, tm):
    n = idx.shape[1]
    small = lambda: pl.BlockSpec((TOP_K, tm), lambda i: (0, i))
    return pl.pallas_call(
        _dest_body,
        grid=(n // tm,),
        in_specs=[small(), small(), pl.BlockSpec((N_EXPERTS, 1), lambda i: (0, 0))],
        out_specs=small(),
        out_shape=jax.ShapeDtypeStruct((TOP_K, n), jnp.int32),
        compiler_params=_params(("arbitrary",)),
        name="dest",
    )(idx, rank, row_start_f)


def _slot_index_copies(src_hbm, col0, width, dst_smem, s, sem):
    w = dst_smem.shape[1] // TOP_K
    return [pltpu.make_async_copy(src_hbm.at[pl.ds(j, 1), pl.ds(col0 + q * LANES, LANES)],
                                  dst_smem.at[s, pl.ds(j * w + q, 1), :], sem)
            for j in range(TOP_K) for q in range(width // LANES)]


def _dispatch_body(meta_ref, lc_ref, dp_hbm, ds_hbm, xp_ref, xsm_ref, xs_hbm, idx_smem, zbuf, isem, ssem, zsem):
    i = pl.program_id(0)
    n = pl.num_programs(0)
    tm = xp_ref.shape[0]
    n_s = xsm_ref.shape[0]
    total_chunks = xs_hbm.shape[0] // GATHER_CHUNK

    def idx_copies(tile, fn):
        @pl.when(tile < n - 1)
        def _():
            for c in _slot_index_copies(dp_hbm, tile * tm, tm, idx_smem, 0, isem):
                fn(c)

        @pl.when(tile == n - 1)
        def _():
            for c in _slot_index_copies(ds_hbm, 0, n_s, idx_smem, 0, isem):
                fn(c)

    @pl.when(i == 0)
    def _():
        idx_copies(0, lambda c: c.start())

    idx_copies(i, lambda c: c.wait())

    def scatter(src_ref):
        cnt = src_ref.shape[0]
        for j in range(TOP_K):
            for q in range(cnt // LANES):
                idx_row = idx_smem.at[0, j * (tm // LANES) + q]

                def body(p, carry, idx_row=idx_row, q=q):
                    for u in range(DMA_THREADS):
                        t = DMA_THREADS * p + u
                        row = idx_row[t]
                        pltpu.make_async_copy(src_ref.at[q * LANES + t], xs_hbm.at[row], ssem).start(priority=u)
                    return carry
                lax.fori_loop(0, LANES // DMA_THREADS, body, 0, unroll=4)

        @pl.when(i + 1 < n)
        def _():
            idx_copies(i + 1, lambda c: c.start())

        for j in range(TOP_K):
            pltpu.make_async_copy(src_ref, xs_hbm.at[pl.ds(0, cnt)], ssem).wait()

    @pl.when(i == 0)
    def _():
        zbuf[...] = jnp.zeros(zbuf.shape, F32)

        def zero_chunk(c):
            return pltpu.make_async_copy(zbuf, xs_hbm.at[pl.ds(c * GATHER_CHUNK, GATHER_CHUNK)], zsem)

        def each_expert(fn):
            def body(e, carry):
                @pl.when(lc_ref[e] >= 0)
                def _():
                    fn(lc_ref[e])
                return carry
            lax.fori_loop(0, N_EXPERTS, body, 0)

        def each_tail(fn):
            def body(c, carry):
                fn(c)
                return carry
            lax.fori_loop(meta_ref[0], total_chunks, body, 0)

        each_expert(lambda c: zero_chunk(c).start())
        each_tail(lambda c: zero_chunk(c).start())
        each_expert(lambda c: zero_chunk(c).wait())
        each_tail(lambda c: zero_chunk(c).wait())

    @pl.when(i < n - 1)
    def _():
        scatter(xp_ref)

    @pl.when(i == n - 1)
    def _():
        scatter(xsm_ref)


def _dispatch(meta, last_chunk, dest_p, dest_s, x1p, x1s, n_rows, tm):
    n_p, sub, lanes = x1p.shape
    n_s = x1s.shape[0]
    n_pt = n_p // tm
    grid_spec = pltpu.PrefetchScalarGridSpec(
        num_scalar_prefetch=2,
        grid=(n_pt + 1,),
        in_specs=[
            pl.BlockSpec(memory_space=pl.ANY),
            pl.BlockSpec(memory_space=pl.ANY),
            pl.BlockSpec((tm, sub, lanes), lambda i, *_: (jnp.minimum(i, n_pt - 1), 0, 0)),
            pl.BlockSpec((n_s, sub, lanes), lambda i, *_: (0, 0, 0)),
        ],
        out_specs=pl.BlockSpec(memory_space=pl.ANY),
        scratch_shapes=[
            pltpu.SMEM((1, TOP_K * tm // LANES, LANES), jnp.int32),
            pltpu.VMEM((GATHER_CHUNK, sub, lanes), F32),
            pltpu.SemaphoreType.DMA(()),
            pltpu.SemaphoreType.DMA(()),
            pltpu.SemaphoreType.DMA(()),
        ],
    )
    return pl.pallas_call(
        _dispatch_body,
        grid_spec=grid_spec,
        out_shape=jax.ShapeDtypeStruct((n_rows, sub, lanes), F32),
        compiler_params=_dma_params(("arbitrary",)),
        name="dispatch",
    )(meta, last_chunk, dest_p, dest_s, x1p, x1s)


def _experts_body(be_ref, bx_ref, bn_ref, bz_ref, xs_hbm, wg_ref, wu_ref, wd_ref, ys_hbm,
                  xbuf, obuf, gsem, osem, wgu_bf, wd_bf):
    b = pl.program_id(0)
    nb = pl.num_programs(0)
    slot = b % 2
    ff = wd_ref.shape[1]

    def in_copies(blk, s, nch):
        rows = pl.ds(bx_ref[blk] * GATHER_CHUNK, nch * GATHER_CHUNK)
        return [pltpu.make_async_copy(xs_hbm.at[rows, c, :],
                                      xbuf.at[s, pl.ds(0, nch * GATHER_CHUNK), pl.ds(c * LANES, LANES)],
                                      gsem.at[s]) for c in range(xs_hbm.shape[1])]

    def out_copies(blk, s, nch):
        rows = pl.ds(bx_ref[blk] * GATHER_CHUNK, nch * GATHER_CHUNK)
        return [pltpu.make_async_copy(obuf.at[s, pl.ds(0, nch * GATHER_CHUNK), pl.ds(c * LANES, LANES)],
                                      ys_hbm.at[rows, c, :], osem.at[s]) for c in range(ys_hbm.shape[1])]

    def for_block(count, copies, fn):
        for nch in range(1, BLOCK_CHUNKS + 1):
            @pl.when(count == nch)
            def _():
                for c in copies(nch):
                    fn(c)

    def n_in(blk):
        return jnp.where(bz_ref[blk] == 0, bn_ref[blk], 0)

    @pl.when(b == 0)
    def _():
        for_block(n_in(0), lambda k: in_copies(0, 0, k), lambda c: c.start())

    @pl.when(b + 1 < nb)
    def _():
        for_block(n_in(b + 1), lambda k: in_copies(b + 1, 1 - slot, k), lambda c: c.start())

    @pl.when(b >= 2)
    def _():
        for_block(bn_ref[b - 2], lambda k: out_copies(b - 2, slot, k), lambda c: c.wait())

    for_block(n_in(b), lambda k: in_copies(b, slot, k), lambda c: c.wait())

    e_now = be_ref[b]
    e_prev = be_ref[jnp.maximum(b - 1, 0)]

    @pl.when(jnp.logical_or(b == 0, e_now != e_prev))
    def _():
        wgu_bf[:, 0:ff] = wg_ref[0].astype(BF16)
        wgu_bf[:, ff:2 * ff] = wu_ref[0].astype(BF16)
        wd_bf[...] = wd_ref[0].astype(BF16)

    for nch in range(1, BLOCK_CHUNKS + 1):
        @pl.when(n_in(b) == nch)
        def _():
            rows = nch * GATHER_CHUNK
            gu = _dot(xbuf[slot, 0:rows, :].astype(BF16), wgu_bf[...])
            hid = _silu(gu[:, :ff]) * gu[:, ff:]
            obuf[slot, 0:rows, :] = _dot(hid.astype(BF16), wd_bf[...])

    @pl.when(bz_ref[b] != 0)
    def _():
        obuf[slot] = jnp.zeros(obuf.shape[1:], F32)

    for_block(bn_ref[b], lambda k: out_copies(b, slot, k), lambda c: c.start())

    @pl.when(b == nb - 1)
    def _():
        for_block(bn_ref[b], lambda k: out_copies(b, slot, k), lambda c: c.wait())

        @pl.when(nb > 1)
        def _():
            for_block(bn_ref[b - 1], lambda k: out_copies(b - 1, 1 - slot, k), lambda c: c.wait())


def _experts(block_e, block_x, block_n, block_z, xs, we_gate, we_up, we_down):
    nb = block_e.shape[0]
    n_rows = xs.shape[0]
    d = we_gate.shape[1]
    ff = we_gate.shape[2]
    rows = BLOCK_CHUNKS * GATHER_CHUNK
    grid_spec = pltpu.PrefetchScalarGridSpec(
        num_scalar_prefetch=4,
        grid=(nb,),
        in_specs=[
            pl.BlockSpec(memory_space=pl.ANY),
            pl.BlockSpec((1, d, ff), lambda b, be, *_: (be[b], 0, 0)),
            pl.BlockSpec((1, d, ff), lambda b, be, *_: (be[b], 0, 0)),
            pl.BlockSpec((1, ff, d), lambda b, be, *_: (be[b], 0, 0)),
        ],
        out_specs=pl.BlockSpec(memory_space=pl.ANY),
        scratch_shapes=[
            pltpu.VMEM((2, rows, d), F32),
            pltpu.VMEM((2, rows, d), F32),
            pltpu.SemaphoreType.DMA((2,)),
            pltpu.SemaphoreType.DMA((2,)),
            pltpu.VMEM((d, 2 * ff), BF16),
            pltpu.VMEM((ff, d), BF16),
        ],
    )
    return pl.pallas_call(
        _experts_body,
        grid_spec=grid_spec,
        out_shape=jax.ShapeDtypeStruct(xs.shape, F32),
        compiler_params=_dma_params(("arbitrary",)),
        name="experts",
    )(block_e, block_x, block_n, block_z, xs, we_gate, we_up, we_down)


def _combine_body(dt_hbm, wt_hbm, ys_hbm, part_ref, g_ref, b_ref, y_ref, idx_smem, w_smem, gbuf, isem, wsem, gsem):
    i = pl.program_id(0)
    n = pl.num_programs(0)
    tm = gbuf.shape[2]
    per_step = part_ref.shape[0] // tm
    more = i + 1 < n

    def idx_copies(tile, s):
        return _slot_index_copies(dt_hbm, tile * tm, tm, idx_smem, s, isem.at[s])

    def w_copies(tile, s):
        return _slot_index_copies(wt_hbm, tile * tm, tm, w_smem, s, wsem.at[s])

    def start(copies):
        for c in copies:
            c.start()

    def wait(copies):
        for c in copies:
            c.wait()

    def issue_rows(s):
        for j in range(TOP_K):
            def body(p, carry):
                for u in range(DMA_THREADS):
                    t = DMA_THREADS * p + u
                    row = idx_smem.at[s, j][t]
                    pltpu.make_async_copy(ys_hbm.at[row], gbuf.at[s, j, t], gsem.at[s]).start(priority=u)
                return carry
            lax.fori_loop(0, tm // DMA_THREADS, body, 0, unroll=4)

    def finish_tile(s):
        for j in range(TOP_K):
            pltpu.make_async_copy(ys_hbm.at[pl.ds(0, tm)], gbuf.at[s, j], gsem.at[s]).wait()
        wait(w_copies(0, s))
        base = s * tm

        def token(t, carry):
            f = part_ref[base + t]
            for j in range(TOP_K):
                f = f + w_smem.at[s, j][t] * gbuf[s, j, t]
            y_ref[base + t] = f
            return carry
        lax.fori_loop(0, tm, token, 0, unroll=8)

        f = y_ref[base:base + tm]
        inv_d = 1.0 / (f.shape[1] * f.shape[2])
        tile_sum = lambda a: jnp.sum(jnp.sum(a, axis=2, keepdims=True), axis=1, keepdims=True)
        fc = f - tile_sum(f) * inv_d
        var = tile_sum(fc * fc) * inv_d
        y_ref[base:base + tm] = fc * lax.rsqrt(var + LN_EPS) * g_ref[...] + b_ref[...]

    first = i * per_step

    @pl.when(i == 0)
    def _():
        start(idx_copies(0, 0))
        start(w_copies(0, 0))
        wait(idx_copies(0, 0))
        issue_rows(0)
        if per_step == 2:
            start(idx_copies(1, 1))
            start(w_copies(1, 1))

    if per_step == 1:
        finish_tile(0)

        @pl.when(more)
        def _():
            start(idx_copies(first + 1, 0))
            start(w_copies(first + 1, 0))
            wait(idx_copies(first + 1, 0))
            issue_rows(0)
    else:
        wait(idx_copies(first + 1, 1))
        issue_rows(1)

        @pl.when(more)
        def _():
            start(idx_copies(first + 2, 0))

        finish_tile(0)

        @pl.when(more)
        def _():
            start(w_copies(first + 2, 0))
            wait(idx_copies(first + 2, 0))
            issue_rows(0)
            start(idx_copies(first + 3, 1))

        finish_tile(1)

        @pl.when(more)
        def _():
            start(w_copies(first + 3, 1))


def _combine(dest, wts, ys, part, ln_g, ln_b, tm, per_step):
    n, sub, lanes = part.shape
    assert tm == LANES
    rows = tm * per_step
    return pl.pallas_call(
        _combine_body,
        grid=(n // rows,),
        in_specs=[
            pl.BlockSpec(memory_space=pl.ANY),
            pl.BlockSpec(memory_space=pl.ANY),
            pl.BlockSpec(memory_space=pl.ANY),
            pl.BlockSpec((rows, sub, lanes), lambda i: (i, 0, 0)),
            pl.BlockSpec((sub, lanes), lambda i: (0, 0)),
            pl.BlockSpec((sub, lanes), lambda i: (0, 0)),
        ],
        out_specs=pl.BlockSpec((rows, sub, lanes), lambda i: (i, 0, 0)),
        out_shape=jax.ShapeDtypeStruct((n, sub, lanes), F32),
        scratch_shapes=[
            pltpu.SMEM((2, TOP_K, LANES), jnp.int32),
            pltpu.SMEM((2, TOP_K, LANES), F32),
            pltpu.VMEM((2, TOP_K, tm, sub, lanes), F32),
            pltpu.SemaphoreType.DMA((2,)),
            pltpu.SemaphoreType.DMA((2,)),
            pltpu.SemaphoreType.DMA((2,)),
        ],
        compiler_params=_dma_params(("arbitrary",)),
        name="combine",
    )(dest, wts, ys, part, ln_g.reshape(sub, lanes), ln_b.reshape(sub, lanes))


def kernel(x_prompt, x_sample, cache_k, cache_v, state_conv, ln_in_g, ln_in_b, w_in, b_in, conv_w, conv_b,
           conv_ln_g, conv_ln_b, w_conv_out, w_attn_out, w_out, ln1_g, ln1_b, w_router, router_bias,
           we_gate, we_up, we_down, ws_gate, ws_up, ws_down, ln2_g, ln2_b):
    bp, tp, d = x_prompt.shape
    bs, ts, _ = x_sample.shape
    past = cache_k.shape[2]
    assert w_in.shape[0] == DEPTH and d == N_HEADS * HEAD_DIM and conv_w.shape[2] == d
    assert tp % 1024 == 0 and ts <= KEY_BLOCK and ts % 16 == 0 and past % KEY_BLOCK == 0
    n_p, n_s = bp * tp, bs * ts
    assert n_s % LANES == 0

    row = lambda a: a.reshape(1, -1)
    lng, lnb = row(ln_in_g), row(ln_in_b)
    w_in_b = w_in[0].astype(BF16)
    w_co_b = w_conv_out[0].astype(BF16)
    conv_args = (lng, lnb, w_in_b, b_in, conv_w[0], conv_b, conv_ln_g, conv_ln_b, w_co_b)

    qp, kp, vp, gcp, sap, csp = _proj_conv(x_prompt, jnp.zeros((bp, CONV_WIDTH - 1, d), F32), *conv_args, tm=256)
    op = _attn_prompt(qp, kp, vp, tq=512)

    qs, ks, vs, gcs, sas, css = _proj_conv(x_sample, state_conv[0], *conv_args, tm=ts)
    pad_new = lambda a: jnp.pad(a, ((0, 0), (0, KEY_BLOCK - ts), (0, 0)))
    os_ = _attn_sample(qs, pad_new(ks), pad_new(vs), cache_k[0].reshape(bs, past, d), cache_v[0].reshape(bs, past, d))

    wr = w_router[0].T
    wr_hi = wr.astype(BF16)
    wr_lo = (wr - wr_hi.astype(F32)).astype(BF16)
    post_args = (lng, lnb, w_attn_out[0].astype(BF16), w_out[0].astype(BF16), ln1_g, ln1_b, wr_hi, wr_lo,
                 router_bias.reshape(N_EXPERTS, 1),
                 jnp.concatenate([ws_gate[0], ws_up[0]], axis=1).astype(BF16), ws_down[0].astype(BF16))
    flat = lambda a, n: a.reshape(n, d)
    zero_cnt = jnp.zeros((N_EXPERTS, LANES), F32)
    x1p, partp, idxp, wtp, rankp, cnt_p = _post_attn(
        flat(op, n_p), flat(gcp, n_p), flat(sap, n_p), flat(x_prompt, n_p), zero_cnt, *post_args, tm=256)
    x1s, parts, idxs, wts, ranks, cnt = _post_attn(
        flat(os_, n_s), flat(gcs, n_s), flat(sas, n_s), flat(x_sample, n_s), cnt_p, *post_args, tm=n_s)

    i32 = jnp.int32
    n_tok = n_p + n_s
    total_chunks = (n_tok * TOP_K + N_EXPERTS * (GATHER_CHUNK - 1) + GATHER_CHUNK - 1) // GATHER_CHUNK
    n_blocks = total_chunks // BLOCK_CHUNKS + N_EXPERTS
    counts = cnt[:, 0].astype(i32)
    chunks = (counts + GATHER_CHUNK - 1) // GATHER_CHUNK
    chunk_end = jnp.cumsum(chunks)
    chunk_start = chunk_end - chunks
    row_start = chunk_start * GATHER_CHUNK
    used_chunks = chunk_end[-1]
    blocks = (chunks + BLOCK_CHUNKS - 1) // BLOCK_CHUNKS
    blk_end = jnp.cumsum(blocks)
    blk_start = blk_end - blocks
    bid = jnp.arange(n_blocks, dtype=i32)
    block_e = jnp.minimum(jnp.sum(bid[:, None] >= blk_end[None, :], axis=1), N_EXPERTS - 1).astype(i32)
    k_in_e = bid - blk_start[block_e]
    is_tail = bid >= blk_end[-1]
    tail_x = used_chunks + BLOCK_CHUNKS * (bid - blk_end[-1])
    block_x = jnp.where(is_tail, tail_x, chunk_start[block_e] + BLOCK_CHUNKS * k_in_e)
    block_n = jnp.where(is_tail, total_chunks - tail_x, chunks[block_e] - BLOCK_CHUNKS * k_in_e)
    block_n = jnp.clip(block_n, 0, BLOCK_CHUNKS).astype(i32)
    block_x = jnp.where(block_n > 0, block_x, 0).astype(i32)
    block_z = is_tail.astype(i32)
    meta = jnp.stack([used_chunks, used_chunks]).astype(i32)
    last_chunk = jnp.where(chunks > 0, chunk_end - 1, -1).astype(i32)

    rs_f = row_start.astype(F32).reshape(N_EXPERTS, 1)
    dest_p = _dest(idxp, rankp, rs_f, tm=1024)
    dest_s = _dest(idxs, ranks, rs_f, tm=n_s)

    xs = _dispatch(meta, last_chunk, dest_p, dest_s, x1p, x1s, total_chunks * GATHER_CHUNK, tm=512)
    ys = _experts(block_e, block_x, block_n, block_z, xs, we_gate[0], we_up[0], we_down[0])

    yp = _combine(dest_p, wtp, ys, partp, ln2_g, ln2_b, tm=128, per_step=2)
    ysm = _combine(dest_s, wts, ys, parts, ln2_g, ln2_b, tm=n_s, per_step=1)

    heads = lambda a, b, t: a.reshape(1, b, t, N_HEADS, HEAD_DIM)
    return (yp.reshape(bp, tp, d), ysm.reshape(bs, ts, d),
            heads(kp, bp, tp), heads(vp, bp, tp), csp[None],
            heads(ks, bs, ts), heads(vs, bs, ts), css[None])
```

```python
import functools

import jax
import jax.numpy as jnp
from jax import lax
from jax.experimental import pallas as pl
from jax.experimental.pallas import tpu as pltpu

F32 = jnp.float32
BF16 = jnp.bfloat16

N_HEADS = 16
HEAD_DIM = 64
CONV_WIDTH = 31
N_EXPERTS = 256
TOP_K = 8
N_GROUPS = 8
TOPK_GROUPS = 4
GROUP_SIZE = N_EXPERTS // N_GROUPS
ROUTED_SCALE = 2.5
LN_EPS = 1e-5
DEPTH = 1
DEEPNORM_ALPHA = (2 * DEPTH) ** 0.25
ATTN_SCALE = HEAD_DIM ** -0.5

LANES = 128
SUBLANES = 8
VMEM_LIMIT_BYTES = 56 * 1024 * 1024

KEY_BLOCK = 128
HIST_PAD = 32
GATHER_CHUNK = 128
BLOCK_CHUNKS = 4
DMA_THREADS = 2
LOG_DEAD = -110.0


def _layer_norm(x, g, b):
    mu = jnp.mean(x, axis=-1, keepdims=True)
    xc = x - mu
    var = jnp.mean(xc * xc, axis=-1, keepdims=True)
    return xc * lax.rsqrt(var + LN_EPS) * g + b


def _sigmoid(x):
    return 1.0 / (1.0 + jnp.exp(-x))


def _silu(x):
    return x * _sigmoid(x)


def _dot(a, b):
    return jnp.dot(a, b, preferred_element_type=F32)


def _dot_nt(a, b):
    return lax.dot_general(a, b, (((1,), (1,)), ((), ())), preferred_element_type=F32)


def _params(sem):
    return pltpu.CompilerParams(dimension_semantics=sem, vmem_limit_bytes=VMEM_LIMIT_BYTES)


def _dma_params(sem):
    return pltpu.CompilerParams(dimension_semantics=sem, vmem_limit_bytes=VMEM_LIMIT_BYTES,
                                disable_bounds_checks=True)


def _proj_conv_body(x_ref, hist_ref, lng_ref, lnb_ref, win_ref, bin_ref, cw_ref, cb_ref, clg_ref,
                    clb_ref, wco_ref, q_ref, k_ref, v_ref, gc_ref, sa_ref, cs_ref, ubuf, cbuf, ush):
    tm = x_ref.shape[1]
    d = x_ref.shape[2]
    hist = CONV_WIDTH - 1
    lead = HIST_PAD - hist

    @pl.when(pl.program_id(1) == 0)
    def _():
        ubuf[0:lead, :] = jnp.zeros((lead, d), F32)
        ubuf[lead:HIST_PAD, :] = hist_ref[0]

    hb = _layer_norm(x_ref[0], lng_ref[...], lnb_ref[...]).astype(BF16)

    def proj(i):
        return _dot(hb, win_ref[:, i * d:(i + 1) * d]) + bin_ref[:, i * d:(i + 1) * d]

    ubuf[HIST_PAD:HIST_PAD + tm, :] = proj(0) * _sigmoid(proj(1))
    q_ref[0] = (proj(2) * ATTN_SCALE).astype(BF16)
    k_ref[0] = proj(3)
    v_ref[0] = proj(4)
    sa_ref[0] = _sigmoid(proj(6)).astype(BF16)

    span = tm + HIST_PAD - SUBLANES
    for ph in range(1, SUBLANES):
        ush[ph - 1, 0:span, :] = ubuf[ph:ph + span, :]
    rows = min(tm, 32)
    for r0 in range(0, tm, rows):
        acc = jnp.broadcast_to(cb_ref[...], (rows, d))
        for kk in range(CONV_WIDTH):
            base, ph = divmod(lead + kk, SUBLANES)
            lo = base * SUBLANES + r0
            tap = ubuf[lo:lo + rows, :] if ph == 0 else ush[ph - 1, lo:lo + rows, :]
            acc = acc + cw_ref[kk:kk + 1, :] * tap
        cbuf[r0:r0 + rows, :] = _silu(_layer_norm(acc, clg_ref[...], clb_ref[...])).astype(BF16)
    gc_ref[0] = (_sigmoid(proj(5)) * _dot(cbuf[...], wco_ref[...])).astype(BF16)

    cs_ref[0] = ubuf[tm + lead:tm + HIST_PAD, :]
    ubuf[0:HIST_PAD, :] = ubuf[tm:tm + HIST_PAD, :]


def _proj_conv(x, hist, ln_g, ln_b, w_in, b_in, conv_w, conv_b, cln_g, cln_b, w_co, tm):
    bsz, t, d = x.shape
    pw = w_in.shape[1]
    const = lambda shape: pl.BlockSpec(shape, lambda b, i: (0,) * len(shape))
    resident = lambda shape: pl.BlockSpec(shape, lambda b, i: (0,) * len(shape), pipeline_mode=pl.Buffered(1))
    tile = lambda: pl.BlockSpec((1, tm, d), lambda b, i: (b, i, 0))
    return pl.pallas_call(
        _proj_conv_body,
        grid=(bsz, t // tm),
        in_specs=[
            tile(),
            pl.BlockSpec((1, CONV_WIDTH - 1, d), lambda b, i: (b, 0, 0)),
            const((1, d)), const((1, d)), resident((d, pw)), const((1, pw)),
            const((CONV_WIDTH, d)), const((1, d)), const((1, d)), const((1, d)), resident((d, d)),
        ],
        out_specs=[tile(), tile(), tile(), tile(), tile(),
                   pl.BlockSpec((1, CONV_WIDTH - 1, d), lambda b, i: (b, 0, 0))],
        out_shape=[
            jax.ShapeDtypeStruct((bsz, t, d), BF16),
            jax.ShapeDtypeStruct((bsz, t, d), F32),
            jax.ShapeDtypeStruct((bsz, t, d), F32),
            jax.ShapeDtypeStruct((bsz, t, d), BF16),
            jax.ShapeDtypeStruct((bsz, t, d), BF16),
            jax.ShapeDtypeStruct((bsz, CONV_WIDTH - 1, d), F32),
        ],
        scratch_shapes=[pltpu.VMEM((HIST_PAD + tm, d), F32), pltpu.VMEM((tm, d), BF16),
                        pltpu.VMEM((SUBLANES - 1, tm + HIST_PAD - SUBLANES, d), F32)],
        compiler_params=_params(("arbitrary", "arbitrary")),
        name="proj_conv",
    )(x, hist, ln_g, ln_b, w_in, b_in, conv_w, conv_b, cln_g, cln_b, w_co)


def _suffix_matrix():
    row = lax.broadcasted_iota(jnp.int32, (KEY_BLOCK, KEY_BLOCK), 0)
    col = lax.broadcasted_iota(jnp.int32, (KEY_BLOCK, KEY_BLOCK), 1)
    return jnp.where(row >= col, 1.0, 0.0).astype(BF16)


def _sb_blocks(qss, kbs, vbs, acc, ll, vis, suffix):
    r = qss[0].shape[0]
    z = jnp.concatenate([_dot_nt(q, kb) for q, kb in zip(qss, kbs)], axis=0)
    lk = -(jnp.maximum(z, 0.0) + jnp.log(1.0 + jnp.exp(-jnp.abs(z))))
    if vis is not None:
        lk = jnp.where(vis, lk, 0.0)
    inb = _dot(lk.astype(BF16), suffix)
    arg = z + inb + ll
    if vis is not None:
        arg = jnp.where(vis, arg, -jnp.inf)
    a = jnp.exp(arg).astype(BF16)
    av = jnp.concatenate([_dot(a[i * r:(i + 1) * r], vb) for i, vb in enumerate(vbs)], axis=0)
    return acc + av, ll + inb[:, 0:1]


def _sb_block(qs, kb, vb, acc, ll, vis, suffix):
    return _sb_blocks([qs], [kb], [vb], acc, ll, vis, suffix)


def _sb_older_blocks(qs, k_ref, v_ref, first, acc, ll, suffix):
    def live(ll_):
        return (jnp.max(ll_, axis=0, keepdims=True)[0, 0] > LOG_DEAD).astype(jnp.int32)

    def cond(c):
        return jnp.logical_and(c[0] >= 0, c[3] > 0)

    def body(c):
        j, acc_, ll_, _ = c
        start = pl.multiple_of(j * KEY_BLOCK, KEY_BLOCK)
        kb = k_ref[0, pl.ds(start, KEY_BLOCK), :].astype(BF16)
        vb = v_ref[0, pl.ds(start, KEY_BLOCK), :].astype(BF16)
        acc_, ll_ = _sb_block(qs, kb, vb, acc_, ll_, None, suffix)
        return j - 1, acc_, ll_, live(ll_)

    _, acc, _, _ = lax.while_loop(cond, body, (first, acc, ll, live(ll)))
    return acc


def _stack_heads(q):
    lane = lax.broadcasted_iota(jnp.int32, q.shape, 1)
    zero = jnp.zeros_like(q)
    return jnp.concatenate([jnp.where(lane < HEAD_DIM, q, zero), jnp.where(lane >= HEAD_DIM, q, zero)], axis=0)


def _row_in_head(rows, cols):
    r = lax.broadcasted_iota(jnp.int32, (rows, cols), 0)
    return jnp.where(r >= rows // 2, r - rows // 2, r)


def _unstack_heads(acc):
    r = acc.shape[0] // 2
    lane = lax.broadcasted_iota(jnp.int32, (r, LANES), 1)
    return jnp.where(lane < HEAD_DIM, acc[:r], acc[r:])


def _attn_prompt_body(q_ref, k_ref, v_ref, o_ref):
    tq = q_ref.shape[1]
    nq = tq // KEY_BLOCK
    r = 2 * KEY_BLOCK
    g0 = pl.program_id(2) * nq
    col = lax.broadcasted_iota(jnp.int32, (nq * r, KEY_BLOCK), 1)
    row = lax.broadcasted_iota(jnp.int32, (nq * r, KEY_BLOCK), 0) & (KEY_BLOCK - 1)
    vis = col < row
    suffix = _suffix_matrix()

    def kv_blocks(js):
        starts = [pl.multiple_of(j * KEY_BLOCK, KEY_BLOCK) for j in js]
        return ([k_ref[0, pl.ds(s, KEY_BLOCK), :].astype(BF16) for s in starts],
                [v_ref[0, pl.ds(s, KEY_BLOCK), :].astype(BF16) for s in starts])

    def mask_finished(ll, off):
        return jnp.concatenate([jnp.where(g0 + qi - off >= 0, ll[qi * r:(qi + 1) * r], -jnp.inf)
                                for qi in range(nq)], axis=0)

    def any_live(ll, off):
        m = jnp.max(mask_finished(ll, off), axis=0, keepdims=True)
        return (m[0, 0] > LOG_DEAD).astype(jnp.int32)

    qss = [_stack_heads(q_ref[0, qi * KEY_BLOCK:(qi + 1) * KEY_BLOCK, :]) for qi in range(nq)]
    kbs, vbs = kv_blocks([g0 + qi for qi in range(nq)])
    acc, ll = _sb_blocks(qss, kbs, vbs, jnp.zeros((nq * r, LANES), F32), jnp.zeros((nq * r, 1), F32), vis,
                         suffix)

    def cond(c):
        return c[3] > 0

    def body(c):
        off, acc_, ll_, _ = c
        kbs_, vbs_ = kv_blocks([jnp.maximum(g0 + qi - off, 0) for qi in range(nq)])
        acc_, ll_ = _sb_blocks(qss, kbs_, vbs_, acc_, mask_finished(ll_, off), None, suffix)
        return off + 1, acc_, ll_, any_live(ll_, off + 1)

    one = jnp.int32(1)
    _, acc, _, _ = lax.while_loop(cond, body, (one, acc, ll, any_live(ll, one)))
    for qi in range(nq):
        o_ref[0, qi * KEY_BLOCK:(qi + 1) * KEY_BLOCK, :] = _unstack_heads(acc[qi * r:(qi + 1) * r]).astype(BF16)


def _attn_prompt(q, k, v, tq):
    bsz, t, d = q.shape
    pairs = d // LANES
    return pl.pallas_call(
        _attn_prompt_body,
        grid=(bsz, pairs, t // tq),
        in_specs=[
            pl.BlockSpec((1, tq, LANES), lambda b, p, i: (b, i, p)),
            pl.BlockSpec((1, t, LANES), lambda b, p, i: (b, 0, p)),
            pl.BlockSpec((1, t, LANES), lambda b, p, i: (b, 0, p)),
        ],
        out_specs=pl.BlockSpec((1, tq, LANES), lambda b, p, i: (b, i, p)),
        out_shape=jax.ShapeDtypeStruct((bsz, t, d), BF16),
        compiler_params=_params(("arbitrary", "arbitrary", "arbitrary")),
        name="attn_prompt",
    )(q, k, v)


def _attn_sample_body(q_ref, kn_ref, vn_ref, ck_ref, cv_ref, o_ref):
    tq = q_ref.shape[1]
    past_blocks = ck_ref.shape[1] // KEY_BLOCK
    qs = _stack_heads(q_ref[0])
    row = _row_in_head(2 * tq, KEY_BLOCK)
    col = lax.broadcasted_iota(jnp.int32, (2 * tq, KEY_BLOCK), 1)
    acc = jnp.zeros((2 * tq, LANES), F32)
    ll = jnp.zeros((2 * tq, 1), F32)
    suffix = _suffix_matrix()
    acc, ll = _sb_block(qs, kn_ref[0].astype(BF16), vn_ref[0].astype(BF16), acc, ll, col < row, suffix)
    acc = _sb_older_blocks(qs, ck_ref, cv_ref, past_blocks - 1, acc, ll, suffix)
    o_ref[0] = _unstack_heads(acc).astype(BF16)


def _attn_sample(q, k_new, v_new, cache_k, cache_v):
    bsz, t, d = q.shape
    past = cache_k.shape[1]
    pairs = d // LANES
    return pl.pallas_call(
        _attn_sample_body,
        grid=(bsz, pairs),
        in_specs=[
            pl.BlockSpec((1, t, LANES), lambda b, p: (b, 0, p)),
            pl.BlockSpec((1, KEY_BLOCK, LANES), lambda b, p: (b, 0, p)),
            pl.BlockSpec((1, KEY_BLOCK, LANES), lambda b, p: (b, 0, p)),
            pl.BlockSpec((1, past, LANES), lambda b, p: (b, 0, p)),
            pl.BlockSpec((1, past, LANES), lambda b, p: (b, 0, p)),
        ],
        out_specs=pl.BlockSpec((1, t, LANES), lambda b, p: (b, 0, p)),
        out_shape=jax.ShapeDtypeStruct((bsz, t, d), BF16),
        compiler_params=_params(("arbitrary", "arbitrary")),
        name="attn_sample",
    )(q, k_new, v_new, cache_k, cache_v)


def _first_argmax(cur, iota_f, n):
    m = jnp.max(cur, axis=0, keepdims=True)
    i = jnp.min(jnp.where(cur == m, iota_f, float(n)), axis=0, keepdims=True)
    return m, i


def _store_row_tiles(ref, val):
    for s in range(val.shape[1] // LANES):
        ref[:, s, :] = val[:, s * LANES:(s + 1) * LANES]


def _post_attn_body(o_ref, gc_ref, sa_ref, x_ref, cnt_ref, lng_ref, lnb_ref, wao_ref, wo_ref, l1g_ref,
                    l1b_ref, wrh_ref, wrl_ref, rb_ref, wsgu_ref, wsd_ref,
                    x1_ref, part_ref, idx_ref, wts_ref, rank_ref, cnt_out_ref, run):
    tm = x_ref.shape[0]
    ff = wsd_ref.shape[0]

    @pl.when(pl.program_id(0) == 0)
    def _():
        run[...] = cnt_ref[...]

    h = _layer_norm(x_ref[...], lng_ref[...], lnb_ref[...])
    att = _dot(o_ref[...], wao_ref[...])
    merged = gc_ref[...].astype(F32) + sa_ref[...].astype(F32) * att
    mixed = _dot(merged.astype(BF16), wo_ref[...])
    x1 = _layer_norm(DEEPNORM_ALPHA * h + mixed, l1g_ref[...], l1b_ref[...])
    _store_row_tiles(x1_ref, x1)

    x1h = x1.astype(BF16)
    gu = _dot(x1h, wsgu_ref[...])
    shared = _dot((_silu(gu[:, :ff]) * gu[:, ff:]).astype(BF16), wsd_ref[...])
    _store_row_tiles(part_ref, DEEPNORM_ALPHA * x1 + shared)

    x1l = (x1 - x1h.astype(F32)).astype(BF16)
    logits = _dot_nt(wrh_ref[...], x1h) + _dot_nt(wrl_ref[...], x1h) + _dot_nt(wrh_ref[...], x1l)
    scores = _sigmoid(logits)
    choice = scores + rb_ref[...]
    neg = -jnp.inf

    giota = lax.broadcasted_iota(jnp.int32, (GROUP_SIZE, tm), 0).astype(F32)
    gs = []
    for g in range(N_GROUPS):
        blk = choice[g * GROUP_SIZE:(g + 1) * GROUP_SIZE, :]
        m1, i1 = _first_argmax(blk, giota, GROUP_SIZE)
        m2 = jnp.max(jnp.where(giota == i1, neg, blk), axis=0, keepdims=True)
        gs.append(m1 + m2)
    gscore = jnp.concatenate(gs, axis=0)

    g8 = lax.broadcasted_iota(jnp.int32, (N_GROUPS, tm), 0).astype(F32)
    gsel = jnp.zeros((N_GROUPS, tm), F32)
    cur = gscore
    for _ in range(TOPK_GROUPS):
        _, i = _first_argmax(cur, g8, N_GROUPS)
        hit = g8 == i
        gsel = jnp.where(hit, 1.0, gsel)
        cur = jnp.where(hit, neg, cur)
    emask = jnp.concatenate(
        [jnp.broadcast_to(gsel[g:g + 1, :], (GROUP_SIZE, tm)) for g in range(N_GROUPS)], axis=0)

    eiota = lax.broadcasted_iota(jnp.int32, (N_EXPERTS, tm), 0).astype(F32)
    cur = jnp.where(emask > 0.0, choice, neg)
    hits, sel_w = [], []
    for r in range(TOP_K):
        _, i = _first_argmax(cur, eiota, N_EXPERTS)
        hit = eiota == i
        hits.append(hit)
        sel_w.append(jnp.sum(jnp.where(hit, scores, 0.0), axis=0, keepdims=True))
        cur = jnp.where(hit, neg, cur)
        idx_ref[r:r + 1, :] = i.astype(jnp.int32)
    wsum = sel_w[0]
    for r in range(1, TOP_K):
        wsum = wsum + sel_w[r]
    for r in range(TOP_K):
        wts_ref[r:r + 1, :] = sel_w[r] / wsum * ROUTED_SCALE

    picked = jnp.zeros((N_EXPERTS, tm), F32)
    for r in range(TOP_K):
        picked = jnp.where(hits[r], 1.0, picked)
    picked_b = picked.astype(BF16)
    trow = lax.broadcasted_iota(jnp.int32, (tm, tm), 0)
    tcol = lax.broadcasted_iota(jnp.int32, (tm, tm), 1)
    earlier = jnp.where(trow < tcol, 1.0, 0.0).astype(BF16)
    before = _dot(picked_b, earlier) + run[:, 0:1]
    for r in range(TOP_K):
        rank_ref[r:r + 1, :] = jnp.sum(jnp.where(hits[r], before, 0.0), axis=0, keepdims=True).astype(jnp.int32)
    run[...] = run[...] + _dot(picked_b, jnp.ones((tm, LANES), BF16))
    cnt_out_ref[...] = run[...]


def _post_attn(o, gc, sa, x, cnt_in, ln_g, ln_b, w_ao, w_o, l1g, l1b, wr_hi, wr_lo, r_bias, ws_gu, ws_d, tm):
    n, d = x.shape
    ff = ws_d.shape[0]
    const = lambda shape: pl.BlockSpec(shape, lambda i: (0,) * len(shape))
    tile = lambda: pl.BlockSpec((tm, d), lambda i: (i, 0))
    small = lambda: pl.BlockSpec((TOP_K, tm), lambda i: (0, i))
    row_tiles = lambda: pl.BlockSpec((tm, d // LANES, LANES), lambda i: (i, 0, 0))
    return pl.pallas_call(
        _post_attn_body,
        grid=(n // tm,),
        in_specs=[tile(), tile(), tile(), tile(), const((N_EXPERTS, LANES)),
                  const((1, d)), const((1, d)), const((d, d)), const((d, d)), const((1, d)), const((1, d)),
                  const((N_EXPERTS, d)), const((N_EXPERTS, d)), const((N_EXPERTS, 1)),
                  const((d, 2 * ff)), const((ff, d))],
        out_specs=[row_tiles(), row_tiles(), small(), small(), small(), const((N_EXPERTS, LANES))],
        out_shape=[
            jax.ShapeDtypeStruct((n, d // LANES, LANES), F32),
            jax.ShapeDtypeStruct((n, d // LANES, LANES), F32),
            jax.ShapeDtypeStruct((TOP_K, n), jnp.int32),
            jax.ShapeDtypeStruct((TOP_K, n), F32),
            jax.ShapeDtypeStruct((TOP_K, n), jnp.int32),
            jax.ShapeDtypeStruct((N_EXPERTS, LANES), F32),
        ],
        scratch_shapes=[pltpu.VMEM((N_EXPERTS, LANES), F32)],
        compiler_params=_params(("arbitrary",)),
        name="post_attn",
    )(o, gc, sa, x, cnt_in, ln_g, ln_b, w_ao, w_o, l1g, l1b, wr_hi, wr_lo, r_bias, ws_gu, ws_d)


def _dest_body(idx_ref, rank_ref, rs_ref, dest_ref):
    tm = idx_ref.shape[1]
    eiota = lax.broadcasted_iota(jnp.int32, (N_EXPERTS, tm), 0)
    rs = rs_ref[...]
    for r in range(TOP_K):
        base = jnp.sum(jnp.where(eiota == idx_ref[r:r + 1, :], rs, 0.0), axis=0, keepdims=True)
        dest_ref[r:r + 1, :] = base.astype(jnp.int32) + rank_ref[r:r + 1, :]


def _dest(idx, rank, row_start_f, tm):
    n = idx.shape[1]
    small = lambda: pl.BlockSpec((TOP_K, tm), lambda i: (0, i))
    return pl.pallas_call(
        _dest_body,
        grid=(n // tm,),
        in_specs=[small(), small(), pl.BlockSpec((N_EXPERTS, 1), lambda i: (0, 0))],
        out_specs=small(),
        out_shape=jax.ShapeDtypeStruct((TOP_K, n), jnp.int32),
        compiler_params=_params(("arbitrary",)),
        name="dest",
    )(idx, rank, row_start_f)


def _slot_index_copies(src_hbm, col0, width, dst_smem, s, sem):
    w = dst_smem.shape[1] // TOP_K
    return [pltpu.make_async_copy(src_hbm.at[pl.ds(j, 1), pl.ds(col0 + q * LANES, LANES)],
                                  dst_smem.at[s, pl.ds(j * w + q, 1), :], sem)
            for j in range(TOP_K) for q in range(width // LANES)]


def _dispatch_body(meta_ref, lc_ref, dp_hbm, ds_hbm, xp_ref, xsm_ref, xs_hbm, idx_smem, zbuf, isem, ssem, zsem):
    i = pl.program_id(0)
    n = pl.num_programs(0)
    tm = xp_ref.shape[0]
    n_s = xsm_ref.shape[0]
    total_chunks = xs_hbm.shape[0] // GATHER_CHUNK

    def idx_copies(tile, fn):
        @pl.when(tile < n - 1)
        def _():
            for c in _slot_index_copies(dp_hbm, tile * tm, tm, idx_smem, 0, isem):
                fn(c)

        @pl.when(tile == n - 1)
        def _():
            for c in _slot_index_copies(ds_hbm, 0, n_s, idx_smem, 0, isem):
                fn(c)

    @pl.when(i == 0)
    def _():
        idx_copies(0, lambda c: c.start())

    idx_copies(i, lambda c: c.wait())

    def scatter(src_ref):
        cnt = src_ref.shape[0]
        for j in range(TOP_K):
            for q in range(cnt // LANES):
                idx_row = idx_smem.at[0, j * (tm // LANES) + q]

                def body(p, carry, idx_row=idx_row, q=q):
                    for u in range(DMA_THREADS):
                        t = DMA_THREADS * p + u
                        row = idx_row[t]
                        pltpu.make_async_copy(src_ref.at[q * LANES + t], xs_hbm.at[row], ssem).start(priority=u)
                    return carry
                lax.fori_loop(0, LANES // DMA_THREADS, body, 0, unroll=4)

        @pl.when(i + 1 < n)
        def _():
            idx_copies(i + 1, lambda c: c.start())

        for j in range(TOP_K):
            pltpu.make_async_copy(src_ref, xs_hbm.at[pl.ds(0, cnt)], ssem).wait()

    @pl.when(i == 0)
    def _():
        zbuf[...] = jnp.zeros(zbuf.shape, F32)

        def zero_chunk(c):
            return pltpu.make_async_copy(zbuf, xs_hbm.at[pl.ds(c * GATHER_CHUNK, GATHER_CHUNK)], zsem)

        def each_expert(fn):
            def body(e, carry):
                @pl.when(lc_ref[e] >= 0)
                def _():
                    fn(lc_ref[e])
                return carry
            lax.fori_loop(0, N_EXPERTS, body, 0)

        def each_tail(fn):
            def body(c, carry):
                fn(c)
                return carry
            lax.fori_loop(meta_ref[0], total_chunks, body, 0)

        each_expert(lambda c: zero_chunk(c).start())
        each_tail(lambda c: zero_chunk(c).start())
        each_expert(lambda c: zero_chunk(c).wait())
        each_tail(lambda c: zero_chunk(c).wait())

    @pl.when(i < n - 1)
    def _():
        scatter(xp_ref)

    @pl.when(i == n - 1)
    def _():
        scatter(xsm_ref)


def _dispatch(meta, last_chunk, dest_p, dest_s, x1p, x1s, n_rows, tm):
    n_p, sub, lanes = x1p.shape
    n_s = x1s.shape[0]
    n_pt = n_p // tm
    grid_spec = pltpu.PrefetchScalarGridSpec(
        num_scalar_prefetch=2,
        grid=(n_pt + 1,),
        in_specs=[
            pl.BlockSpec(memory_space=pl.ANY),
            pl.BlockSpec(memory_space=pl.ANY),
            pl.BlockSpec((tm, sub, lanes), lambda i, *_: (jnp.minimum(i, n_pt - 1), 0, 0)),
            pl.BlockSpec((n_s, sub, lanes), lambda i, *_: (0, 0, 0)),
        ],
        out_specs=pl.BlockSpec(memory_space=pl.ANY),
        scratch_shapes=[
            pltpu.SMEM((1, TOP_K * tm // LANES, LANES), jnp.int32),
            pltpu.VMEM((GATHER_CHUNK, sub, lanes), F32),
            pltpu.SemaphoreType.DMA(()),
            pltpu.SemaphoreType.DMA(()),
            pltpu.SemaphoreType.DMA(()),
        ],
    )
    return pl.pallas_call(
        _dispatch_body,
        grid_spec=grid_spec,
        out_shape=jax.ShapeDtypeStruct((n_rows, sub, lanes), F32),
        compiler_params=_dma_params(("arbitrary",)),
        name="dispatch",
    )(meta, last_chunk, dest_p, dest_s, x1p, x1s)


def _experts_body(be_ref, bx_ref, bn_ref, bz_ref, xs_hbm, wg_ref, wu_ref, wd_ref, ys_hbm,
                  xbuf, obuf, gsem, osem, wgu_bf, wd_bf):
    b = pl.program_id(0)
    nb = pl.num_programs(0)
    slot = b % 2
    ff = wd_ref.shape[1]

    def in_copies(blk, s, nch):
        rows = pl.ds(bx_ref[blk] * GATHER_CHUNK, nch * GATHER_CHUNK)
        return [pltpu.make_async_copy(xs_hbm.at[rows, c, :],
                                      xbuf.at[s, pl.ds(0, nch * GATHER_CHUNK), pl.ds(c * LANES, LANES)],
                                      gsem.at[s]) for c in range(xs_hbm.shape[1])]

    def out_copies(blk, s, nch):
        rows = pl.ds(bx_ref[blk] * GATHER_CHUNK, nch * GATHER_CHUNK)
        return [pltpu.make_async_copy(obuf.at[s, pl.ds(0, nch * GATHER_CHUNK), pl.ds(c * LANES, LANES)],
                                      ys_hbm.at[rows, c, :], osem.at[s]) for c in range(ys_hbm.shape[1])]

    def for_block(count, copies, fn):
        for nch in range(1, BLOCK_CHUNKS + 1):
            @pl.when(count == nch)
            def _():
                for c in copies(nch):
                    fn(c)

    def n_in(blk):
        return jnp.where(bz_ref[blk] == 0, bn_ref[blk], 0)

    @pl.when(b == 0)
    def _():
        for_block(n_in(0), lambda k: in_copies(0, 0, k), lambda c: c.start())

    @pl.when(b + 1 < nb)
    def _():
        for_block(n_in(b + 1), lambda k: in_copies(b + 1, 1 - slot, k), lambda c: c.start())

    @pl.when(b >= 2)
    def _():
        for_block(bn_ref[b - 2], lambda k: out_copies(b - 2, slot, k), lambda c: c.wait())

    for_block(n_in(b), lambda k: in_copies(b, slot, k), lambda c: c.wait())

    e_now = be_ref[b]
    e_prev = be_ref[jnp.maximum(b - 1, 0)]

    @pl.when(jnp.logical_or(b == 0, e_now != e_prev))
    def _():
        wgu_bf[:, 0:ff] = wg_ref[0].astype(BF16)
        wgu_bf[:, ff:2 * ff] = wu_ref[0].astype(BF16)
        wd_bf[...] = wd_ref[0].astype(BF16)

    for nch in range(1, BLOCK_CHUNKS + 1):
        @pl.when(n_in(b) == nch)
        def _():
            rows = nch * GATHER_CHUNK
            gu = _dot(xbuf[slot, 0:rows, :].astype(BF16), wgu_bf[...])
            hid = _silu(gu[:, :ff]) * gu[:, ff:]
            obuf[slot, 0:rows, :] = _dot(hid.astype(BF16), wd_bf[...])

    @pl.when(bz_ref[b] != 0)
    def _():
        obuf[slot] = jnp.zeros(obuf.shape[1:], F32)

    for_block(bn_ref[b], lambda k: out_copies(b, slot, k), lambda c: c.start())

    @pl.when(b == nb - 1)
    def _():
        for_block(bn_ref[b], lambda k: out_copies(b, slot, k), lambda c: c.wait())

        @pl.when(nb > 1)
        def _():
            for_block(bn_ref[b - 1], lambda k: out_copies(b - 1, 1 - slot, k), lambda c: c.wait())


def _experts(block_e, block_x, block_n, block_z, xs, we_gate, we_up, we_down):
    nb = block_e.shape[0]
    n_rows = xs.shape[0]
    d = we_gate.shape[1]
    ff = we_gate.shape[2]
    rows = BLOCK_CHUNKS * GATHER_CHUNK
    grid_spec = pltpu.PrefetchScalarGridSpec(
        num_scalar_prefetch=4,
        grid=(nb,),
        in_specs=[
            pl.BlockSpec(memory_space=pl.ANY),
            pl.BlockSpec((1, d, ff), lambda b, be, *_: (be[b], 0, 0)),
            pl.BlockSpec((1, d, ff), lambda b, be, *_: (be[b], 0, 0)),
            pl.BlockSpec((1, ff, d), lambda b, be, *_: (be[b], 0, 0)),
        ],
        out_specs=pl.BlockSpec(memory_space=pl.ANY),
        scratch_shapes=[
            pltpu.VMEM((2, rows, d), F32),
            pltpu.VMEM((2, rows, d), F32),
            pltpu.SemaphoreType.DMA((2,)),
            pltpu.SemaphoreType.DMA((2,)),
            pltpu.VMEM((d, 2 * ff), BF16),
            pltpu.VMEM((ff, d), BF16),
        ],
    )
    return pl.pallas_call(
        _experts_body,
        grid_spec=grid_spec,
        out_shape=jax.ShapeDtypeStruct(xs.shape, F32),
        compiler_params=_dma_params(("arbitrary",)),
        name="experts",
    )(block_e, block_x, block_n, block_z, xs, we_gate, we_up, we_down)


def _combine_body(dt_hbm, wt_hbm, ys_hbm, part_ref, g_ref, b_ref, y_hbm, idx_smem, w_smem, gbuf, ybuf, isem, wsem,
                  gsem, osem):
    i = pl.program_id(0)
    n = pl.num_programs(0)
    tm = gbuf.shape[2]
    per_step = part_ref.shape[0] // tm
    more = i + 1 < n

    def idx_copies(tile, s):
        return _slot_index_copies(dt_hbm, tile * tm, tm, idx_smem, s, isem.at[s])

    def w_copies(tile, s):
        return _slot_index_copies(wt_hbm, tile * tm, tm, w_smem, s, wsem.at[s])

    def start(copies):
        for c in copies:
            c.start()

    def wait(copies):
        for c in copies:
            c.wait()

    def issue_rows(s):
        for j in range(TOP_K):
            def body(p, carry):
                for u in range(DMA_THREADS):
                    t = DMA_THREADS * p + u
                    row = idx_smem.at[s, j][t]
                    pltpu.make_async_copy(ys_hbm.at[row], gbuf.at[s, j, t], gsem.at[s]).start(priority=u)
                return carry
            lax.fori_loop(0, tm // DMA_THREADS, body, 0, unroll=4)

    def out_copies(s, tile):
        return [pltpu.make_async_copy(ybuf.at[s, pl.ds(0, tm), c, :],
                                      y_hbm.at[pl.ds(tile * tm, tm), pl.ds(c * LANES, LANES)], osem.at[s])
                for c in range(ybuf.shape[2])]

    def finish_tile(s):
        for j in range(TOP_K):
            pltpu.make_async_copy(ys_hbm.at[pl.ds(0, tm)], gbuf.at[s, j], gsem.at[s]).wait()
        wait(w_copies(0, s))

        @pl.when(i > 0)
        def _():
            wait(out_copies(s, 0))
        base = s * tm

        def token(t, carry):
            f = part_ref[base + t]
            for j in range(TOP_K):
                f = f + w_smem.at[s, j][t] * gbuf[s, j, t]
            ybuf[s, t] = f
            return carry
        lax.fori_loop(0, tm, token, 0, unroll=8)

        f = ybuf[s]
        inv_d = 1.0 / (f.shape[1] * f.shape[2])
        tile_sum = lambda a: jnp.sum(jnp.sum(a, axis=2, keepdims=True), axis=1, keepdims=True)
        fc = f - tile_sum(f) * inv_d
        var = tile_sum(fc * fc) * inv_d
        ybuf[s] = fc * lax.rsqrt(var + LN_EPS) * g_ref[...] + b_ref[...]
        start(out_copies(s, first + s))

    first = i * per_step

    @pl.when(i == 0)
    def _():
        start(idx_copies(0, 0))
        start(w_copies(0, 0))
        wait(idx_copies(0, 0))
        issue_rows(0)
        if per_step == 2:
            start(idx_copies(1, 1))
            start(w_copies(1, 1))

    if per_step == 1:
        finish_tile(0)

        @pl.when(more)
        def _():
            start(idx_copies(first + 1, 0))
            start(w_copies(first + 1, 0))
            wait(idx_copies(first + 1, 0))
            issue_rows(0)
    else:
        wait(idx_copies(first + 1, 1))
        issue_rows(1)

        @pl.when(more)
        def _():
            start(idx_copies(first + 2, 0))

        finish_tile(0)

        @pl.when(more)
        def _():
            start(w_copies(first + 2, 0))
            wait(idx_copies(first + 2, 0))
            issue_rows(0)
            start(idx_copies(first + 3, 1))

        finish_tile(1)

        @pl.when(more)
        def _():
            start(w_copies(first + 3, 1))

    @pl.when(jnp.logical_not(more))
    def _():
        for s in range(per_step):
            wait(out_copies(s, 0))


def _combine(dest, wts, ys, part, ln_g, ln_b, tm, per_step):
    n, sub, lanes = part.shape
    assert tm == LANES
    rows = tm * per_step
    return pl.pallas_call(
        _combine_body,
        grid=(n // rows,),
        in_specs=[
            pl.BlockSpec(memory_space=pl.ANY),
            pl.BlockSpec(memory_space=pl.ANY),
            pl.BlockSpec(memory_space=pl.ANY),
            pl.BlockSpec((rows, sub, lanes), lambda i: (i, 0, 0)),
            pl.BlockSpec((sub, lanes), lambda i: (0, 0)),
            pl.BlockSpec((sub, lanes), lambda i: (0, 0)),
        ],
        out_specs=pl.BlockSpec(memory_space=pl.ANY),
        out_shape=jax.ShapeDtypeStruct((n, sub * lanes), F32),
        scratch_shapes=[
            pltpu.SMEM((2, TOP_K, LANES), jnp.int32),
            pltpu.SMEM((2, TOP_K, LANES), F32),
            pltpu.VMEM((2, TOP_K, tm, sub, lanes), F32),
            pltpu.VMEM((2, tm, sub, lanes), F32),
            pltpu.SemaphoreType.DMA((2,)),
            pltpu.SemaphoreType.DMA((2,)),
            pltpu.SemaphoreType.DMA((2,)),
            pltpu.SemaphoreType.DMA((2,)),
        ],
        compiler_params=_dma_params(("arbitrary",)),
        name="combine",
    )(dest, wts, ys, part, ln_g.reshape(sub, lanes), ln_b.reshape(sub, lanes))


def kernel(x_prompt, x_sample, cache_k, cache_v, state_conv, ln_in_g, ln_in_b, w_in, b_in, conv_w, conv_b,
           conv_ln_g, conv_ln_b, w_conv_out, w_attn_out, w_out, ln1_g, ln1_b, w_router, router_bias,
           we_gate, we_up, we_down, ws_gate, ws_up, ws_down, ln2_g, ln2_b):
    bp, tp, d = x_prompt.shape
    bs, ts, _ = x_sample.shape
    past = cache_k.shape[2]
    assert w_in.shape[0] == DEPTH and d == N_HEADS * HEAD_DIM and conv_w.shape[2] == d
    assert tp % 1024 == 0 and ts <= KEY_BLOCK and ts % 16 == 0 and past % KEY_BLOCK == 0
    n_p, n_s = bp * tp, bs * ts
    assert n_s % LANES == 0

    row = lambda a: a.reshape(1, -1)
    lng, lnb = row(ln_in_g), row(ln_in_b)
    w_in_b = w_in[0].astype(BF16)
    w_co_b = w_conv_out[0].astype(BF16)
    conv_args = (lng, lnb, w_in_b, b_in, conv_w[0], conv_b, conv_ln_g, conv_ln_b, w_co_b)

    qp, kp, vp, gcp, sap, csp = _proj_conv(x_prompt, jnp.zeros((bp, CONV_WIDTH - 1, d), F32), *conv_args, tm=256)
    op = _attn_prompt(qp, kp, vp, tq=512)

    qs, ks, vs, gcs, sas, css = _proj_conv(x_sample, state_conv[0], *conv_args, tm=ts)
    pad_new = lambda a: jnp.pad(a, ((0, 0), (0, KEY_BLOCK - ts), (0, 0)))
    os_ = _attn_sample(qs, pad_new(ks), pad_new(vs), cache_k[0].reshape(bs, past, d), cache_v[0].reshape(bs, past, d))

    wr = w_router[0].T
    wr_hi = wr.astype(BF16)
    wr_lo = (wr - wr_hi.astype(F32)).astype(BF16)
    post_args = (lng, lnb, w_attn_out[0].astype(BF16), w_out[0].astype(BF16), ln1_g, ln1_b, wr_hi, wr_lo,
                 router_bias.reshape(N_EXPERTS, 1),
                 jnp.concatenate([ws_gate[0], ws_up[0]], axis=1).astype(BF16), ws_down[0].astype(BF16))
    flat = lambda a, n: a.reshape(n, d)
    zero_cnt = jnp.zeros((N_EXPERTS, LANES), F32)
    x1p, partp, idxp, wtp, rankp, cnt_p = _post_attn(
        flat(op, n_p), flat(gcp, n_p), flat(sap, n_p), flat(x_prompt, n_p), zero_cnt, *post_args, tm=512)
    x1s, parts, idxs, wts, ranks, cnt = _post_attn(
        flat(os_, n_s), flat(gcs, n_s), flat(sas, n_s), flat(x_sample, n_s), cnt_p, *post_args, tm=n_s)

    i32 = jnp.int32
    n_tok = n_p + n_s
    total_chunks = (n_tok * TOP_K + N_EXPERTS * (GATHER_CHUNK - 1) + GATHER_CHUNK - 1) // GATHER_CHUNK
    n_blocks = total_chunks // BLOCK_CHUNKS + N_EXPERTS
    counts = cnt[:, 0].astype(i32)
    chunks = (counts + GATHER_CHUNK - 1) // GATHER_CHUNK
    chunk_end = jnp.cumsum(chunks)
    chunk_start = chunk_end - chunks
    row_start = chunk_start * GATHER_CHUNK
    used_chunks = chunk_end[-1]
    blocks = (chunks + BLOCK_CHUNKS - 1) // BLOCK_CHUNKS
    blk_end = jnp.cumsum(blocks)
    blk_start = blk_end - blocks
    bid = jnp.arange(n_blocks, dtype=i32)
    block_e = jnp.minimum(jnp.sum(bid[:, None] >= blk_end[None, :], axis=1), N_EXPERTS - 1).astype(i32)
    k_in_e = bid - blk_start[block_e]
    is_tail = bid >= blk_end[-1]
    tail_x = used_chunks + BLOCK_CHUNKS * (bid - blk_end[-1])
    block_x = jnp.where(is_tail, tail_x, chunk_start[block_e] + BLOCK_CHUNKS * k_in_e)
    block_n = jnp.where(is_tail, total_chunks - tail_x, chunks[block_e] - BLOCK_CHUNKS * k_in_e)
    block_n = jnp.clip(block_n, 0, BLOCK_CHUNKS).astype(i32)
    block_x = jnp.where(block_n > 0, block_x, 0).astype(i32)
    block_z = is_tail.astype(i32)
    meta = jnp.stack([used_chunks, used_chunks]).astype(i32)
    last_chunk = jnp.where(chunks > 0, chunk_end - 1, -1).astype(i32)

    rs_f = row_start.astype(F32).reshape(N_EXPERTS, 1)
    dest_p = _dest(idxp, rankp, rs_f, tm=1024)
    dest_s = _dest(idxs, ranks, rs_f, tm=n_s)

    xs = _dispatch(meta, last_chunk, dest_p, dest_s, x1p, x1s, total_chunks * GATHER_CHUNK, tm=512)
    ys = _experts(block_e, block_x, block_n, block_z, xs, we_gate[0], we_up[0], we_down[0])

    yp = _combine(dest_p, wtp, ys, partp, ln2_g, ln2_b, tm=128, per_step=2)
    ysm = _combine(dest_s, wts, ys, parts, ln2_g, ln2_b, tm=n_s, per_step=1)

    heads = lambda a, b, t: a.reshape(1, b, t, N_HEADS, HEAD_DIM)
    return (yp.reshape(bp, tp, d), ysm.reshape(bs, ts, d),
            heads(kp, bp, tp), heads(vp, bp, tp), csp[None],
            heads(ks, bs, ts), heads(vs, bs, ts), css[None])
```

```python
import functools

import jax
import jax.numpy as jnp
from jax import lax
from jax.experimental import pallas as pl
from jax.experimental.pallas import tpu as pltpu

F32 = jnp.float32
BF16 = jnp.bfloat16

N_HEADS = 16
HEAD_DIM = 64
CONV_WIDTH = 31
N_EXPERTS = 256
TOP_K = 8
N_GROUPS = 8
TOPK_GROUPS = 4
GROUP_SIZE = N_EXPERTS // N_GROUPS
ROUTED_SCALE = 2.5
LN_EPS = 1e-5
DEPTH = 1
DEEPNORM_ALPHA = (2 * DEPTH) ** 0.25
ATTN_SCALE = HEAD_DIM ** -0.5

LANES = 128
SUBLANES = 8
VMEM_LIMIT_BYTES = 56 * 1024 * 1024

KEY_BLOCK = 128
HIST_PAD = 32
GATHER_CHUNK = 128
BLOCK_CHUNKS = 4
DMA_THREADS = 2
LOG_DEAD = -110.0


def _layer_norm(x, g, b):
    mu = jnp.mean(x, axis=-1, keepdims=True)
    xc = x - mu
    var = jnp.mean(xc * xc, axis=-1, keepdims=True)
    return xc * lax.rsqrt(var + LN_EPS) * g + b


def _sigmoid(x):
    return 1.0 / (1.0 + jnp.exp(-x))


def _silu(x):
    return x * _sigmoid(x)


def _dot(a, b):
    return jnp.dot(a, b, preferred_element_type=F32)


def _dot_nt(a, b):
    return lax.dot_general(a, b, (((1,), (1,)), ((), ())), preferred_element_type=F32)


def _params(sem):
    return pltpu.CompilerParams(dimension_semantics=sem, vmem_limit_bytes=VMEM_LIMIT_BYTES)


def _dma_params(sem):
    return pltpu.CompilerParams(dimension_semantics=sem, vmem_limit_bytes=VMEM_LIMIT_BYTES,
                                disable_bounds_checks=True)


def _proj_conv_body(x_ref, hist_ref, lng_ref, lnb_ref, win_ref, bin_ref, cw_ref, cb_ref, clg_ref,
                    clb_ref, wco_ref, q_ref, k_ref, v_ref, gc_ref, sa_ref, cs_ref, ubuf, cbuf, ush):
    tm = x_ref.shape[1]
    d = x_ref.shape[2]
    hist = CONV_WIDTH - 1
    lead = HIST_PAD - hist

    @pl.when(pl.program_id(1) == 0)
    def _():
        ubuf[0:lead, :] = jnp.zeros((lead, d), F32)
        ubuf[lead:HIST_PAD, :] = hist_ref[0]

    hb = _layer_norm(x_ref[0], lng_ref[...], lnb_ref[...]).astype(BF16)

    def proj(i):
        return _dot(hb, win_ref[:, i * d:(i + 1) * d]) + bin_ref[:, i * d:(i + 1) * d]

    ubuf[HIST_PAD:HIST_PAD + tm, :] = proj(0) * _sigmoid(proj(1))
    q_ref[0] = (proj(2) * ATTN_SCALE).astype(BF16)
    k_ref[0] = proj(3)
    v_ref[0] = proj(4)
    sa_ref[0] = _sigmoid(proj(6)).astype(BF16)

    span = tm + HIST_PAD - SUBLANES
    for ph in range(1, SUBLANES):
        ush[ph - 1, 0:span, :] = ubuf[ph:ph + span, :]
    rows = min(tm, 32)
    for r0 in range(0, tm, rows):
        acc = jnp.broadcast_to(cb_ref[...], (rows, d))
        for kk in range(CONV_WIDTH):
            base, ph = divmod(lead + kk, SUBLANES)
            lo = base * SUBLANES + r0
            tap = ubuf[lo:lo + rows, :] if ph == 0 else ush[ph - 1, lo:lo + rows, :]
            acc = acc + cw_ref[kk:kk + 1, :] * tap
        cbuf[r0:r0 + rows, :] = _silu(_layer_norm(acc, clg_ref[...], clb_ref[...])).astype(BF16)
    gc_ref[0] = (_sigmoid(proj(5)) * _dot(cbuf[...], wco_ref[...])).astype(BF16)

    cs_ref[0] = ubuf[tm + lead:tm + HIST_PAD, :]
    ubuf[0:HIST_PAD, :] = ubuf[tm:tm + HIST_PAD, :]


def _proj_conv(x, hist, ln_g, ln_b, w_in, b_in, conv_w, conv_b, cln_g, cln_b, w_co, tm):
    bsz, t, d = x.shape
    pw = w_in.shape[1]
    const = lambda shape: pl.BlockSpec(shape, lambda b, i: (0,) * len(shape))
    resident = lambda shape: pl.BlockSpec(shape, lambda b, i: (0,) * len(shape), pipeline_mode=pl.Buffered(1))
    tile = lambda: pl.BlockSpec((1, tm, d), lambda b, i: (b, i, 0))
    return pl.pallas_call(
        _proj_conv_body,
        grid=(bsz, t // tm),
        in_specs=[
            tile(),
            pl.BlockSpec((1, CONV_WIDTH - 1, d), lambda b, i: (b, 0, 0)),
            const((1, d)), const((1, d)), resident((d, pw)), const((1, pw)),
            const((CONV_WIDTH, d)), const((1, d)), const((1, d)), const((1, d)), resident((d, d)),
        ],
        out_specs=[tile(), tile(), tile(), tile(), tile(),
                   pl.BlockSpec((1, CONV_WIDTH - 1, d), lambda b, i: (b, 0, 0))],
        out_shape=[
            jax.ShapeDtypeStruct((bsz, t, d), BF16),
            jax.ShapeDtypeStruct((bsz, t, d), F32),
            jax.ShapeDtypeStruct((bsz, t, d), F32),
            jax.ShapeDtypeStruct((bsz, t, d), BF16),
            jax.ShapeDtypeStruct((bsz, t, d), BF16),
            jax.ShapeDtypeStruct((bsz, CONV_WIDTH - 1, d), F32),
        ],
        scratch_shapes=[pltpu.VMEM((HIST_PAD + tm, d), F32), pltpu.VMEM((tm, d), BF16),
                        pltpu.VMEM((SUBLANES - 1, tm + HIST_PAD - SUBLANES, d), F32)],
        compiler_params=_params(("arbitrary", "arbitrary")),
        name="proj_conv",
    )(x, hist, ln_g, ln_b, w_in, b_in, conv_w, conv_b, cln_g, cln_b, w_co)


def _suffix_matrix():
    row = lax.broadcasted_iota(jnp.int32, (KEY_BLOCK, KEY_BLOCK), 0)
    col = lax.broadcasted_iota(jnp.int32, (KEY_BLOCK, KEY_BLOCK), 1)
    return jnp.where(row >= col, 1.0, 0.0).astype(BF16)


def _sb_blocks(qss, kbs, vbs, acc, ll, vis, suffix):
    r = qss[0].shape[0]
    z = jnp.concatenate([_dot_nt(q, kb) for q, kb in zip(qss, kbs)], axis=0)
    lk = -(jnp.maximum(z, 0.0) + jnp.log(1.0 + jnp.exp(-jnp.abs(z))))
    if vis is not None:
        lk = jnp.where(vis, lk, 0.0)
    inb = _dot(lk.astype(BF16), suffix)
    arg = z + inb + ll
    if vis is not None:
        arg = jnp.where(vis, arg, -jnp.inf)
    a = jnp.exp(arg).astype(BF16)
    av = jnp.concatenate([_dot(a[i * r:(i + 1) * r], vb) for i, vb in enumerate(vbs)], axis=0)
    return acc + av, ll + inb[:, 0:1]


def _sb_block(qs, kb, vb, acc, ll, vis, suffix):
    return _sb_blocks([qs], [kb], [vb], acc, ll, vis, suffix)


def _sb_older_blocks(qs, k_ref, v_ref, first, acc, ll, suffix):
    def live(ll_):
        return (jnp.max(ll_, axis=0, keepdims=True)[0, 0] > LOG_DEAD).astype(jnp.int32)

    def cond(c):
        return jnp.logical_and(c[0] >= 0, c[3] > 0)

    def body(c):
        j, acc_, ll_, _ = c
        start = pl.multiple_of(j * KEY_BLOCK, KEY_BLOCK)
        kb = k_ref[0, pl.ds(start, KEY_BLOCK), :].astype(BF16)
        vb = v_ref[0, pl.ds(start, KEY_BLOCK), :].astype(BF16)
        acc_, ll_ = _sb_block(qs, kb, vb, acc_, ll_, None, suffix)
        return j - 1, acc_, ll_, live(ll_)

    _, acc, _, _ = lax.while_loop(cond, body, (first, acc, ll, live(ll)))
    return acc


def _stack_heads(q):
    lane = lax.broadcasted_iota(jnp.int32, q.shape, 1)
    zero = jnp.zeros_like(q)
    return jnp.concatenate([jnp.where(lane < HEAD_DIM, q, zero), jnp.where(lane >= HEAD_DIM, q, zero)], axis=0)


def _row_in_head(rows, cols):
    r = lax.broadcasted_iota(jnp.int32, (rows, cols), 0)
    return jnp.where(r >= rows // 2, r - rows // 2, r)


def _unstack_heads(acc):
    r = acc.shape[0] // 2
    lane = lax.broadcasted_iota(jnp.int32, (r, LANES), 1)
    return jnp.where(lane < HEAD_DIM, acc[:r], acc[r:])


def _attn_prompt_body(q_ref, k_ref, v_ref, o_ref):
    tq = q_ref.shape[1]
    nq = tq // KEY_BLOCK
    r = 2 * KEY_BLOCK
    g0 = pl.program_id(2) * nq
    col = lax.broadcasted_iota(jnp.int32, (nq * r, KEY_BLOCK), 1)
    row = lax.broadcasted_iota(jnp.int32, (nq * r, KEY_BLOCK), 0) & (KEY_BLOCK - 1)
    vis = col < row
    suffix = _suffix_matrix()

    def kv_blocks(js):
        starts = [pl.multiple_of(j * KEY_BLOCK, KEY_BLOCK) for j in js]
        return ([k_ref[0, pl.ds(s, KEY_BLOCK), :].astype(BF16) for s in starts],
                [v_ref[0, pl.ds(s, KEY_BLOCK), :].astype(BF16) for s in starts])

    def mask_finished(ll, off):
        return jnp.concatenate([jnp.where(g0 + qi - off >= 0, ll[qi * r:(qi + 1) * r], -jnp.inf)
                                for qi in range(nq)], axis=0)

    def any_live(ll, off):
        m = jnp.max(mask_finished(ll, off), axis=0, keepdims=True)
        return (m[0, 0] > LOG_DEAD).astype(jnp.int32)

    qss = [_stack_heads(q_ref[0, qi * KEY_BLOCK:(qi + 1) * KEY_BLOCK, :]) for qi in range(nq)]
    kbs, vbs = kv_blocks([g0 + qi for qi in range(nq)])
    acc, ll = _sb_blocks(qss, kbs, vbs, jnp.zeros((nq * r, LANES), F32), jnp.zeros((nq * r, 1), F32), vis,
                         suffix)

    def cond(c):
        return c[3] > 0

    def body(c):
        off, acc_, ll_, _ = c
        kbs_, vbs_ = kv_blocks([jnp.maximum(g0 + qi - off, 0) for qi in range(nq)])
        acc_, ll_ = _sb_blocks(qss, kbs_, vbs_, acc_, mask_finished(ll_, off), None, suffix)
        return off + 1, acc_, ll_, any_live(ll_, off + 1)

    one = jnp.int32(1)
    _, acc, _, _ = lax.while_loop(cond, body, (one, acc, ll, any_live(ll, one)))
    for qi in range(nq):
        o_ref[0, qi * KEY_BLOCK:(qi + 1) * KEY_BLOCK, :] = _unstack_heads(acc[qi * r:(qi + 1) * r]).astype(BF16)


def _attn_prompt(q, k, v, tq):
    bsz, t, d = q.shape
    pairs = d // LANES
    return pl.pallas_call(
        _attn_prompt_body,
        grid=(bsz, pairs, t // tq),
        in_specs=[
            pl.BlockSpec((1, tq, LANES), lambda b, p, i: (b, i, p)),
            pl.BlockSpec((1, t, LANES), lambda b, p, i: (b, 0, p)),
            pl.BlockSpec((1, t, LANES), lambda b, p, i: (b, 0, p)),
        ],
        out_specs=pl.BlockSpec((1, tq, LANES), lambda b, p, i: (b, i, p)),
        out_shape=jax.ShapeDtypeStruct((bsz, t, d), BF16),
        compiler_params=_params(("arbitrary", "arbitrary", "arbitrary")),
        name="attn_prompt",
    )(q, k, v)


def _attn_sample_body(q_ref, kn_ref, vn_ref, ck_ref, cv_ref, o_ref):
    tq = q_ref.shape[1]
    past_blocks = ck_ref.shape[1] // KEY_BLOCK
    qs = _stack_heads(q_ref[0])
    row = _row_in_head(2 * tq, KEY_BLOCK)
    col = lax.broadcasted_iota(jnp.int32, (2 * tq, KEY_BLOCK), 1)
    acc = jnp.zeros((2 * tq, LANES), F32)
    ll = jnp.zeros((2 * tq, 1), F32)
    suffix = _suffix_matrix()
    acc, ll = _sb_block(qs, kn_ref[0].astype(BF16), vn_ref[0].astype(BF16), acc, ll, col < row, suffix)
    acc = _sb_older_blocks(qs, ck_ref, cv_ref, past_blocks - 1, acc, ll, suffix)
    o_ref[0] = _unstack_heads(acc).astype(BF16)


def _attn_sample(q, k_new, v_new, cache_k, cache_v):
    bsz, t, d = q.shape
    past = cache_k.shape[1]
    pairs = d // LANES
    return pl.pallas_call(
        _attn_sample_body,
        grid=(bsz, pairs),
        in_specs=[
            pl.BlockSpec((1, t, LANES), lambda b, p: (b, 0, p)),
            pl.BlockSpec((1, KEY_BLOCK, LANES), lambda b, p: (b, 0, p)),
            pl.BlockSpec((1, KEY_BLOCK, LANES), lambda b, p: (b, 0, p)),
            pl.BlockSpec((1, past, LANES), lambda b, p: (b, 0, p)),
            pl.BlockSpec((1, past, LANES), lambda b, p: (b, 0, p)),
        ],
        out_specs=pl.BlockSpec((1, t, LANES), lambda b, p: (b, 0, p)),
        out_shape=jax.ShapeDtypeStruct((bsz, t, d), BF16),
        compiler_params=_params(("arbitrary", "arbitrary")),
        name="attn_sample",
    )(q, k_new, v_new, cache_k, cache_v)


def _first_argmax(cur, iota_f, n):
    m = jnp.max(cur, axis=0, keepdims=True)
    i = jnp.min(jnp.where(cur == m, iota_f, float(n)), axis=0, keepdims=True)
    return m, i


def _store_row_tiles(ref, val):
    for s in range(val.shape[1] // LANES):
        ref[:, s, :] = val[:, s * LANES:(s + 1) * LANES]


def _post_attn_body(o_ref, gc_ref, sa_ref, x_ref, cnt_ref, lng_ref, lnb_ref, wao_ref, wo_ref, l1g_ref,
                    l1b_ref, wrh_ref, wrl_ref, rb_ref, wsgu_ref, wsd_ref,
                    x1_ref, part_ref, idx_ref, wts_ref, rank_ref, cnt_out_ref, run):
    tm = x_ref.shape[0]
    ff = wsd_ref.shape[0]

    @pl.when(pl.program_id(0) == 0)
    def _():
        run[...] = cnt_ref[...]

    h = _layer_norm(x_ref[...], lng_ref[...], lnb_ref[...])
    att = _dot(o_ref[...], wao_ref[...])
    merged = gc_ref[...].astype(F32) + sa_ref[...].astype(F32) * att
    mixed = _dot(merged.astype(BF16), wo_ref[...])
    x1 = _layer_norm(DEEPNORM_ALPHA * h + mixed, l1g_ref[...], l1b_ref[...])
    _store_row_tiles(x1_ref, x1)

    x1h = x1.astype(BF16)
    gu = _dot(x1h, wsgu_ref[...])
    shared = _dot((_silu(gu[:, :ff]) * gu[:, ff:]).astype(BF16), wsd_ref[...])
    _store_row_tiles(part_ref, DEEPNORM_ALPHA * x1 + shared)

    x1l = (x1 - x1h.astype(F32)).astype(BF16)
    logits = _dot_nt(wrh_ref[...], x1h) + _dot_nt(wrl_ref[...], x1h) + _dot_nt(wrh_ref[...], x1l)
    scores = _sigmoid(logits)
    choice = scores + rb_ref[...]
    neg = -jnp.inf

    giota = lax.broadcasted_iota(jnp.int32, (GROUP_SIZE, tm), 0).astype(F32)
    gs = []
    for g in range(N_GROUPS):
        blk = choice[g * GROUP_SIZE:(g + 1) * GROUP_SIZE, :]
        m1, i1 = _first_argmax(blk, giota, GROUP_SIZE)
        m2 = jnp.max(jnp.where(giota == i1, neg, blk), axis=0, keepdims=True)
        gs.append(m1 + m2)
    gscore = jnp.concatenate(gs, axis=0)

    g8 = lax.broadcasted_iota(jnp.int32, (N_GROUPS, tm), 0).astype(F32)
    gsel = jnp.zeros((N_GROUPS, tm), F32)
    cur = gscore
    for _ in range(TOPK_GROUPS):
        _, i = _first_argmax(cur, g8, N_GROUPS)
        hit = g8 == i
        gsel = jnp.where(hit, 1.0, gsel)
        cur = jnp.where(hit, neg, cur)
    emask = jnp.concatenate(
        [jnp.broadcast_to(gsel[g:g + 1, :], (GROUP_SIZE, tm)) for g in range(N_GROUPS)], axis=0)

    eiota = lax.broadcasted_iota(jnp.int32, (N_EXPERTS, tm), 0).astype(F32)
    cur = jnp.where(emask > 0.0, choice, neg)
    hits, sel_w = [], []
    for r in range(TOP_K):
        _, i = _first_argmax(cur, eiota, N_EXPERTS)
        hit = eiota == i
        hits.append(hit)
        sel_w.append(jnp.sum(jnp.where(hit, scores, 0.0), axis=0, keepdims=True))
        cur = jnp.where(hit, neg, cur)
        idx_ref[r:r + 1, :] = i.astype(jnp.int32)
    wsum = sel_w[0]
    for r in range(1, TOP_K):
        wsum = wsum + sel_w[r]
    for r in range(TOP_K):
        wts_ref[r:r + 1, :] = sel_w[r] / wsum * ROUTED_SCALE

    picked = jnp.zeros((N_EXPERTS, tm), F32)
    for r in range(TOP_K):
        picked = jnp.where(hits[r], 1.0, picked)
    picked_b = picked.astype(BF16)
    trow = lax.broadcasted_iota(jnp.int32, (tm, tm), 0)
    tcol = lax.broadcasted_iota(jnp.int32, (tm, tm), 1)
    earlier = jnp.where(trow < tcol, 1.0, 0.0).astype(BF16)
    before = _dot(picked_b, earlier) + run[:, 0:1]
    for r in range(TOP_K):
        rank_ref[r:r + 1, :] = jnp.sum(jnp.where(hits[r], before, 0.0), axis=0, keepdims=True).astype(jnp.int32)
    run[...] = run[...] + _dot(picked_b, jnp.ones((tm, LANES), BF16))
    cnt_out_ref[...] = run[...]


def _post_attn(o, gc, sa, x, cnt_in, ln_g, ln_b, w_ao, w_o, l1g, l1b, wr_hi, wr_lo, r_bias, ws_gu, ws_d, tm):
    n, d = x.shape
    ff = ws_d.shape[0]
    const = lambda shape: pl.BlockSpec(shape, lambda i: (0,) * len(shape))
    tile = lambda: pl.BlockSpec((tm, d), lambda i: (i, 0))
    small = lambda: pl.BlockSpec((TOP_K, tm), lambda i: (0, i))
    row_tiles = lambda: pl.BlockSpec((tm, d // LANES, LANES), lambda i: (i, 0, 0))
    return pl.pallas_call(
        _post_attn_body,
        grid=(n // tm,),
        in_specs=[tile(), tile(), tile(), tile(), const((N_EXPERTS, LANES)),
                  const((1, d)), const((1, d)), const((d, d)), const((d, d)), const((1, d)), const((1, d)),
                  const((N_EXPERTS, d)), const((N_EXPERTS, d)), const((N_EXPERTS, 1)),
                  const((d, 2 * ff)), const((ff, d))],
        out_specs=[row_tiles(), row_tiles(), small(), small(), small(), const((N_EXPERTS, LANES))],
        out_shape=[
            jax.ShapeDtypeStruct((n, d // LANES, LANES), F32),
            jax.ShapeDtypeStruct((n, d // LANES, LANES), F32),
            jax.ShapeDtypeStruct((TOP_K, n), jnp.int32),
            jax.ShapeDtypeStruct((TOP_K, n), F32),
            jax.ShapeDtypeStruct((TOP_K, n), jnp.int32),
            jax.ShapeDtypeStruct((N_EXPERTS, LANES), F32),
        ],
        scratch_shapes=[pltpu.VMEM((N_EXPERTS, LANES), F32)],
        compiler_params=_params(("arbitrary",)),
        name="post_attn",
    )(o, gc, sa, x, cnt_in, ln_g, ln_b, w_ao, w_o, l1g, l1b, wr_hi, wr_lo, r_bias, ws_gu, ws_d)


def _dest_body(idx_ref, rank_ref, rs_ref, dest_ref):
    tm = idx_ref.shape[1]
    eiota = lax.broadcasted_iota(jnp.int32, (N_EXPERTS, tm), 0)
    rs = rs_ref[...]
    for r in range(TOP_K):
        base = jnp.sum(jnp.where(eiota == idx_ref[r:r + 1, :], rs, 0.0), axis=0, keepdims=True)
        dest_ref[r:r + 1, :] = base.astype(jnp.int32) + rank_ref[r:r + 1, :]


def _dest(idx, rank, row_start_f, tm):
    n = idx.shape[1]
    small = lambda: pl.BlockSpec((TOP_K, tm), lambda i: (0, i))
    return pl.pallas_call(
        _dest_body,
        grid=(n // tm,),
        in_specs=[small(), small(), pl.BlockSpec((N_EXPERTS, 1), lambda i: (0, 0))],
        out_specs=small(),
        out_shape=jax.ShapeDtypeStruct((TOP_K, n), jnp.int32),
        compiler_params=_params(("arbitrary",)),
        name="dest",
    )(idx, rank, row_start_f)


def _slot_index_copies(src_hbm, col0, width, dst_smem, s, sem):
    w = dst_smem.shape[1] // TOP_K
    return [pltpu.make_async_copy(src_hbm.at[pl.ds(j, 1), pl.ds(col0 + q * LANES, LANES)],
                                  dst_smem.at[s, pl.ds(j * w + q, 1), :], sem)
            for j in range(TOP_K) for q in range(width // LANES)]


def _dispatch_body(meta_ref, lc_ref, dp_hbm, ds_hbm, xp_ref, xsm_ref, xs_hbm, idx_smem, zbuf, isem, ssem, zsem):
    i = pl.program_id(0)
    n = pl.num_programs(0)
    tm = xp_ref.shape[0]
    n_s = xsm_ref.shape[0]
    total_chunks = xs_hbm.shape[0] // GATHER_CHUNK

    def idx_copies(tile, fn):
        @pl.when(tile < n - 1)
        def _():
            for c in _slot_index_copies(dp_hbm, tile * tm, tm, idx_smem, 0, isem):
                fn(c)

        @pl.when(tile == n - 1)
        def _():
            for c in _slot_index_copies(ds_hbm, 0, n_s, idx_smem, 0, isem):
                fn(c)

    @pl.when(i == 0)
    def _():
        idx_copies(0, lambda c: c.start())

    idx_copies(i, lambda c: c.wait())

    def scatter(src_ref):
        cnt = src_ref.shape[0]
        for j in range(TOP_K):
            for q in range(cnt // LANES):
                idx_row = idx_smem.at[0, j * (tm // LANES) + q]

                def body(p, carry, idx_row=idx_row, q=q):
                    for u in range(DMA_THREADS):
                        t = DMA_THREADS * p + u
                        row = idx_row[t]
                        pltpu.make_async_copy(src_ref.at[q * LANES + t], xs_hbm.at[row], ssem).start(priority=u)
                    return carry
                lax.fori_loop(0, LANES // DMA_THREADS, body, 0, unroll=4)

        @pl.when(i + 1 < n)
        def _():
            idx_copies(i + 1, lambda c: c.start())

        for j in range(TOP_K):
            pltpu.make_async_copy(src_ref, xs_hbm.at[pl.ds(0, cnt)], ssem).wait()

    @pl.when(i == 0)
    def _():
        zbuf[...] = jnp.zeros(zbuf.shape, F32)

        def zero_chunk(c):
            return pltpu.make_async_copy(zbuf, xs_hbm.at[pl.ds(c * GATHER_CHUNK, GATHER_CHUNK)], zsem)

        def each_expert(fn):
            def body(e, carry):
                @pl.when(lc_ref[e] >= 0)
                def _():
                    fn(lc_ref[e])
                return carry
            lax.fori_loop(0, N_EXPERTS, body, 0)

        def each_tail(fn):
            def body(c, carry):
                fn(c)
                return carry
            lax.fori_loop(meta_ref[0], total_chunks, body, 0)

        each_expert(lambda c: zero_chunk(c).start())
        each_tail(lambda c: zero_chunk(c).start())
        each_expert(lambda c: zero_chunk(c).wait())
        each_tail(lambda c: zero_chunk(c).wait())

    @pl.when(i < n - 1)
    def _():
        scatter(xp_ref)

    @pl.when(i == n - 1)
    def _():
        scatter(xsm_ref)


def _dispatch(meta, last_chunk, dest_p, dest_s, x1p, x1s, n_rows, tm):
    n_p, sub, lanes = x1p.shape
    n_s = x1s.shape[0]
    n_pt = n_p // tm
    grid_spec = pltpu.PrefetchScalarGridSpec(
        num_scalar_prefetch=2,
        grid=(n_pt + 1,),
        in_specs=[
            pl.BlockSpec(memory_space=pl.ANY),
            pl.BlockSpec(memory_space=pl.ANY),
            pl.BlockSpec((tm, sub, lanes), lambda i, *_: (jnp.minimum(i, n_pt - 1), 0, 0)),
            pl.BlockSpec((n_s, sub, lanes), lambda i, *_: (0, 0, 0)),
        ],
        out_specs=pl.BlockSpec(memory_space=pl.ANY),
        scratch_shapes=[
            pltpu.SMEM((1, TOP_K * tm // LANES, LANES), jnp.int32),
            pltpu.VMEM((GATHER_CHUNK, sub, lanes), F32),
            pltpu.SemaphoreType.DMA(()),
            pltpu.SemaphoreType.DMA(()),
            pltpu.SemaphoreType.DMA(()),
        ],
    )
    return pl.pallas_call(
        _dispatch_body,
        grid_spec=grid_spec,
        out_shape=jax.ShapeDtypeStruct((n_rows, sub, lanes), F32),
        compiler_params=_dma_params(("arbitrary",)),
        name="dispatch",
    )(meta, last_chunk, dest_p, dest_s, x1p, x1s)


def _experts_body(be_ref, bx_ref, bn_ref, bz_ref, xs_hbm, wg_ref, wu_ref, wd_ref, ys_hbm,
                  xbuf, obuf, gsem, osem, wgu_bf, wd_bf):
    b = pl.program_id(0)
    nb = pl.num_programs(0)
    slot = b % 2
    ff = wd_ref.shape[1]

    def in_copies(blk, s, nch):
        rows = pl.ds(bx_ref[blk] * GATHER_CHUNK, nch * GATHER_CHUNK)
        return [pltpu.make_async_copy(xs_hbm.at[rows, c, :],
                                      xbuf.at[s, pl.ds(0, nch * GATHER_CHUNK), pl.ds(c * LANES, LANES)],
                                      gsem.at[s]) for c in range(xs_hbm.shape[1])]

    def out_copies(blk, s, nch):
        rows = pl.ds(bx_ref[blk] * GATHER_CHUNK, nch * GATHER_CHUNK)
        return [pltpu.make_async_copy(obuf.at[s, pl.ds(0, nch * GATHER_CHUNK), pl.ds(c * LANES, LANES)],
                                      ys_hbm.at[rows, c, :], osem.at[s]) for c in range(ys_hbm.shape[1])]

    def for_block(count, copies, fn):
        for nch in range(1, BLOCK_CHUNKS + 1):
            @pl.when(count == nch)
            def _():
                for c in copies(nch):
                    fn(c)

    def n_in(blk):
        return jnp.where(bz_ref[blk] == 0, bn_ref[blk], 0)

    @pl.when(b == 0)
    def _():
        for_block(n_in(0), lambda k: in_copies(0, 0, k), lambda c: c.start())

    @pl.when(b + 1 < nb)
    def _():
        for_block(n_in(b + 1), lambda k: in_copies(b + 1, 1 - slot, k), lambda c: c.start())

    @pl.when(b >= 2)
    def _():
        for_block(bn_ref[b - 2], lambda k: out_copies(b - 2, slot, k), lambda c: c.wait())

    for_block(n_in(b), lambda k: in_copies(b, slot, k), lambda c: c.wait())

    e_now = be_ref[b]
    e_prev = be_ref[jnp.maximum(b - 1, 0)]

    @pl.when(jnp.logical_or(b == 0, e_now != e_prev))
    def _():
        wgu_bf[:, 0:ff] = wg_ref[0].astype(BF16)
        wgu_bf[:, ff:2 * ff] = wu_ref[0].astype(BF16)
        wd_bf[...] = wd_ref[0].astype(BF16)

    for nch in range(1, BLOCK_CHUNKS + 1):
        @pl.when(n_in(b) == nch)
        def _():
            rows = nch * GATHER_CHUNK
            gu = _dot(xbuf[slot, 0:rows, :].astype(BF16), wgu_bf[...])
            hid = _silu(gu[:, :ff]) * gu[:, ff:]
            obuf[slot, 0:rows, :] = _dot(hid.astype(BF16), wd_bf[...])

    @pl.when(bz_ref[b] != 0)
    def _():
        obuf[slot] = jnp.zeros(obuf.shape[1:], F32)

    for_block(bn_ref[b], lambda k: out_copies(b, slot, k), lambda c: c.start())

    @pl.when(b == nb - 1)
    def _():
        for_block(bn_ref[b], lambda k: out_copies(b, slot, k), lambda c: c.wait())

        @pl.when(nb > 1)
        def _():
            for_block(bn_ref[b - 1], lambda k: out_copies(b - 1, 1 - slot, k), lambda c: c.wait())


def _experts(block_e, block_x, block_n, block_z, xs, we_gate, we_up, we_down):
    nb = block_e.shape[0]
    n_rows = xs.shape[0]
    d = we_gate.shape[1]
    ff = we_gate.shape[2]
    rows = BLOCK_CHUNKS * GATHER_CHUNK
    grid_spec = pltpu.PrefetchScalarGridSpec(
        num_scalar_prefetch=4,
        grid=(nb,),
        in_specs=[
            pl.BlockSpec(memory_space=pl.ANY),
            pl.BlockSpec((1, d, ff), lambda b, be, *_: (be[b], 0, 0)),
            pl.BlockSpec((1, d, ff), lambda b, be, *_: (be[b], 0, 0)),
            pl.BlockSpec((1, ff, d), lambda b, be, *_: (be[b], 0, 0)),
        ],
        out_specs=pl.BlockSpec(memory_space=pl.ANY),
        scratch_shapes=[
            pltpu.VMEM((2, rows, d), F32),
            pltpu.VMEM((2, rows, d), F32),
            pltpu.SemaphoreType.DMA((2,)),
            pltpu.SemaphoreType.DMA((2,)),
            pltpu.VMEM((d, 2 * ff), BF16),
            pltpu.VMEM((ff, d), BF16),
        ],
    )
    return pl.pallas_call(
        _experts_body,
        grid_spec=grid_spec,
        out_shape=jax.ShapeDtypeStruct(xs.shape, F32),
        compiler_params=_dma_params(("arbitrary",)),
        name="experts",
    )(block_e, block_x, block_n, block_z, xs, we_gate, we_up, we_down)


def _combine_body(dt_hbm, wt_hbm, ys_hbm, part_ref, g_ref, b_ref, y_hbm, idx_smem, w_smem, gbuf, ybuf, isem, wsem,
                  gsem, osem):
    i = pl.program_id(0)
    n = pl.num_programs(0)
    tm = gbuf.shape[2]
    per_step = part_ref.shape[0] // tm
    more = i + 1 < n

    def idx_copies(tile, s):
        return _slot_index_copies(dt_hbm, tile * tm, tm, idx_smem, s, isem.at[s])

    def w_copies(tile, s):
        return _slot_index_copies(wt_hbm, tile * tm, tm, w_smem, s, wsem.at[s])

    def start(copies):
        for c in copies:
            c.start()

    def wait(copies):
        for c in copies:
            c.wait()

    def issue_rows(s):
        for j in range(TOP_K):
            def body(p, carry):
                for u in range(DMA_THREADS):
                    t = DMA_THREADS * p + u
                    row = idx_smem.at[s, j][t]
                    pltpu.make_async_copy(ys_hbm.at[row], gbuf.at[s, j, t], gsem.at[s]).start(priority=u)
                return carry
            lax.fori_loop(0, tm // DMA_THREADS, body, 0, unroll=4)

    def out_copies(s, tile):
        return [pltpu.make_async_copy(ybuf.at[s, pl.ds(0, tm), c, :],
                                      y_hbm.at[pl.ds(tile * tm, tm), pl.ds(c * LANES, LANES)], osem.at[s])
                for c in range(ybuf.shape[2])]

    def finish_tile(s):
        for j in range(TOP_K):
            pltpu.make_async_copy(ys_hbm.at[pl.ds(0, tm)], gbuf.at[s, j], gsem.at[s]).wait()
        wait(w_copies(0, s))

        @pl.when(i > 0)
        def _():
            wait(out_copies(s, 0))
        base = s * tm

        def token(t, carry):
            f = part_ref[base + t]
            for j in range(TOP_K):
                f = f + w_smem.at[s, j][t] * gbuf[s, j, t]
            ybuf[s, t] = f
            return carry
        lax.fori_loop(0, tm, token, 0, unroll=8)

        f = ybuf[s]
        inv_d = 1.0 / (f.shape[1] * f.shape[2])
        tile_sum = lambda a: jnp.sum(jnp.sum(a, axis=2, keepdims=True), axis=1, keepdims=True)
        fc = f - tile_sum(f) * inv_d
        var = tile_sum(fc * fc) * inv_d
        ybuf[s] = fc * lax.rsqrt(var + LN_EPS) * g_ref[...] + b_ref[...]
        start(out_copies(s, first + s))

    first = i * per_step

    @pl.when(i == 0)
    def _():
        start(idx_copies(0, 0))
        start(w_copies(0, 0))
        wait(idx_copies(0, 0))
        issue_rows(0)
        if per_step == 2:
            start(idx_copies(1, 1))
            start(w_copies(1, 1))

    if per_step == 1:
        finish_tile(0)

        @pl.when(more)
        def _():
            start(idx_copies(first + 1, 0))
            start(w_copies(first + 1, 0))
            wait(idx_copies(first + 1, 0))
            issue_rows(0)
    else:
        wait(idx_copies(first + 1, 1))
        issue_rows(1)

        @pl.when(more)
        def _():
            start(idx_copies(first + 2, 0))

        finish_tile(0)

        @pl.when(more)
        def _():
            start(w_copies(first + 2, 0))
            wait(idx_copies(first + 2, 0))
            issue_rows(0)
            start(idx_copies(first + 3, 1))

        finish_tile(1)

        @pl.when(more)
        def _():
            start(w_copies(first + 3, 1))

    @pl.when(jnp.logical_not(more))
    def _():
        for s in range(per_step):
            wait(out_copies(s, 0))


def _combine(dest, wts, ys, part, ln_g, ln_b, tm, per_step):
    n, sub, lanes = part.shape
    assert tm == LANES
    rows = tm * per_step
    return pl.pallas_call(
        _combine_body,
        grid=(n // rows,),
        in_specs=[
            pl.BlockSpec(memory_space=pl.ANY),
            pl.BlockSpec(memory_space=pl.ANY),
            pl.BlockSpec(memory_space=pl.ANY),
            pl.BlockSpec((rows, sub, lanes), lambda i: (i, 0, 0)),
            pl.BlockSpec((sub, lanes), lambda i: (0, 0)),
            pl.BlockSpec((sub, lanes), lambda i: (0, 0)),
        ],
        out_specs=pl.BlockSpec(memory_space=pl.ANY),
        out_shape=jax.ShapeDtypeStruct((n, sub * lanes), F32),
        scratch_shapes=[
            pltpu.SMEM((2, TOP_K, LANES), jnp.int32),
            pltpu.SMEM((2, TOP_K, LANES), F32),
            pltpu.VMEM((2, TOP_K, tm, sub, lanes), F32),
            pltpu.VMEM((2, tm, sub, lanes), F32),
            pltpu.SemaphoreType.DMA((2,)),
            pltpu.SemaphoreType.DMA((2,)),
            pltpu.SemaphoreType.DMA((2,)),
            pltpu.SemaphoreType.DMA((2,)),
        ],
        compiler_params=_dma_params(("arbitrary",)),
        name="combine",
    )(dest, wts, ys, part, ln_g.reshape(sub, lanes), ln_b.reshape(sub, lanes))


def kernel(x_prompt, x_sample, cache_k, cache_v, state_conv, ln_in_g, ln_in_b, w_in, b_in, conv_w, conv_b,
           conv_ln_g, conv_ln_b, w_conv_out, w_attn_out, w_out, ln1_g, ln1_b, w_router, router_bias,
           we_gate, we_up, we_down, ws_gate, ws_up, ws_down, ln2_g, ln2_b):
    bp, tp, d = x_prompt.shape
    bs, ts, _ = x_sample.shape
    past = cache_k.shape[2]
    assert w_in.shape[0] == DEPTH and d == N_HEADS * HEAD_DIM and conv_w.shape[2] == d
    assert tp % 1024 == 0 and ts <= KEY_BLOCK and ts % 16 == 0 and past % KEY_BLOCK == 0
    n_p, n_s = bp * tp, bs * ts
    assert n_s % LANES == 0

    row = lambda a: a.reshape(1, -1)
    lng, lnb = row(ln_in_g), row(ln_in_b)
    w_in_b = w_in[0].astype(BF16)
    w_co_b = w_conv_out[0].astype(BF16)
    conv_args = (lng, lnb, w_in_b, b_in, conv_w[0], conv_b, conv_ln_g, conv_ln_b, w_co_b)

    qp, kp, vp, gcp, sap, csp = _proj_conv(x_prompt, jnp.zeros((bp, CONV_WIDTH - 1, d), F32), *conv_args, tm=256)
    op = _attn_prompt(qp, kp, vp, tq=1024)

    qs, ks, vs, gcs, sas, css = _proj_conv(x_sample, state_conv[0], *conv_args, tm=ts)
    pad_new = lambda a: jnp.pad(a, ((0, 0), (0, KEY_BLOCK - ts), (0, 0)))
    os_ = _attn_sample(qs, pad_new(ks), pad_new(vs), cache_k[0].reshape(bs, past, d), cache_v[0].reshape(bs, past, d))

    wr = w_router[0].T
    wr_hi = wr.astype(BF16)
    wr_lo = (wr - wr_hi.astype(F32)).astype(BF16)
    post_args = (lng, lnb, w_attn_out[0].astype(BF16), w_out[0].astype(BF16), ln1_g, ln1_b, wr_hi, wr_lo,
                 router_bias.reshape(N_EXPERTS, 1),
                 jnp.concatenate([ws_gate[0], ws_up[0]], axis=1).astype(BF16), ws_down[0].astype(BF16))
    flat = lambda a, n: a.reshape(n, d)
    zero_cnt = jnp.zeros((N_EXPERTS, LANES), F32)
    x1p, partp, idxp, wtp, rankp, cnt_p = _post_attn(
        flat(op, n_p), flat(gcp, n_p), flat(sap, n_p), flat(x_prompt, n_p), zero_cnt, *post_args, tm=512)
    x1s, parts, idxs, wts, ranks, cnt = _post_attn(
        flat(os_, n_s), flat(gcs, n_s), flat(sas, n_s), flat(x_sample, n_s), cnt_p, *post_args, tm=n_s)

    i32 = jnp.int32
    n_tok = n_p + n_s
    total_chunks = (n_tok * TOP_K + N_EXPERTS * (GATHER_CHUNK - 1) + GATHER_CHUNK - 1) // GATHER_CHUNK
    n_blocks = total_chunks // BLOCK_CHUNKS + N_EXPERTS
    counts = cnt[:, 0].astype(i32)
    chunks = (counts + GATHER_CHUNK - 1) // GATHER_CHUNK
    chunk_end = jnp.cumsum(chunks)
    chunk_start = chunk_end - chunks
    row_start = chunk_start * GATHER_CHUNK
    used_chunks = chunk_end[-1]
    blocks = (chunks + BLOCK_CHUNKS - 1) // BLOCK_CHUNKS
    blk_end = jnp.cumsum(blocks)
    blk_start = blk_end - blocks
    bid = jnp.arange(n_blocks, dtype=i32)
    block_e = jnp.minimum(jnp.sum(bid[:, None] >= blk_end[None, :], axis=1), N_EXPERTS - 1).astype(i32)
    k_in_e = bid - blk_start[block_e]
    is_tail = bid >= blk_end[-1]
    tail_x = used_chunks + BLOCK_CHUNKS * (bid - blk_end[-1])
    block_x = jnp.where(is_tail, tail_x, chunk_start[block_e] + BLOCK_CHUNKS * k_in_e)
    block_n = jnp.where(is_tail, total_chunks - tail_x, chunks[block_e] - BLOCK_CHUNKS * k_in_e)
    block_n = jnp.clip(block_n, 0, BLOCK_CHUNKS).astype(i32)
    block_x = jnp.where(block_n > 0, block_x, 0).astype(i32)
    block_z = is_tail.astype(i32)
    meta = jnp.stack([used_chunks, used_chunks]).astype(i32)
    last_chunk = jnp.where(chunks > 0, chunk_end - 1, -1).astype(i32)

    rs_f = row_start.astype(F32).reshape(N_EXPERTS, 1)
    dest_p = _dest(idxp, rankp, rs_f, tm=1024)
    dest_s = _dest(idxs, ranks, rs_f, tm=n_s)

    xs = _dispatch(meta, last_chunk, dest_p, dest_s, x1p, x1s, total_chunks * GATHER_CHUNK, tm=1024)
    ys = _experts(block_e, block_x, block_n, block_z, xs, we_gate[0], we_up[0], we_down[0])

    yp = _combine(dest_p, wtp, ys, partp, ln2_g, ln2_b, tm=128, per_step=2)
    ysm = _combine(dest_s, wts, ys, parts, ln2_g, ln2_b, tm=n_s, per_step=1)

    heads = lambda a, b, t: a.reshape(1, b, t, N_HEADS, HEAD_DIM)
    return (yp.reshape(bp, tp, d), ysm.reshape(bs, ts, d),
            heads(kp, bp, tp), heads(vp, bp, tp), csp[None],
            heads(ks, bs, ts), heads(vs, bs, ts), css[None])
```

```python
import jax
import jax.numpy as jnp
from jax import lax
from jax.experimental import pallas as pl
from jax.experimental.pallas import tpu as pltpu

F32 = jnp.float32
BF16 = jnp.bfloat16

N_HEADS = 16
HEAD_DIM = 64
CONV_WIDTH = 31
N_EXPERTS = 256
TOP_K = 8
N_GROUPS = 8
TOPK_GROUPS = 4
GROUP_SIZE = N_EXPERTS // N_GROUPS
ROUTED_SCALE = 2.5
LN_EPS = 1e-5
DEPTH = 1
DEEPNORM_ALPHA = (2 * DEPTH) ** 0.25
ATTN_SCALE = HEAD_DIM ** -0.5

LANES = 128
SUBLANES = 8
VMEM_LIMIT_BYTES = 56 * 1024 * 1024

KEY_BLOCK = 128
HIST_PAD = 32
GATHER_CHUNK = 128
BLOCK_CHUNKS = 4
DMA_THREADS = 2
LOG2E = 1.4426950408889634
LOG2_DEAD = -110.0 * LOG2E


def _layer_norm(x, g, b):
    mu = jnp.mean(x, axis=-1, keepdims=True)
    xc = x - mu
    var = jnp.mean(xc * xc, axis=-1, keepdims=True)
    return xc * lax.rsqrt(var + LN_EPS) * g + b


def _sigmoid(x):
    return 1.0 / (1.0 + jnp.exp(-x))


def _silu(x):
    return x * _sigmoid(x)


def _dot(a, b):
    return jnp.dot(a, b, preferred_element_type=F32)


def _dot_nt(a, b):
    return lax.dot_general(a, b, (((1,), (1,)), ((), ())), preferred_element_type=F32)


def _params(sem):
    return pltpu.CompilerParams(dimension_semantics=sem, vmem_limit_bytes=VMEM_LIMIT_BYTES)


def _dma_params(sem):
    return pltpu.CompilerParams(dimension_semantics=sem, vmem_limit_bytes=VMEM_LIMIT_BYTES,
                                disable_bounds_checks=True)


def _proj_conv_body(x_ref, hist_ref, lng_ref, lnb_ref, win_ref, bin_ref, cw_ref, cb_ref, clg_ref,
                    clb_ref, wco_ref, q_ref, k_ref, v_ref, gc_ref, sa_ref, cs_ref, ubuf, cbuf, ush):
    tm = x_ref.shape[1]
    d = x_ref.shape[2]
    hist = CONV_WIDTH - 1
    lead = HIST_PAD - hist

    @pl.when(pl.program_id(1) == 0)
    def _():
        ubuf[0:lead, :] = jnp.zeros((lead, d), F32)
        ubuf[lead:HIST_PAD, :] = hist_ref[0]

    hb = _layer_norm(x_ref[0], lng_ref[...], lnb_ref[...]).astype(BF16)

    def proj(i):
        return _dot(hb, win_ref[:, i * d:(i + 1) * d]) + bin_ref[:, i * d:(i + 1) * d]

    ubuf[HIST_PAD:HIST_PAD + tm, :] = proj(0) * _sigmoid(proj(1))
    q_ref[0] = (proj(2) * (ATTN_SCALE * LOG2E)).astype(BF16)
    k_ref[0] = proj(3)
    v_ref[0] = proj(4)
    sa_ref[0] = _sigmoid(proj(6)).astype(BF16)

    span = tm + HIST_PAD - SUBLANES
    for ph in range(1, SUBLANES):
        ush[ph - 1, 0:span, :] = ubuf[ph:ph + span, :]
    rows = min(tm, 32)
    for r0 in range(0, tm, rows):
        acc = jnp.broadcast_to(cb_ref[...], (rows, d))
        for kk in range(CONV_WIDTH):
            base, ph = divmod(lead + kk, SUBLANES)
            lo = base * SUBLANES + r0
            tap = ubuf[lo:lo + rows, :] if ph == 0 else ush[ph - 1, lo:lo + rows, :]
            acc = acc + cw_ref[kk:kk + 1, :] * tap
        cbuf[r0:r0 + rows, :] = _silu(_layer_norm(acc, clg_ref[...], clb_ref[...])).astype(BF16)
    gc_ref[0] = (_sigmoid(proj(5)) * _dot(cbuf[...], wco_ref[...])).astype(BF16)

    cs_ref[0] = ubuf[tm + lead:tm + HIST_PAD, :]
    ubuf[0:HIST_PAD, :] = ubuf[tm:tm + HIST_PAD, :]


def _proj_conv(x, hist, ln_g, ln_b, w_in, b_in, conv_w, conv_b, cln_g, cln_b, w_co, tm):
    bsz, t, d = x.shape
    pw = w_in.shape[1]
    const = lambda shape: pl.BlockSpec(shape, lambda b, i: (0,) * len(shape))
    resident = lambda shape: pl.BlockSpec(shape, lambda b, i: (0,) * len(shape), pipeline_mode=pl.Buffered(1))
    tile = lambda: pl.BlockSpec((1, tm, d), lambda b, i: (b, i, 0))
    return pl.pallas_call(
        _proj_conv_body,
        grid=(bsz, t // tm),
        in_specs=[
            tile(),
            pl.BlockSpec((1, CONV_WIDTH - 1, d), lambda b, i: (b, 0, 0)),
            const((1, d)), const((1, d)), resident((d, pw)), const((1, pw)),
            const((CONV_WIDTH, d)), const((1, d)), const((1, d)), const((1, d)), resident((d, d)),
        ],
        out_specs=[tile(), tile(), tile(), tile(), tile(),
                   pl.BlockSpec((1, CONV_WIDTH - 1, d), lambda b, i: (b, 0, 0))],
        out_shape=[
            jax.ShapeDtypeStruct((bsz, t, d), BF16),
            jax.ShapeDtypeStruct((bsz, t, d), F32),
            jax.ShapeDtypeStruct((bsz, t, d), F32),
            jax.ShapeDtypeStruct((bsz, t, d), BF16),
            jax.ShapeDtypeStruct((bsz, t, d), BF16),
            jax.ShapeDtypeStruct((bsz, CONV_WIDTH - 1, d), F32),
        ],
        scratch_shapes=[pltpu.VMEM((HIST_PAD + tm, d), F32), pltpu.VMEM((tm, d), BF16),
                        pltpu.VMEM((SUBLANES - 1, tm + HIST_PAD - SUBLANES, d), F32)],
        compiler_params=_params(("arbitrary", "arbitrary")),
        name="proj_conv",
    )(x, hist, ln_g, ln_b, w_in, b_in, conv_w, conv_b, cln_g, cln_b, w_co)


def _suffix_matrix():
    row = lax.broadcasted_iota(jnp.int32, (KEY_BLOCK, KEY_BLOCK), 0)
    col = lax.broadcasted_iota(jnp.int32, (KEY_BLOCK, KEY_BLOCK), 1)
    return jnp.where(row >= col, 1.0, 0.0).astype(BF16)


def _sb_blocks(qss, kbs, vbs, acc, ll, vis, suffix):
    r = qss[0].shape[0]
    z = jnp.concatenate([_dot_nt(q, kb) for q, kb in zip(qss, kbs)], axis=0)
    lk = -(jnp.maximum(z, 0.0) + jnp.log2(1.0 + jnp.exp2(-jnp.abs(z))))
    if vis is not None:
        lk = jnp.where(vis, lk, 0.0)
    inb = _dot(lk.astype(BF16), suffix)
    arg = z + inb + ll
    if vis is not None:
        arg = jnp.where(vis, arg, -jnp.inf)
    a = jnp.exp2(arg).astype(BF16)
    av = jnp.concatenate([_dot(a[i * r:(i + 1) * r], vb) for i, vb in enumerate(vbs)], axis=0)
    return acc + av, ll + inb[:, 0:1]


def _sb_block(qs, kb, vb, acc, ll, vis, suffix):
    return _sb_blocks([qs], [kb], [vb], acc, ll, vis, suffix)


def _sb_older_blocks(qs, k_ref, v_ref, first, acc, ll, suffix):
    def live(ll_):
        return (jnp.max(ll_, axis=0, keepdims=True)[0, 0] > LOG2_DEAD).astype(jnp.int32)

    def cond(c):
        return jnp.logical_and(c[0] >= 0, c[3] > 0)

    def body(c):
        j, acc_, ll_, _ = c
        start = pl.multiple_of(j * KEY_BLOCK, KEY_BLOCK)
        kb = k_ref[0, pl.ds(start, KEY_BLOCK), :].astype(BF16)
        vb = v_ref[0, pl.ds(start, KEY_BLOCK), :].astype(BF16)
        acc_, ll_ = _sb_block(qs, kb, vb, acc_, ll_, None, suffix)
        return j - 1, acc_, ll_, live(ll_)

    _, acc, _, _ = lax.while_loop(cond, body, (first, acc, ll, live(ll)))
    return acc


def _stack_heads(q):
    lane = lax.broadcasted_iota(jnp.int32, q.shape, 1)
    zero = jnp.zeros_like(q)
    return jnp.concatenate([jnp.where(lane < HEAD_DIM, q, zero), jnp.where(lane >= HEAD_DIM, q, zero)], axis=0)


def _row_in_head(rows, cols):
    r = lax.broadcasted_iota(jnp.int32, (rows, cols), 0)
    return jnp.where(r >= rows // 2, r - rows // 2, r)


def _unstack_heads(acc):
    r = acc.shape[0] // 2
    lane = lax.broadcasted_iota(jnp.int32, (r, LANES), 1)
    return jnp.where(lane < HEAD_DIM, acc[:r], acc[r:])


def _attn_prompt_body(q_ref, k_ref, v_ref, o_ref):
    tq = q_ref.shape[1]
    nq = tq // KEY_BLOCK
    r = 2 * KEY_BLOCK
    g0 = pl.program_id(2) * nq
    col = lax.broadcasted_iota(jnp.int32, (nq * r, KEY_BLOCK), 1)
    row = lax.broadcasted_iota(jnp.int32, (nq * r, KEY_BLOCK), 0) & (KEY_BLOCK - 1)
    vis = col < row
    suffix = _suffix_matrix()

    def kv_blocks(js):
        starts = [pl.multiple_of(j * KEY_BLOCK, KEY_BLOCK) for j in js]
        return ([k_ref[0, pl.ds(s, KEY_BLOCK), :].astype(BF16) for s in starts],
                [v_ref[0, pl.ds(s, KEY_BLOCK), :].astype(BF16) for s in starts])

    def mask_finished(ll, off):
        return jnp.concatenate([jnp.where(g0 + qi - off >= 0, ll[qi * r:(qi + 1) * r], -jnp.inf)
                                for qi in range(nq)], axis=0)

    def any_live(ll, off):
        m = jnp.max(mask_finished(ll, off), axis=0, keepdims=True)
        return (m[0, 0] > LOG2_DEAD).astype(jnp.int32)

    qss = [_stack_heads(q_ref[0, qi * KEY_BLOCK:(qi + 1) * KEY_BLOCK, :]) for qi in range(nq)]
    kbs, vbs = kv_blocks([g0 + qi for qi in range(nq)])
    acc, ll = _sb_blocks(qss, kbs, vbs, jnp.zeros((nq * r, LANES), F32), jnp.zeros((nq * r, 1), F32), vis,
                         suffix)

    def cond(c):
        return c[3] > 0

    def body(c):
        off, acc_, ll_, _ = c
        kbs_, vbs_ = kv_blocks([jnp.maximum(g0 + qi - off, 0) for qi in range(nq)])
        acc_, ll_ = _sb_blocks(qss, kbs_, vbs_, acc_, mask_finished(ll_, off), None, suffix)
        return off + 1, acc_, ll_, any_live(ll_, off + 1)

    one = jnp.int32(1)
    _, acc, _, _ = lax.while_loop(cond, body, (one, acc, ll, any_live(ll, one)))
    for qi in range(nq):
        o_ref[0, qi * KEY_BLOCK:(qi + 1) * KEY_BLOCK, :] = _unstack_heads(acc[qi * r:(qi + 1) * r]).astype(BF16)


def _attn_prompt(q, k, v, tq):
    bsz, t, d = q.shape
    pairs = d // LANES
    return pl.pallas_call(
        _attn_prompt_body,
        grid=(bsz, pairs, t // tq),
        in_specs=[
            pl.BlockSpec((1, tq, LANES), lambda b, p, i: (b, i, p)),
            pl.BlockSpec((1, t, LANES), lambda b, p, i: (b, 0, p)),
            pl.BlockSpec((1, t, LANES), lambda b, p, i: (b, 0, p)),
        ],
        out_specs=pl.BlockSpec((1, tq, LANES), lambda b, p, i: (b, i, p)),
        out_shape=jax.ShapeDtypeStruct((bsz, t, d), BF16),
        compiler_params=_params(("arbitrary", "arbitrary", "arbitrary")),
        name="attn_prompt",
    )(q, k, v)


def _attn_sample_body(q_ref, kn_ref, vn_ref, ck_ref, cv_ref, o_ref):
    tq = q_ref.shape[1]
    past_blocks = ck_ref.shape[1] // KEY_BLOCK
    qs = _stack_heads(q_ref[0])
    row = _row_in_head(2 * tq, KEY_BLOCK)
    col = lax.broadcasted_iota(jnp.int32, (2 * tq, KEY_BLOCK), 1)
    acc = jnp.zeros((2 * tq, LANES), F32)
    ll = jnp.zeros((2 * tq, 1), F32)
    suffix = _suffix_matrix()
    acc, ll = _sb_block(qs, kn_ref[0].astype(BF16), vn_ref[0].astype(BF16), acc, ll, col < row, suffix)
    acc = _sb_older_blocks(qs, ck_ref, cv_ref, past_blocks - 1, acc, ll, suffix)
    o_ref[0] = _unstack_heads(acc).astype(BF16)


def _attn_sample(q, k_new, v_new, cache_k, cache_v):
    bsz, t, d = q.shape
    past = cache_k.shape[1]
    pairs = d // LANES
    return pl.pallas_call(
        _attn_sample_body,
        grid=(bsz, pairs),
        in_specs=[
            pl.BlockSpec((1, t, LANES), lambda b, p: (b, 0, p)),
            pl.BlockSpec((1, KEY_BLOCK, LANES), lambda b, p: (b, 0, p)),
            pl.BlockSpec((1, KEY_BLOCK, LANES), lambda b, p: (b, 0, p)),
            pl.BlockSpec((1, past, LANES), lambda b, p: (b, 0, p)),
            pl.BlockSpec((1, past, LANES), lambda b, p: (b, 0, p)),
        ],
        out_specs=pl.BlockSpec((1, t, LANES), lambda b, p: (b, 0, p)),
        out_shape=jax.ShapeDtypeStruct((bsz, t, d), BF16),
        compiler_params=_params(("arbitrary", "arbitrary")),
        name="attn_sample",
    )(q, k_new, v_new, cache_k, cache_v)


def _first_argmax(cur, iota_f, n):
    m = jnp.max(cur, axis=0, keepdims=True)
    i = jnp.min(jnp.where(cur == m, iota_f, float(n)), axis=0, keepdims=True)
    return m, i


def _store_row_tiles(ref, val):
    for s in range(val.shape[1] // LANES):
        ref[:, s, :] = val[:, s * LANES:(s + 1) * LANES]


def _post_attn_body(o_ref, gc_ref, sa_ref, x_ref, cnt_ref, lng_ref, lnb_ref, wao_ref, wo_ref, l1g_ref,
                    l1b_ref, wrh_ref, wrl_ref, rb_ref, wsgu_ref, wsd_ref,
                    x1_ref, part_ref, idx_ref, wts_ref, rank_ref, cnt_out_ref, run):
    tm = x_ref.shape[0]
    ff = wsd_ref.shape[0]

    @pl.when(pl.program_id(0) == 0)
    def _():
        run[...] = cnt_ref[...]

    h = _layer_norm(x_ref[...], lng_ref[...], lnb_ref[...])
    att = _dot(o_ref[...], wao_ref[...])
    merged = gc_ref[...].astype(F32) + sa_ref[...].astype(F32) * att
    mixed = _dot(merged.astype(BF16), wo_ref[...])
    x1 = _layer_norm(DEEPNORM_ALPHA * h + mixed, l1g_ref[...], l1b_ref[...])
    _store_row_tiles(x1_ref, x1)

    x1h = x1.astype(BF16)
    gu = _dot(x1h, wsgu_ref[...])
    shared = _dot((_silu(gu[:, :ff]) * gu[:, ff:]).astype(BF16), wsd_ref[...])
    _store_row_tiles(part_ref, DEEPNORM_ALPHA * x1 + shared)

    x1l = (x1 - x1h.astype(F32)).astype(BF16)
    logits = _dot_nt(wrh_ref[...], x1h) + _dot_nt(wrl_ref[...], x1h) + _dot_nt(wrh_ref[...], x1l)
    scores = _sigmoid(logits)
    choice = scores + rb_ref[...]
    neg = -jnp.inf

    giota = lax.broadcasted_iota(jnp.int32, (GROUP_SIZE, tm), 0).astype(F32)
    gs = []
    for g in range(N_GROUPS):
        blk = choice[g * GROUP_SIZE:(g + 1) * GROUP_SIZE, :]
        m1, i1 = _first_argmax(blk, giota, GROUP_SIZE)
        m2 = jnp.max(jnp.where(giota == i1, neg, blk), axis=0, keepdims=True)
        gs.append(m1 + m2)
    gscore = jnp.concatenate(gs, axis=0)

    g8 = lax.broadcasted_iota(jnp.int32, (N_GROUPS, tm), 0).astype(F32)
    gsel = jnp.zeros((N_GROUPS, tm), F32)
    cur = gscore
    for _ in range(TOPK_GROUPS):
        _, i = _first_argmax(cur, g8, N_GROUPS)
        hit = g8 == i
        gsel = jnp.where(hit, 1.0, gsel)
        cur = jnp.where(hit, neg, cur)
    emask = jnp.concatenate(
        [jnp.broadcast_to(gsel[g:g + 1, :], (GROUP_SIZE, tm)) for g in range(N_GROUPS)], axis=0)

    eiota = lax.broadcasted_iota(jnp.int32, (N_EXPERTS, tm), 0).astype(F32)
    cur = jnp.where(emask > 0.0, choice, neg)
    hits, sel_w = [], []
    for r in range(TOP_K):
        _, i = _first_argmax(cur, eiota, N_EXPERTS)
        hit = eiota == i
        hits.append(hit)
        sel_w.append(jnp.sum(jnp.where(hit, scores, 0.0), axis=0, keepdims=True))
        cur = jnp.where(hit, neg, cur)
        idx_ref[r:r + 1, :] = i.astype(jnp.int32)
    wsum = sel_w[0]
    for r in range(1, TOP_K):
        wsum = wsum + sel_w[r]
    for r in range(TOP_K):
        wts_ref[r:r + 1, :] = sel_w[r] / wsum * ROUTED_SCALE

    picked = jnp.zeros((N_EXPERTS, tm), F32)
    for r in range(TOP_K):
        picked = jnp.where(hits[r], 1.0, picked)
    picked_b = picked.astype(BF16)
    trow = lax.broadcasted_iota(jnp.int32, (tm, tm), 0)
    tcol = lax.broadcasted_iota(jnp.int32, (tm, tm), 1)
    earlier = jnp.where(trow < tcol, 1.0, 0.0).astype(BF16)
    before = _dot(picked_b, earlier) + run[:, 0:1]
    for r in range(TOP_K):
        rank_ref[r:r + 1, :] = jnp.sum(jnp.where(hits[r], before, 0.0), axis=0, keepdims=True).astype(jnp.int32)
    run[...] = run[...] + _dot(picked_b, jnp.ones((tm, LANES), BF16))
    cnt_out_ref[...] = run[...]


def _post_attn(o, gc, sa, x, cnt_in, ln_g, ln_b, w_ao, w_o, l1g, l1b, wr_hi, wr_lo, r_bias, ws_gu, ws_d, tm):
    n, d = x.shape
    ff = ws_d.shape[0]
    const = lambda shape: pl.BlockSpec(shape, lambda i: (0,) * len(shape))
    tile = lambda: pl.BlockSpec((tm, d), lambda i: (i, 0))
    small = lambda: pl.BlockSpec((TOP_K, tm), lambda i: (0, i))
    row_tiles = lambda: pl.BlockSpec((tm, d // LANES, LANES), lambda i: (i, 0, 0))
    return pl.pallas_call(
        _post_attn_body,
        grid=(n // tm,),
        in_specs=[tile(), tile(), tile(), tile(), const((N_EXPERTS, LANES)),
                  const((1, d)), const((1, d)), const((d, d)), const((d, d)), const((1, d)), const((1, d)),
                  const((N_EXPERTS, d)), const((N_EXPERTS, d)), const((N_EXPERTS, 1)),
                  const((d, 2 * ff)), const((ff, d))],
        out_specs=[row_tiles(), row_tiles(), small(), small(), small(), const((N_EXPERTS, LANES))],
        out_shape=[
            jax.ShapeDtypeStruct((n, d // LANES, LANES), F32),
            jax.ShapeDtypeStruct((n, d // LANES, LANES), F32),
            jax.ShapeDtypeStruct((TOP_K, n), jnp.int32),
            jax.ShapeDtypeStruct((TOP_K, n), F32),
            jax.ShapeDtypeStruct((TOP_K, n), jnp.int32),
            jax.ShapeDtypeStruct((N_EXPERTS, LANES), F32),
        ],
        scratch_shapes=[pltpu.VMEM((N_EXPERTS, LANES), F32)],
        compiler_params=_params(("arbitrary",)),
        name="post_attn",
    )(o, gc, sa, x, cnt_in, ln_g, ln_b, w_ao, w_o, l1g, l1b, wr_hi, wr_lo, r_bias, ws_gu, ws_d)


def _dest_body(idx_ref, rank_ref, rs_ref, dest_ref):
    tm = idx_ref.shape[1]
    eiota = lax.broadcasted_iota(jnp.int32, (N_EXPERTS, tm), 0)
    rs = rs_ref[...]
    for r in range(TOP_K):
        base = jnp.sum(jnp.where(eiota == idx_ref[r:r + 1, :], rs, 0.0), axis=0, keepdims=True)
        dest_ref[r:r + 1, :] = base.astype(jnp.int32) + rank_ref[r:r + 1, :]


def _dest(idx, rank, row_start_f, tm):
    n = idx.shape[1]
    small = lambda: pl.BlockSpec((TOP_K, tm), lambda i: (0, i))
    return pl.pallas_call(
        _dest_body,
        grid=(n // tm,),
        in_specs=[small(), small(), pl.BlockSpec((N_EXPERTS, 1), lambda i: (0, 0))],
        out_specs=small(),
        out_shape=jax.ShapeDtypeStruct((TOP_K, n), jnp.int32),
        compiler_params=_params(("arbitrary",)),
        name="dest",
    )(idx, rank, row_start_f)


def _slot_index_copies(src_hbm, col0, width, dst_smem, s, sem):
    w = dst_smem.shape[1] // TOP_K
    return [pltpu.make_async_copy(src_hbm.at[pl.ds(j, 1), pl.ds(col0 + q * LANES, LANES)],
                                  dst_smem.at[s, pl.ds(j * w + q, 1), :], sem)
            for j in range(TOP_K) for q in range(width // LANES)]


def _dispatch_body(meta_ref, lc_ref, dp_hbm, ds_hbm, xp_ref, xsm_ref, xs_hbm, idx_smem, zbuf, isem, ssem, zsem):
    i = pl.program_id(0)
    n = pl.num_programs(0)
    tm = xp_ref.shape[0]
    n_s = xsm_ref.shape[0]
    total_chunks = xs_hbm.shape[0] // GATHER_CHUNK

    def idx_copies(tile, fn):
        @pl.when(tile < n - 1)
        def _():
            for c in _slot_index_copies(dp_hbm, tile * tm, tm, idx_smem, 0, isem):
                fn(c)

        @pl.when(tile == n - 1)
        def _():
            for c in _slot_index_copies(ds_hbm, 0, n_s, idx_smem, 0, isem):
                fn(c)

    @pl.when(i == 0)
    def _():
        idx_copies(0, lambda c: c.start())

    idx_copies(i, lambda c: c.wait())

    def scatter(src_ref):
        cnt = src_ref.shape[0]
        for j in range(TOP_K):
            for q in range(cnt // LANES):
                idx_row = idx_smem.at[0, j * (tm // LANES) + q]

                def body(p, carry, idx_row=idx_row, q=q):
                    for u in range(DMA_THREADS):
                        t = DMA_THREADS * p + u
                        row = idx_row[t]
                        pltpu.make_async_copy(src_ref.at[q * LANES + t], xs_hbm.at[row], ssem).start(priority=u)
                    return carry
                lax.fori_loop(0, LANES // DMA_THREADS, body, 0, unroll=4)

        @pl.when(i + 1 < n)
        def _():
            idx_copies(i + 1, lambda c: c.start())

        for j in range(TOP_K):
            pltpu.make_async_copy(src_ref, xs_hbm.at[pl.ds(0, cnt)], ssem).wait()

    @pl.when(i == 0)
    def _():
        zbuf[...] = jnp.zeros(zbuf.shape, F32)

        def zero_chunk(c):
            return pltpu.make_async_copy(zbuf, xs_hbm.at[pl.ds(c * GATHER_CHUNK, GATHER_CHUNK)], zsem)

        def each_expert(fn):
            def body(e, carry):
                @pl.when(lc_ref[e] >= 0)
                def _():
                    fn(lc_ref[e])
                return carry
            lax.fori_loop(0, N_EXPERTS, body, 0)

        def each_tail(fn):
            def body(c, carry):
                fn(c)
                return carry
            lax.fori_loop(meta_ref[0], total_chunks, body, 0)

        each_expert(lambda c: zero_chunk(c).start())
        each_tail(lambda c: zero_chunk(c).start())
        each_expert(lambda c: zero_chunk(c).wait())
        each_tail(lambda c: zero_chunk(c).wait())

    @pl.when(i < n - 1)
    def _():
        scatter(xp_ref)

    @pl.when(i == n - 1)
    def _():
        scatter(xsm_ref)


def _dispatch(meta, last_chunk, dest_p, dest_s, x1p, x1s, n_rows, tm):
    n_p, sub, lanes = x1p.shape
    n_s = x1s.shape[0]
    n_pt = n_p // tm
    grid_spec = pltpu.PrefetchScalarGridSpec(
        num_scalar_prefetch=2,
        grid=(n_pt + 1,),
        in_specs=[
            pl.BlockSpec(memory_space=pl.ANY),
            pl.BlockSpec(memory_space=pl.ANY),
            pl.BlockSpec((tm, sub, lanes), lambda i, *_: (jnp.minimum(i, n_pt - 1), 0, 0)),
            pl.BlockSpec((n_s, sub, lanes), lambda i, *_: (0, 0, 0)),
        ],
        out_specs=pl.BlockSpec(memory_space=pl.ANY),
        scratch_shapes=[
            pltpu.SMEM((1, TOP_K * tm // LANES, LANES), jnp.int32),
            pltpu.VMEM((GATHER_CHUNK, sub, lanes), F32),
            pltpu.SemaphoreType.DMA(()),
            pltpu.SemaphoreType.DMA(()),
            pltpu.SemaphoreType.DMA(()),
        ],
    )
    return pl.pallas_call(
        _dispatch_body,
        grid_spec=grid_spec,
        out_shape=jax.ShapeDtypeStruct((n_rows, sub, lanes), F32),
        compiler_params=_dma_params(("arbitrary",)),
        name="dispatch",
    )(meta, last_chunk, dest_p, dest_s, x1p, x1s)


def _experts_body(be_ref, bx_ref, bn_ref, bz_ref, xs_hbm, wg_ref, wu_ref, wd_ref, ys_hbm,
                  xbuf, obuf, gsem, osem, wgu_bf, wd_bf):
    b = pl.program_id(0)
    nb = pl.num_programs(0)
    slot = b % 2
    ff = wd_ref.shape[1]

    def in_copies(blk, s, nch):
        rows = pl.ds(bx_ref[blk] * GATHER_CHUNK, nch * GATHER_CHUNK)
        return [pltpu.make_async_copy(xs_hbm.at[rows, c, :],
                                      xbuf.at[s, pl.ds(0, nch * GATHER_CHUNK), pl.ds(c * LANES, LANES)],
                                      gsem.at[s]) for c in range(xs_hbm.shape[1])]

    def out_copies(blk, s, nch):
        rows = pl.ds(bx_ref[blk] * GATHER_CHUNK, nch * GATHER_CHUNK)
        return [pltpu.make_async_copy(obuf.at[s, pl.ds(0, nch * GATHER_CHUNK), pl.ds(c * LANES, LANES)],
                                      ys_hbm.at[rows, c, :], osem.at[s]) for c in range(ys_hbm.shape[1])]

    def for_block(count, copies, fn):
        for nch in range(1, BLOCK_CHUNKS + 1):
            @pl.when(count == nch)
            def _():
                for c in copies(nch):
                    fn(c)

    def n_in(blk):
        return jnp.where(bz_ref[blk] == 0, bn_ref[blk], 0)

    @pl.when(b == 0)
    def _():
        for_block(n_in(0), lambda k: in_copies(0, 0, k), lambda c: c.start())

    @pl.when(b + 1 < nb)
    def _():
        for_block(n_in(b + 1), lambda k: in_copies(b + 1, 1 - slot, k), lambda c: c.start())

    @pl.when(b >= 2)
    def _():
        for_block(bn_ref[b - 2], lambda k: out_copies(b - 2, slot, k), lambda c: c.wait())

    for_block(n_in(b), lambda k: in_copies(b, slot, k), lambda c: c.wait())

    e_now = be_ref[b]
    e_prev = be_ref[jnp.maximum(b - 1, 0)]

    @pl.when(jnp.logical_or(b == 0, e_now != e_prev))
    def _():
        wgu_bf[:, 0:ff] = wg_ref[0].astype(BF16)
        wgu_bf[:, ff:2 * ff] = wu_ref[0].astype(BF16)
        wd_bf[...] = wd_ref[0].astype(BF16)

    for nch in range(1, BLOCK_CHUNKS + 1):
        @pl.when(n_in(b) == nch)
        def _():
            rows = nch * GATHER_CHUNK
            gu = _dot(xbuf[slot, 0:rows, :].astype(BF16), wgu_bf[...])
            hid = _silu(gu[:, :ff]) * gu[:, ff:]
            obuf[slot, 0:rows, :] = _dot(hid.astype(BF16), wd_bf[...])

    @pl.when(bz_ref[b] != 0)
    def _():
        obuf[slot] = jnp.zeros(obuf.shape[1:], F32)

    for_block(bn_ref[b], lambda k: out_copies(b, slot, k), lambda c: c.start())

    @pl.when(b == nb - 1)
    def _():
        for_block(bn_ref[b], lambda k: out_copies(b, slot, k), lambda c: c.wait())

        @pl.when(nb > 1)
        def _():
            for_block(bn_ref[b - 1], lambda k: out_copies(b - 1, 1 - slot, k), lambda c: c.wait())


def _experts(block_e, block_x, block_n, block_z, xs, we_gate, we_up, we_down):
    nb = block_e.shape[0]
    n_rows = xs.shape[0]
    d = we_gate.shape[1]
    ff = we_gate.shape[2]
    rows = BLOCK_CHUNKS * GATHER_CHUNK
    grid_spec = pltpu.PrefetchScalarGridSpec(
        num_scalar_prefetch=4,
        grid=(nb,),
        in_specs=[
            pl.BlockSpec(memory_space=pl.ANY),
            pl.BlockSpec((1, d, ff), lambda b, be, *_: (be[b], 0, 0)),
            pl.BlockSpec((1, d, ff), lambda b, be, *_: (be[b], 0, 0)),
            pl.BlockSpec((1, ff, d), lambda b, be, *_: (be[b], 0, 0)),
        ],
        out_specs=pl.BlockSpec(memory_space=pl.ANY),
        scratch_shapes=[
            pltpu.VMEM((2, rows, d), F32),
            pltpu.VMEM((2, rows, d), F32),
            pltpu.SemaphoreType.DMA((2,)),
            pltpu.SemaphoreType.DMA((2,)),
            pltpu.VMEM((d, 2 * ff), BF16),
            pltpu.VMEM((ff, d), BF16),
        ],
    )
    return pl.pallas_call(
        _experts_body,
        grid_spec=grid_spec,
        out_shape=jax.ShapeDtypeStruct(xs.shape, F32),
        compiler_params=_dma_params(("arbitrary",)),
        name="experts",
    )(block_e, block_x, block_n, block_z, xs, we_gate, we_up, we_down)


def _combine_body(dt_hbm, wt_hbm, ys_hbm, part_ref, g_ref, b_ref, y_hbm, idx_smem, w_smem, gbuf, ybuf, isem, wsem,
                  gsem, osem):
    i = pl.program_id(0)
    n = pl.num_programs(0)
    tm = gbuf.shape[2]
    per_step = part_ref.shape[0] // tm
    more = i + 1 < n

    def idx_copies(tile, s):
        return _slot_index_copies(dt_hbm, tile * tm, tm, idx_smem, s, isem.at[s])

    def w_copies(tile, s):
        return _slot_index_copies(wt_hbm, tile * tm, tm, w_smem, s, wsem.at[s])

    def start(copies):
        for c in copies:
            c.start()

    def wait(copies):
        for c in copies:
            c.wait()

    def issue_rows(s):
        for j in range(TOP_K):
            def body(p, carry):
                for u in range(DMA_THREADS):
                    t = DMA_THREADS * p + u
                    row = idx_smem.at[s, j][t]
                    pltpu.make_async_copy(ys_hbm.at[row], gbuf.at[s, j, t], gsem.at[s]).start(priority=u)
                return carry
            lax.fori_loop(0, tm // DMA_THREADS, body, 0, unroll=4)

    def out_copies(s, tile):
        return [pltpu.make_async_copy(ybuf.at[s, pl.ds(0, tm), c, :],
                                      y_hbm.at[pl.ds(tile * tm, tm), pl.ds(c * LANES, LANES)], osem.at[s])
                for c in range(ybuf.shape[2])]

    def finish_tile(s):
        for j in range(TOP_K):
            pltpu.make_async_copy(ys_hbm.at[pl.ds(0, tm)], gbuf.at[s, j], gsem.at[s]).wait()
        wait(w_copies(0, s))

        @pl.when(i > 0)
        def _():
            wait(out_copies(s, 0))
        base = s * tm

        def token(t, carry):
            f = part_ref[base + t]
            for j in range(TOP_K):
                f = f + w_smem.at[s, j][t] * gbuf[s, j, t]
            ybuf[s, t] = f
            return carry
        lax.fori_loop(0, tm, token, 0, unroll=8)

        f = ybuf[s]
        inv_d = 1.0 / (f.shape[1] * f.shape[2])
        tile_sum = lambda a: jnp.sum(jnp.sum(a, axis=2, keepdims=True), axis=1, keepdims=True)
        fc = f - tile_sum(f) * inv_d
        var = tile_sum(fc * fc) * inv_d
        ybuf[s] = fc * lax.rsqrt(var + LN_EPS) * g_ref[...] + b_ref[...]
        start(out_copies(s, first + s))

    first = i * per_step

    @pl.when(i == 0)
    def _():
        start(idx_copies(0, 0))
        start(w_copies(0, 0))
        wait(idx_copies(0, 0))
        issue_rows(0)
        if per_step == 2:
            start(idx_copies(1, 1))
            start(w_copies(1, 1))

    if per_step == 1:
        finish_tile(0)

        @pl.when(more)
        def _():
            start(idx_copies(first + 1, 0))
            start(w_copies(first + 1, 0))
            wait(idx_copies(first + 1, 0))
            issue_rows(0)
    else:
        wait(idx_copies(first + 1, 1))
        issue_rows(1)

        @pl.when(more)
        def _():
            start(idx_copies(first + 2, 0))

        finish_tile(0)

        @pl.when(more)
        def _():
            start(w_copies(first + 2, 0))
            wait(idx_copies(first + 2, 0))
            issue_rows(0)
            start(idx_copies(first + 3, 1))

        finish_tile(1)

        @pl.when(more)
        def _():
            start(w_copies(first + 3, 1))

    @pl.when(jnp.logical_not(more))
    def _():
        for s in range(per_step):
            wait(out_copies(s, 0))


def _combine(dest, wts, ys, part, ln_g, ln_b, tm, per_step):
    n, sub, lanes = part.shape
    assert tm == LANES
    rows = tm * per_step
    return pl.pallas_call(
        _combine_body,
        grid=(n // rows,),
        in_specs=[
            pl.BlockSpec(memory_space=pl.ANY),
            pl.BlockSpec(memory_space=pl.ANY),
            pl.BlockSpec(memory_space=pl.ANY),
            pl.BlockSpec((rows, sub, lanes), lambda i: (i, 0, 0)),
            pl.BlockSpec((sub, lanes), lambda i: (0, 0)),
            pl.BlockSpec((sub, lanes), lambda i: (0, 0)),
        ],
        out_specs=pl.BlockSpec(memory_space=pl.ANY),
        out_shape=jax.ShapeDtypeStruct((n, sub * lanes), F32),
        scratch_shapes=[
            pltpu.SMEM((2, TOP_K, LANES), jnp.int32),
            pltpu.SMEM((2, TOP_K, LANES), F32),
            pltpu.VMEM((2, TOP_K, tm, sub, lanes), F32),
            pltpu.VMEM((2, tm, sub, lanes), F32),
            pltpu.SemaphoreType.DMA((2,)),
            pltpu.SemaphoreType.DMA((2,)),
            pltpu.SemaphoreType.DMA((2,)),
            pltpu.SemaphoreType.DMA((2,)),
        ],
        compiler_params=_dma_params(("arbitrary",)),
        name="combine",
    )(dest, wts, ys, part, ln_g.reshape(sub, lanes), ln_b.reshape(sub, lanes))


def kernel(x_prompt, x_sample, cache_k, cache_v, state_conv, ln_in_g, ln_in_b, w_in, b_in, conv_w, conv_b,
           conv_ln_g, conv_ln_b, w_conv_out, w_attn_out, w_out, ln1_g, ln1_b, w_router, router_bias,
           we_gate, we_up, we_down, ws_gate, ws_up, ws_down, ln2_g, ln2_b):
    bp, tp, d = x_prompt.shape
    bs, ts, _ = x_sample.shape
    past = cache_k.shape[2]
    assert w_in.shape[0] == DEPTH and d == N_HEADS * HEAD_DIM and conv_w.shape[2] == d
    assert tp % 1024 == 0 and ts <= KEY_BLOCK and ts % 16 == 0 and past % KEY_BLOCK == 0
    n_p, n_s = bp * tp, bs * ts
    assert n_s % LANES == 0

    row = lambda a: a.reshape(1, -1)
    lng, lnb = row(ln_in_g), row(ln_in_b)
    w_in_b = w_in[0].astype(BF16)
    w_co_b = w_conv_out[0].astype(BF16)
    conv_args = (lng, lnb, w_in_b, b_in, conv_w[0], conv_b, conv_ln_g, conv_ln_b, w_co_b)

    qp, kp, vp, gcp, sap, csp = _proj_conv(x_prompt, jnp.zeros((bp, CONV_WIDTH - 1, d), F32), *conv_args, tm=256)
    op = _attn_prompt(qp, kp, vp, tq=1024)

    qs, ks, vs, gcs, sas, css = _proj_conv(x_sample, state_conv[0], *conv_args, tm=ts)
    pad_new = lambda a: jnp.pad(a, ((0, 0), (0, KEY_BLOCK - ts), (0, 0)))
    os_ = _attn_sample(qs, pad_new(ks), pad_new(vs), cache_k[0].reshape(bs, past, d), cache_v[0].reshape(bs, past, d))

    wr = w_router[0].T
    wr_hi = wr.astype(BF16)
    wr_lo = (wr - wr_hi.astype(F32)).astype(BF16)
    post_args = (lng, lnb, w_attn_out[0].astype(BF16), w_out[0].astype(BF16), ln1_g, ln1_b, wr_hi, wr_lo,
                 router_bias.reshape(N_EXPERTS, 1),
                 jnp.concatenate([ws_gate[0], ws_up[0]], axis=1).astype(BF16), ws_down[0].astype(BF16))
    flat = lambda a, n: a.reshape(n, d)
    zero_cnt = jnp.zeros((N_EXPERTS, LANES), F32)
    x1p, partp, idxp, wtp, rankp, cnt_p = _post_attn(
        flat(op, n_p), flat(gcp, n_p), flat(sap, n_p), flat(x_prompt, n_p), zero_cnt, *post_args, tm=512)
    x1s, parts, idxs, wts, ranks, cnt = _post_attn(
        flat(os_, n_s), flat(gcs, n_s), flat(sas, n_s), flat(x_sample, n_s), cnt_p, *post_args, tm=n_s)

    i32 = jnp.int32
    n_tok = n_p + n_s
    total_chunks = (n_tok * TOP_K + N_EXPERTS * (GATHER_CHUNK - 1) + GATHER_CHUNK - 1) // GATHER_CHUNK
    n_blocks = total_chunks // BLOCK_CHUNKS + N_EXPERTS
    counts = cnt[:, 0].astype(i32)
    chunks = (counts + GATHER_CHUNK - 1) // GATHER_CHUNK
    chunk_end = jnp.cumsum(chunks)
    chunk_start = chunk_end - chunks
    row_start = chunk_start * GATHER_CHUNK
    used_chunks = chunk_end[-1]
    blocks = (chunks + BLOCK_CHUNKS - 1) // BLOCK_CHUNKS
    blk_end = jnp.cumsum(blocks)
    blk_start = blk_end - blocks
    bid = jnp.arange(n_blocks, dtype=i32)
    block_e = jnp.minimum(jnp.sum(bid[:, None] >= blk_end[None, :], axis=1), N_EXPERTS - 1).astype(i32)
    k_in_e = bid - blk_start[block_e]
    is_tail = bid >= blk_end[-1]
    tail_x = used_chunks + BLOCK_CHUNKS * (bid - blk_end[-1])
    block_x = jnp.where(is_tail, tail_x, chunk_start[block_e] + BLOCK_CHUNKS * k_in_e)
    block_n = jnp.where(is_tail, total_chunks - tail_x, chunks[block_e] - BLOCK_CHUNKS * k_in_e)
    block_n = jnp.clip(block_n, 0, BLOCK_CHUNKS).astype(i32)
    block_x = jnp.where(block_n > 0, block_x, 0).astype(i32)
    block_z = is_tail.astype(i32)
    meta = jnp.stack([used_chunks, used_chunks]).astype(i32)
    last_chunk = jnp.where(chunks > 0, chunk_end - 1, -1).astype(i32)

    rs_f = row_start.astype(F32).reshape(N_EXPERTS, 1)
    dest_p = _dest(idxp, rankp, rs_f, tm=1024)
    dest_s = _dest(idxs, ranks, rs_f, tm=n_s)

    xs = _dispatch(meta, last_chunk, dest_p, dest_s, x1p, x1s, total_chunks * GATHER_CHUNK, tm=1024)
    ys = _experts(block_e, block_x, block_n, block_z, xs, we_gate[0], we_up[0], we_down[0])

    yp = _combine(dest_p, wtp, ys, partp, ln2_g, ln2_b, tm=128, per_step=2)
    ysm = _combine(dest_s, wts, ys, parts, ln2_g, ln2_b, tm=n_s, per_step=1)

    heads = lambda a, b, t: a.reshape(1, b, t, N_HEADS, HEAD_DIM)
    return (yp.reshape(bp, tp, d), ysm.reshape(bs, ts, d),
            heads(kp, bp, tp), heads(vp, bp, tp), csp[None],
            heads(ks, bs, ts), heads(vs, bs, ts), css[None])
```

```python
import jax
import jax.numpy as jnp
from jax import lax
from jax.experimental import pallas as pl
from jax.experimental.pallas import tpu as pltpu

F32 = jnp.float32
BF16 = jnp.bfloat16

N_HEADS = 16
HEAD_DIM = 64
CONV_WIDTH = 31
N_EXPERTS = 256
TOP_K = 8
N_GROUPS = 8
TOPK_GROUPS = 4
GROUP_SIZE = N_EXPERTS // N_GROUPS
ROUTED_SCALE = 2.5
LN_EPS = 1e-5
DEPTH = 1
DEEPNORM_ALPHA = (2 * DEPTH) ** 0.25
ATTN_SCALE = HEAD_DIM ** -0.5

LANES = 128
SUBLANES = 8
VMEM_LIMIT_BYTES = 56 * 1024 * 1024

KEY_BLOCK = 128
HIST_PAD = 32
GATHER_CHUNK = 128
BLOCK_CHUNKS = 8
DMA_THREADS = 2
LOG2E = 1.4426950408889634
LOG2_DEAD = -110.0 * LOG2E


def _layer_norm(x, g, b):
    mu = jnp.mean(x, axis=-1, keepdims=True)
    xc = x - mu
    var = jnp.mean(xc * xc, axis=-1, keepdims=True)
    return xc * lax.rsqrt(var + LN_EPS) * g + b


def _sigmoid(x):
    return 1.0 / (1.0 + jnp.exp(-x))


def _silu(x):
    return x * _sigmoid(x)


def _dot(a, b):
    return jnp.dot(a, b, preferred_element_type=F32)


def _dot_nt(a, b):
    return lax.dot_general(a, b, (((1,), (1,)), ((), ())), preferred_element_type=F32)


def _params(sem):
    return pltpu.CompilerParams(dimension_semantics=sem, vmem_limit_bytes=VMEM_LIMIT_BYTES)


def _dma_params(sem):
    return pltpu.CompilerParams(dimension_semantics=sem, vmem_limit_bytes=VMEM_LIMIT_BYTES,
                                disable_bounds_checks=True)


def _proj_conv_body(x_ref, hist_ref, lng_ref, lnb_ref, win_ref, bin_ref, cw_ref, cb_ref, clg_ref,
                    clb_ref, wco_ref, q_ref, k_ref, v_ref, gc_ref, sa_ref, cs_ref, ubuf, cbuf, ush):
    tm = x_ref.shape[1]
    d = x_ref.shape[2]
    hist = CONV_WIDTH - 1
    lead = HIST_PAD - hist

    @pl.when(pl.program_id(1) == 0)
    def _():
        ubuf[0:lead, :] = jnp.zeros((lead, d), F32)
        ubuf[lead:HIST_PAD, :] = hist_ref[0]

    hb = _layer_norm(x_ref[0], lng_ref[...], lnb_ref[...]).astype(BF16)

    def proj(i):
        return _dot(hb, win_ref[:, i * d:(i + 1) * d]) + bin_ref[:, i * d:(i + 1) * d]

    ubuf[HIST_PAD:HIST_PAD + tm, :] = proj(0) * _sigmoid(proj(1))
    q_ref[0] = (proj(2) * (ATTN_SCALE * LOG2E)).astype(BF16)
    k_ref[0] = proj(3)
    v_ref[0] = proj(4)
    sa_ref[0] = _sigmoid(proj(6)).astype(BF16)

    span = tm + HIST_PAD - SUBLANES
    for ph in range(1, SUBLANES):
        ush[ph - 1, 0:span, :] = ubuf[ph:ph + span, :]
    rows = min(tm, 32)
    for r0 in range(0, tm, rows):
        acc = jnp.broadcast_to(cb_ref[...], (rows, d))
        for kk in range(CONV_WIDTH):
            base, ph = divmod(lead + kk, SUBLANES)
            lo = base * SUBLANES + r0
            tap = ubuf[lo:lo + rows, :] if ph == 0 else ush[ph - 1, lo:lo + rows, :]
            acc = acc + cw_ref[kk:kk + 1, :] * tap
        cbuf[r0:r0 + rows, :] = _silu(_layer_norm(acc, clg_ref[...], clb_ref[...])).astype(BF16)
    gc_ref[0] = (_sigmoid(proj(5)) * _dot(cbuf[...], wco_ref[...])).astype(BF16)

    cs_ref[0] = ubuf[tm + lead:tm + HIST_PAD, :]
    ubuf[0:HIST_PAD, :] = ubuf[tm:tm + HIST_PAD, :]


def _proj_conv(x, hist, ln_g, ln_b, w_in, b_in, conv_w, conv_b, cln_g, cln_b, w_co, tm):
    bsz, t, d = x.shape
    pw = w_in.shape[1]
    const = lambda shape: pl.BlockSpec(shape, lambda b, i: (0,) * len(shape))
    resident = lambda shape: pl.BlockSpec(shape, lambda b, i: (0,) * len(shape), pipeline_mode=pl.Buffered(1))
    tile = lambda: pl.BlockSpec((1, tm, d), lambda b, i: (b, i, 0))
    return pl.pallas_call(
        _proj_conv_body,
        grid=(bsz, t // tm),
        in_specs=[
            tile(),
            pl.BlockSpec((1, CONV_WIDTH - 1, d), lambda b, i: (b, 0, 0)),
            const((1, d)), const((1, d)), resident((d, pw)), const((1, pw)),
            const((CONV_WIDTH, d)), const((1, d)), const((1, d)), const((1, d)), resident((d, d)),
        ],
        out_specs=[tile(), tile(), tile(), tile(), tile(),
                   pl.BlockSpec((1, CONV_WIDTH - 1, d), lambda b, i: (b, 0, 0))],
        out_shape=[
            jax.ShapeDtypeStruct((bsz, t, d), BF16),
            jax.ShapeDtypeStruct((bsz, t, d), F32),
            jax.ShapeDtypeStruct((bsz, t, d), F32),
            jax.ShapeDtypeStruct((bsz, t, d), BF16),
            jax.ShapeDtypeStruct((bsz, t, d), BF16),
            jax.ShapeDtypeStruct((bsz, CONV_WIDTH - 1, d), F32),
        ],
        scratch_shapes=[pltpu.VMEM((HIST_PAD + tm, d), F32), pltpu.VMEM((tm, d), BF16),
                        pltpu.VMEM((SUBLANES - 1, tm + HIST_PAD - SUBLANES, d), F32)],
        compiler_params=_params(("arbitrary", "arbitrary")),
        name="proj_conv",
    )(x, hist, ln_g, ln_b, w_in, b_in, conv_w, conv_b, cln_g, cln_b, w_co)


def _suffix_matrix():
    row = lax.broadcasted_iota(jnp.int32, (KEY_BLOCK, KEY_BLOCK), 0)
    col = lax.broadcasted_iota(jnp.int32, (KEY_BLOCK, KEY_BLOCK), 1)
    return jnp.where(row >= col, 1.0, 0.0).astype(BF16)


def _sb_blocks(qss, kbs, vbs, acc, ll, vis, suffix):
    r = qss[0].shape[0]
    z = jnp.concatenate([_dot_nt(q, kb) for q, kb in zip(qss, kbs)], axis=0)
    lk = -(jnp.maximum(z, 0.0) + jnp.log2(1.0 + jnp.exp2(-jnp.abs(z))))
    if vis is not None:
        lk = jnp.where(vis, lk, 0.0)
    inb = _dot(lk.astype(BF16), suffix)
    arg = z + inb + ll
    if vis is not None:
        arg = jnp.where(vis, arg, -jnp.inf)
    a = jnp.exp2(arg).astype(BF16)
    av = jnp.concatenate([_dot(a[i * r:(i + 1) * r], vb) for i, vb in enumerate(vbs)], axis=0)
    return acc + av, ll + inb[:, 0:1]


def _sb_block(qs, kb, vb, acc, ll, vis, suffix):
    return _sb_blocks([qs], [kb], [vb], acc, ll, vis, suffix)


def _sb_older_blocks(qs, k_ref, v_ref, first, acc, ll, suffix):
    def live(ll_):
        return (jnp.max(ll_, axis=0, keepdims=True)[0, 0] > LOG2_DEAD).astype(jnp.int32)

    def cond(c):
        return jnp.logical_and(c[0] >= 0, c[3] > 0)

    def body(c):
        j, acc_, ll_, _ = c
        start = pl.multiple_of(j * KEY_BLOCK, KEY_BLOCK)
        kb = k_ref[0, pl.ds(start, KEY_BLOCK), :].astype(BF16)
        vb = v_ref[0, pl.ds(start, KEY_BLOCK), :].astype(BF16)
        acc_, ll_ = _sb_block(qs, kb, vb, acc_, ll_, None, suffix)
        return j - 1, acc_, ll_, live(ll_)

    _, acc, _, _ = lax.while_loop(cond, body, (first, acc, ll, live(ll)))
    return acc


def _stack_heads(q):
    lane = lax.broadcasted_iota(jnp.int32, q.shape, 1)
    zero = jnp.zeros_like(q)
    return jnp.concatenate([jnp.where(lane < HEAD_DIM, q, zero), jnp.where(lane >= HEAD_DIM, q, zero)], axis=0)


def _row_in_head(rows, cols):
    r = lax.broadcasted_iota(jnp.int32, (rows, cols), 0)
    return jnp.where(r >= rows // 2, r - rows // 2, r)


def _unstack_heads(acc):
    r = acc.shape[0] // 2
    lane = lax.broadcasted_iota(jnp.int32, (r, LANES), 1)
    return jnp.where(lane < HEAD_DIM, acc[:r], acc[r:])


def _attn_prompt_body(q_ref, k_ref, v_ref, o_ref):
    tq = q_ref.shape[1]
    nq = tq // KEY_BLOCK
    r = 2 * KEY_BLOCK
    g0 = pl.program_id(2) * nq
    col = lax.broadcasted_iota(jnp.int32, (nq * r, KEY_BLOCK), 1)
    row = lax.broadcasted_iota(jnp.int32, (nq * r, KEY_BLOCK), 0) & (KEY_BLOCK - 1)
    vis = col < row
    suffix = _suffix_matrix()

    def kv_blocks(js):
        starts = [pl.multiple_of(j * KEY_BLOCK, KEY_BLOCK) for j in js]
        return ([k_ref[0, pl.ds(s, KEY_BLOCK), :].astype(BF16) for s in starts],
                [v_ref[0, pl.ds(s, KEY_BLOCK), :].astype(BF16) for s in starts])

    def mask_finished(ll, off):
        return jnp.concatenate([jnp.where(g0 + qi - off >= 0, ll[qi * r:(qi + 1) * r], -jnp.inf)
                                for qi in range(nq)], axis=0)

    def any_live(ll, off):
        m = jnp.max(mask_finished(ll, off), axis=0, keepdims=True)
        return (m[0, 0] > LOG2_DEAD).astype(jnp.int32)

    qss = [_stack_heads(q_ref[0, qi * KEY_BLOCK:(qi + 1) * KEY_BLOCK, :]) for qi in range(nq)]
    kbs, vbs = kv_blocks([g0 + qi for qi in range(nq)])
    acc, ll = _sb_blocks(qss, kbs, vbs, jnp.zeros((nq * r, LANES), F32), jnp.zeros((nq * r, 1), F32), vis,
                         suffix)

    def cond(c):
        return c[3] > 0

    def body(c):
        off, acc_, ll_, _ = c
        kbs_, vbs_ = kv_blocks([jnp.maximum(g0 + qi - off, 0) for qi in range(nq)])
        acc_, ll_ = _sb_blocks(qss, kbs_, vbs_, acc_, mask_finished(ll_, off), None, suffix)
        return off + 1, acc_, ll_, any_live(ll_, off + 1)

    one = jnp.int32(1)
    _, acc, _, _ = lax.while_loop(cond, body, (one, acc, ll, any_live(ll, one)))
    for qi in range(nq):
        o_ref[0, qi * KEY_BLOCK:(qi + 1) * KEY_BLOCK, :] = _unstack_heads(acc[qi * r:(qi + 1) * r]).astype(BF16)


def _attn_prompt(q, k, v, tq):
    bsz, t, d = q.shape
    pairs = d // LANES
    return pl.pallas_call(
        _attn_prompt_body,
        grid=(bsz, pairs, t // tq),
        in_specs=[
            pl.BlockSpec((1, tq, LANES), lambda b, p, i: (b, i, p)),
            pl.BlockSpec((1, t, LANES), lambda b, p, i: (b, 0, p)),
            pl.BlockSpec((1, t, LANES), lambda b, p, i: (b, 0, p)),
        ],
        out_specs=pl.BlockSpec((1, tq, LANES), lambda b, p, i: (b, i, p)),
        out_shape=jax.ShapeDtypeStruct((bsz, t, d), BF16),
        compiler_params=_params(("arbitrary", "arbitrary", "arbitrary")),
        name="attn_prompt",
    )(q, k, v)


def _attn_sample_body(q_ref, kn_ref, vn_ref, ck_ref, cv_ref, o_ref):
    tq = q_ref.shape[1]
    past_blocks = ck_ref.shape[1] // KEY_BLOCK
    qs = _stack_heads(q_ref[0])
    row = _row_in_head(2 * tq, KEY_BLOCK)
    col = lax.broadcasted_iota(jnp.int32, (2 * tq, KEY_BLOCK), 1)
    acc = jnp.zeros((2 * tq, LANES), F32)
    ll = jnp.zeros((2 * tq, 1), F32)
    suffix = _suffix_matrix()
    acc, ll = _sb_block(qs, kn_ref[0].astype(BF16), vn_ref[0].astype(BF16), acc, ll, col < row, suffix)
    acc = _sb_older_blocks(qs, ck_ref, cv_ref, past_blocks - 1, acc, ll, suffix)
    o_ref[0] = _unstack_heads(acc).astype(BF16)


def _attn_sample(q, k_new, v_new, cache_k, cache_v):
    bsz, t, d = q.shape
    past = cache_k.shape[1]
    pairs = d // LANES
    return pl.pallas_call(
        _attn_sample_body,
        grid=(bsz, pairs),
        in_specs=[
            pl.BlockSpec((1, t, LANES), lambda b, p: (b, 0, p)),
            pl.BlockSpec((1, KEY_BLOCK, LANES), lambda b, p: (b, 0, p)),
            pl.BlockSpec((1, KEY_BLOCK, LANES), lambda b, p: (b, 0, p)),
            pl.BlockSpec((1, past, LANES), lambda b, p: (b, 0, p)),
            pl.BlockSpec((1, past, LANES), lambda b, p: (b, 0, p)),
        ],
        out_specs=pl.BlockSpec((1, t, LANES), lambda b, p: (b, 0, p)),
        out_shape=jax.ShapeDtypeStruct((bsz, t, d), BF16),
        compiler_params=_params(("arbitrary", "arbitrary")),
        name="attn_sample",
    )(q, k_new, v_new, cache_k, cache_v)


def _first_argmax(cur, iota_f, n):
    m = jnp.max(cur, axis=0, keepdims=True)
    i = jnp.min(jnp.where(cur == m, iota_f, float(n)), axis=0, keepdims=True)
    return m, i


def _store_row_tiles(ref, val):
    for s in range(val.shape[1] // LANES):
        ref[:, s, :] = val[:, s * LANES:(s + 1) * LANES]


def _post_attn_body(o_ref, gc_ref, sa_ref, x_ref, cnt_ref, lng_ref, lnb_ref, wao_ref, wo_ref, l1g_ref,
                    l1b_ref, wrh_ref, wrl_ref, rb_ref, wsgu_ref, wsd_ref,
                    x1_ref, part_ref, idx_ref, wts_ref, rank_ref, cnt_out_ref, run):
    tm = x_ref.shape[0]
    ff = wsd_ref.shape[0]

    @pl.when(pl.program_id(0) == 0)
    def _():
        run[...] = cnt_ref[...]

    h = _layer_norm(x_ref[...], lng_ref[...], lnb_ref[...])
    att = _dot(o_ref[...], wao_ref[...])
    merged = gc_ref[...].astype(F32) + sa_ref[...].astype(F32) * att
    mixed = _dot(merged.astype(BF16), wo_ref[...])
    x1 = _layer_norm(DEEPNORM_ALPHA * h + mixed, l1g_ref[...], l1b_ref[...])
    _store_row_tiles(x1_ref, x1)

    x1h = x1.astype(BF16)
    gu = _dot(x1h, wsgu_ref[...])
    shared = _dot((_silu(gu[:, :ff]) * gu[:, ff:]).astype(BF16), wsd_ref[...])
    _store_row_tiles(part_ref, DEEPNORM_ALPHA * x1 + shared)

    x1l = (x1 - x1h.astype(F32)).astype(BF16)
    logits = _dot_nt(wrh_ref[...], x1h) + _dot_nt(wrl_ref[...], x1h) + _dot_nt(wrh_ref[...], x1l)
    scores = _sigmoid(logits)
    choice = scores + rb_ref[...]
    neg = -jnp.inf

    giota = lax.broadcasted_iota(jnp.int32, (GROUP_SIZE, tm), 0).astype(F32)
    gs = []
    for g in range(N_GROUPS):
        blk = choice[g * GROUP_SIZE:(g + 1) * GROUP_SIZE, :]
        m1, i1 = _first_argmax(blk, giota, GROUP_SIZE)
        m2 = jnp.max(jnp.where(giota == i1, neg, blk), axis=0, keepdims=True)
        gs.append(m1 + m2)
    gscore = jnp.concatenate(gs, axis=0)

    g8 = lax.broadcasted_iota(jnp.int32, (N_GROUPS, tm), 0).astype(F32)
    gsel = jnp.zeros((N_GROUPS, tm), F32)
    cur = gscore
    for _ in range(TOPK_GROUPS):
        _, i = _first_argmax(cur, g8, N_GROUPS)
        hit = g8 == i
        gsel = jnp.where(hit, 1.0, gsel)
        cur = jnp.where(hit, neg, cur)
    emask = jnp.concatenate(
        [jnp.broadcast_to(gsel[g:g + 1, :], (GROUP_SIZE, tm)) for g in range(N_GROUPS)], axis=0)

    eiota = lax.broadcasted_iota(jnp.int32, (N_EXPERTS, tm), 0).astype(F32)
    cur = jnp.where(emask > 0.0, choice, neg)
    hits, sel_w = [], []
    for r in range(TOP_K):
        _, i = _first_argmax(cur, eiota, N_EXPERTS)
        hit = eiota == i
        hits.append(hit)
        sel_w.append(jnp.sum(jnp.where(hit, scores, 0.0), axis=0, keepdims=True))
        cur = jnp.where(hit, neg, cur)
        idx_ref[r:r + 1, :] = i.astype(jnp.int32)
    wsum = sel_w[0]
    for r in range(1, TOP_K):
        wsum = wsum + sel_w[r]
    for r in range(TOP_K):
        wts_ref[r:r + 1, :] = sel_w[r] / wsum * ROUTED_SCALE

    picked = jnp.zeros((N_EXPERTS, tm), F32)
    for r in range(TOP_K):
        picked = jnp.where(hits[r], 1.0, picked)
    picked_b = picked.astype(BF16)
    trow = lax.broadcasted_iota(jnp.int32, (tm, tm), 0)
    tcol = lax.broadcasted_iota(jnp.int32, (tm, tm), 1)
    earlier = jnp.where(trow < tcol, 1.0, 0.0).astype(BF16)
    before = _dot(picked_b, earlier) + run[:, 0:1]
    for r in range(TOP_K):
        rank_ref[r:r + 1, :] = jnp.sum(jnp.where(hits[r], before, 0.0), axis=0, keepdims=True).astype(jnp.int32)
    run[...] = run[...] + _dot(picked_b, jnp.ones((tm, LANES), BF16))
    cnt_out_ref[...] = run[...]


def _post_attn(o, gc, sa, x, cnt_in, ln_g, ln_b, w_ao, w_o, l1g, l1b, wr_hi, wr_lo, r_bias, ws_gu, ws_d, tm):
    n, d = x.shape
    ff = ws_d.shape[0]
    const = lambda shape: pl.BlockSpec(shape, lambda i: (0,) * len(shape))
    tile = lambda: pl.BlockSpec((tm, d), lambda i: (i, 0))
    small = lambda: pl.BlockSpec((TOP_K, tm), lambda i: (0, i))
    row_tiles = lambda: pl.BlockSpec((tm, d // LANES, LANES), lambda i: (i, 0, 0))
    return pl.pallas_call(
        _post_attn_body,
        grid=(n // tm,),
        in_specs=[tile(), tile(), tile(), tile(), const((N_EXPERTS, LANES)),
                  const((1, d)), const((1, d)), const((d, d)), const((d, d)), const((1, d)), const((1, d)),
                  const((N_EXPERTS, d)), const((N_EXPERTS, d)), const((N_EXPERTS, 1)),
                  const((d, 2 * ff)), const((ff, d))],
        out_specs=[row_tiles(), row_tiles(), small(), small(), small(), const((N_EXPERTS, LANES))],
        out_shape=[
            jax.ShapeDtypeStruct((n, d // LANES, LANES), F32),
            jax.ShapeDtypeStruct((n, d // LANES, LANES), F32),
            jax.ShapeDtypeStruct((TOP_K, n), jnp.int32),
            jax.ShapeDtypeStruct((TOP_K, n), F32),
            jax.ShapeDtypeStruct((TOP_K, n), jnp.int32),
            jax.ShapeDtypeStruct((N_EXPERTS, LANES), F32),
        ],
        scratch_shapes=[pltpu.VMEM((N_EXPERTS, LANES), F32)],
        compiler_params=_params(("arbitrary",)),
        name="post_attn",
    )(o, gc, sa, x, cnt_in, ln_g, ln_b, w_ao, w_o, l1g, l1b, wr_hi, wr_lo, r_bias, ws_gu, ws_d)


def _dest_body(idx_ref, rank_ref, rs_ref, dest_ref):
    tm = idx_ref.shape[1]
    eiota = lax.broadcasted_iota(jnp.int32, (N_EXPERTS, tm), 0)
    rs = rs_ref[...]
    for r in range(TOP_K):
        base = jnp.sum(jnp.where(eiota == idx_ref[r:r + 1, :], rs, 0.0), axis=0, keepdims=True)
        dest_ref[r:r + 1, :] = base.astype(jnp.int32) + rank_ref[r:r + 1, :]


def _dest(idx, rank, row_start_f, tm):
    n = idx.shape[1]
    small = lambda: pl.BlockSpec((TOP_K, tm), lambda i: (0, i))
    return pl.pallas_call(
        _dest_body,
        grid=(n // tm,),
        in_specs=[small(), small(), pl.BlockSpec((N_EXPERTS, 1), lambda i: (0, 0))],
        out_specs=small(),
        out_shape=jax.ShapeDtypeStruct((TOP_K, n), jnp.int32),
        compiler_params=_params(("arbitrary",)),
        name="dest",
    )(idx, rank, row_start_f)


def _slot_index_copies(src_hbm, col0, width, dst_smem, s, sem):
    w = dst_smem.shape[1] // TOP_K
    return [pltpu.make_async_copy(src_hbm.at[pl.ds(j, 1), pl.ds(col0 + q * LANES, LANES)],
                                  dst_smem.at[s, pl.ds(j * w + q, 1), :], sem)
            for j in range(TOP_K) for q in range(width // LANES)]


def _dispatch_body(meta_ref, lc_ref, dp_hbm, ds_hbm, xp_ref, xsm_ref, xs_hbm, idx_smem, zbuf, isem, ssem, zsem):
    i = pl.program_id(0)
    n = pl.num_programs(0)
    tm = xp_ref.shape[0]
    n_s = xsm_ref.shape[0]
    total_chunks = xs_hbm.shape[0] // GATHER_CHUNK

    def idx_copies(tile, fn):
        @pl.when(tile < n - 1)
        def _():
            for c in _slot_index_copies(dp_hbm, tile * tm, tm, idx_smem, 0, isem):
                fn(c)

        @pl.when(tile == n - 1)
        def _():
            for c in _slot_index_copies(ds_hbm, 0, n_s, idx_smem, 0, isem):
                fn(c)

    @pl.when(i == 0)
    def _():
        idx_copies(0, lambda c: c.start())

    idx_copies(i, lambda c: c.wait())

    def scatter(src_ref):
        cnt = src_ref.shape[0]
        for j in range(TOP_K):
            for q in range(cnt // LANES):
                idx_row = idx_smem.at[0, j * (tm // LANES) + q]

                def body(p, carry, idx_row=idx_row, q=q):
                    for u in range(DMA_THREADS):
                        t = DMA_THREADS * p + u
                        row = idx_row[t]
                        pltpu.make_async_copy(src_ref.at[q * LANES + t], xs_hbm.at[row], ssem).start(priority=u)
                    return carry
                lax.fori_loop(0, LANES // DMA_THREADS, body, 0, unroll=4)

        @pl.when(i + 1 < n)
        def _():
            idx_copies(i + 1, lambda c: c.start())

        for j in range(TOP_K):
            pltpu.make_async_copy(src_ref, xs_hbm.at[pl.ds(0, cnt)], ssem).wait()

    @pl.when(i == 0)
    def _():
        zbuf[...] = jnp.zeros(zbuf.shape, F32)

        def zero_chunk(c):
            return pltpu.make_async_copy(zbuf, xs_hbm.at[pl.ds(c * GATHER_CHUNK, GATHER_CHUNK)], zsem)

        def each_expert(fn):
            def body(e, carry):
                @pl.when(lc_ref[e] >= 0)
                def _():
                    fn(lc_ref[e])
                return carry
            lax.fori_loop(0, N_EXPERTS, body, 0)

        def each_tail(fn):
            def body(c, carry):
                fn(c)
                return carry
            lax.fori_loop(meta_ref[0], total_chunks, body, 0)

        each_expert(lambda c: zero_chunk(c).start())
        each_tail(lambda c: zero_chunk(c).start())
        each_expert(lambda c: zero_chunk(c).wait())
        each_tail(lambda c: zero_chunk(c).wait())

    @pl.when(i < n - 1)
    def _():
        scatter(xp_ref)

    @pl.when(i == n - 1)
    def _():
        scatter(xsm_ref)


def _dispatch(meta, last_chunk, dest_p, dest_s, x1p, x1s, n_rows, tm):
    n_p, sub, lanes = x1p.shape
    n_s = x1s.shape[0]
    n_pt = n_p // tm
    grid_spec = pltpu.PrefetchScalarGridSpec(
        num_scalar_prefetch=2,
        grid=(n_pt + 1,),
        in_specs=[
            pl.BlockSpec(memory_space=pl.ANY),
            pl.BlockSpec(memory_space=pl.ANY),
            pl.BlockSpec((tm, sub, lanes), lambda i, *_: (jnp.minimum(i, n_pt - 1), 0, 0)),
            pl.BlockSpec((n_s, sub, lanes), lambda i, *_: (0, 0, 0)),
        ],
        out_specs=pl.BlockSpec(memory_space=pl.ANY),
        scratch_shapes=[
            pltpu.SMEM((1, TOP_K * tm // LANES, LANES), jnp.int32),
            pltpu.VMEM((GATHER_CHUNK, sub, lanes), F32),
            pltpu.SemaphoreType.DMA(()),
            pltpu.SemaphoreType.DMA(()),
            pltpu.SemaphoreType.DMA(()),
        ],
    )
    return pl.pallas_call(
        _dispatch_body,
        grid_spec=grid_spec,
        out_shape=jax.ShapeDtypeStruct((n_rows, sub, lanes), F32),
        compiler_params=_dma_params(("arbitrary",)),
        name="dispatch",
    )(meta, last_chunk, dest_p, dest_s, x1p, x1s)


def _experts_body(be_ref, bx_ref, bn_ref, bz_ref, xs_hbm, wg_ref, wu_ref, wd_ref, ys_hbm,
                  xbuf, obuf, gsem, osem, wgu_bf, wd_bf):
    b = pl.program_id(0)
    nb = pl.num_programs(0)
    slot = b % 2
    ff = wd_ref.shape[1]

    def in_copies(blk, s, nch):
        rows = pl.ds(bx_ref[blk] * GATHER_CHUNK, nch * GATHER_CHUNK)
        return [pltpu.make_async_copy(xs_hbm.at[rows, c, :],
                                      xbuf.at[s, pl.ds(0, nch * GATHER_CHUNK), pl.ds(c * LANES, LANES)],
                                      gsem.at[s]) for c in range(xs_hbm.shape[1])]

    def out_copies(blk, s, nch):
        rows = pl.ds(bx_ref[blk] * GATHER_CHUNK, nch * GATHER_CHUNK)
        return [pltpu.make_async_copy(obuf.at[s, pl.ds(0, nch * GATHER_CHUNK), pl.ds(c * LANES, LANES)],
                                      ys_hbm.at[rows, c, :], osem.at[s]) for c in range(ys_hbm.shape[1])]

    def for_block(count, copies, fn):
        for nch in range(1, BLOCK_CHUNKS + 1):
            @pl.when(count == nch)
            def _():
                for c in copies(nch):
                    fn(c)

    def n_in(blk):
        return jnp.where(bz_ref[blk] == 0, bn_ref[blk], 0)

    @pl.when(b == 0)
    def _():
        for_block(n_in(0), lambda k: in_copies(0, 0, k), lambda c: c.start())

    @pl.when(b + 1 < nb)
    def _():
        for_block(n_in(b + 1), lambda k: in_copies(b + 1, 1 - slot, k), lambda c: c.start())

    @pl.when(b >= 2)
    def _():
        for_block(bn_ref[b - 2], lambda k: out_copies(b - 2, slot, k), lambda c: c.wait())

    for_block(n_in(b), lambda k: in_copies(b, slot, k), lambda c: c.wait())

    e_now = be_ref[b]
    e_prev = be_ref[jnp.maximum(b - 1, 0)]

    @pl.when(jnp.logical_or(b == 0, e_now != e_prev))
    def _():
        wgu_bf[:, 0:ff] = wg_ref[0].astype(BF16)
        wgu_bf[:, ff:2 * ff] = wu_ref[0].astype(BF16)
        wd_bf[...] = wd_ref[0].astype(BF16)

    for nch in range(1, BLOCK_CHUNKS + 1):
        @pl.when(n_in(b) == nch)
        def _():
            rows = nch * GATHER_CHUNK
            gu = _dot(xbuf[slot, 0:rows, :].astype(BF16), wgu_bf[...])
            hid = _silu(gu[:, :ff]) * gu[:, ff:]
            obuf[slot, 0:rows, :] = _dot(hid.astype(BF16), wd_bf[...])

    @pl.when(bz_ref[b] != 0)
    def _():
        obuf[slot] = jnp.zeros(obuf.shape[1:], F32)

    for_block(bn_ref[b], lambda k: out_copies(b, slot, k), lambda c: c.start())

    @pl.when(b == nb - 1)
    def _():
        for_block(bn_ref[b], lambda k: out_copies(b, slot, k), lambda c: c.wait())

        @pl.when(nb > 1)
        def _():
            for_block(bn_ref[b - 1], lambda k: out_copies(b - 1, 1 - slot, k), lambda c: c.wait())


def _experts(block_e, block_x, block_n, block_z, xs, we_gate, we_up, we_down):
    nb = block_e.shape[0]
    n_rows = xs.shape[0]
    d = we_gate.shape[1]
    ff = we_gate.shape[2]
    rows = BLOCK_CHUNKS * GATHER_CHUNK
    grid_spec = pltpu.PrefetchScalarGridSpec(
        num_scalar_prefetch=4,
        grid=(nb,),
        in_specs=[
            pl.BlockSpec(memory_space=pl.ANY),
            pl.BlockSpec((1, d, ff), lambda b, be, *_: (be[b], 0, 0)),
            pl.BlockSpec((1, d, ff), lambda b, be, *_: (be[b], 0, 0)),
            pl.BlockSpec((1, ff, d), lambda b, be, *_: (be[b], 0, 0)),
        ],
        out_specs=pl.BlockSpec(memory_space=pl.ANY),
        scratch_shapes=[
            pltpu.VMEM((2, rows, d), F32),
            pltpu.VMEM((2, rows, d), F32),
            pltpu.SemaphoreType.DMA((2,)),
            pltpu.SemaphoreType.DMA((2,)),
            pltpu.VMEM((d, 2 * ff), BF16),
            pltpu.VMEM((ff, d), BF16),
        ],
    )
    return pl.pallas_call(
        _experts_body,
        grid_spec=grid_spec,
        out_shape=jax.ShapeDtypeStruct(xs.shape, F32),
        compiler_params=_dma_params(("arbitrary",)),
        name="experts",
    )(block_e, block_x, block_n, block_z, xs, we_gate, we_up, we_down)


def _combine_body(dt_hbm, wt_hbm, ys_hbm, part_ref, g_ref, b_ref, y_hbm, idx_smem, w_smem, gbuf, ybuf, isem, wsem,
                  gsem, osem):
    i = pl.program_id(0)
    n = pl.num_programs(0)
    tm = gbuf.shape[2]
    per_step = part_ref.shape[0] // tm
    more = i + 1 < n

    def idx_copies(tile, s):
        return _slot_index_copies(dt_hbm, tile * tm, tm, idx_smem, s, isem.at[s])

    def w_copies(tile, s):
        return _slot_index_copies(wt_hbm, tile * tm, tm, w_smem, s, wsem.at[s])

    def start(copies):
        for c in copies:
            c.start()

    def wait(copies):
        for c in copies:
            c.wait()

    def issue_rows(s):
        for j in range(TOP_K):
            def body(p, carry):
                for u in range(DMA_THREADS):
                    t = DMA_THREADS * p + u
                    row = idx_smem.at[s, j][t]
                    pltpu.make_async_copy(ys_hbm.at[row], gbuf.at[s, j, t], gsem.at[s]).start(priority=u)
                return carry
            lax.fori_loop(0, tm // DMA_THREADS, body, 0, unroll=4)

    def out_copies(s, tile):
        return [pltpu.make_async_copy(ybuf.at[s, pl.ds(0, tm), c, :],
                                      y_hbm.at[pl.ds(tile * tm, tm), pl.ds(c * LANES, LANES)], osem.at[s])
                for c in range(ybuf.shape[2])]

    def finish_tile(s):
        for j in range(TOP_K):
            pltpu.make_async_copy(ys_hbm.at[pl.ds(0, tm)], gbuf.at[s, j], gsem.at[s]).wait()
        wait(w_copies(0, s))

        @pl.when(i > 0)
        def _():
            wait(out_copies(s, 0))
        base = s * tm

        def token(t, carry):
            f = part_ref[base + t]
            for j in range(TOP_K):
                f = f + w_smem.at[s, j][t] * gbuf[s, j, t]
            ybuf[s, t] = f
            return carry
        lax.fori_loop(0, tm, token, 0, unroll=8)

        f = ybuf[s]
        inv_d = 1.0 / (f.shape[1] * f.shape[2])
        tile_sum = lambda a: jnp.sum(jnp.sum(a, axis=2, keepdims=True), axis=1, keepdims=True)
        fc = f - tile_sum(f) * inv_d
        var = tile_sum(fc * fc) * inv_d
        ybuf[s] = fc * lax.rsqrt(var + LN_EPS) * g_ref[...] + b_ref[...]
        start(out_copies(s, first + s))

    first = i * per_step

    @pl.when(i == 0)
    def _():
        start(idx_copies(0, 0))
        start(w_copies(0, 0))
        wait(idx_copies(0, 0))
        issue_rows(0)
        if per_step == 2:
            start(idx_copies(1, 1))
            start(w_copies(1, 1))

    if per_step == 1:
        finish_tile(0)

        @pl.when(more)
        def _():
            start(idx_copies(first + 1, 0))
            start(w_copies(first + 1, 0))
            wait(idx_copies(first + 1, 0))
            issue_rows(0)
    else:
        wait(idx_copies(first + 1, 1))
        issue_rows(1)

        @pl.when(more)
        def _():
            start(idx_copies(first + 2, 0))

        finish_tile(0)

        @pl.when(more)
        def _():
            start(w_copies(first + 2, 0))
            wait(idx_copies(first + 2, 0))
            issue_rows(0)
            start(idx_copies(first + 3, 1))

        finish_tile(1)

        @pl.when(more)
        def _():
            start(w_copies(first + 3, 1))

    @pl.when(jnp.logical_not(more))
    def _():
        for s in range(per_step):
            wait(out_copies(s, 0))


def _combine(dest, wts, ys, part, ln_g, ln_b, tm, per_step):
    n, sub, lanes = part.shape
    assert tm == LANES
    rows = tm * per_step
    return pl.pallas_call(
        _combine_body,
        grid=(n // rows,),
        in_specs=[
            pl.BlockSpec(memory_space=pl.ANY),
            pl.BlockSpec(memory_space=pl.ANY),
            pl.BlockSpec(memory_space=pl.ANY),
            pl.BlockSpec((rows, sub, lanes), lambda i: (i, 0, 0)),
            pl.BlockSpec((sub, lanes), lambda i: (0, 0)),
            pl.BlockSpec((sub, lanes), lambda i: (0, 0)),
        ],
        out_specs=pl.BlockSpec(memory_space=pl.ANY),
        out_shape=jax.ShapeDtypeStruct((n, sub * lanes), F32),
        scratch_shapes=[
            pltpu.SMEM((2, TOP_K, LANES), jnp.int32),
            pltpu.SMEM((2, TOP_K, LANES), F32),
            pltpu.VMEM((2, TOP_K, tm, sub, lanes), F32),
            pltpu.VMEM((2, tm, sub, lanes), F32),
            pltpu.SemaphoreType.DMA((2,)),
            pltpu.SemaphoreType.DMA((2,)),
            pltpu.SemaphoreType.DMA((2,)),
            pltpu.SemaphoreType.DMA((2,)),
        ],
        compiler_params=_dma_params(("arbitrary",)),
        name="combine",
    )(dest, wts, ys, part, ln_g.reshape(sub, lanes), ln_b.reshape(sub, lanes))


def kernel(x_prompt, x_sample, cache_k, cache_v, state_conv, ln_in_g, ln_in_b, w_in, b_in, conv_w, conv_b,
           conv_ln_g, conv_ln_b, w_conv_out, w_attn_out, w_out, ln1_g, ln1_b, w_router, router_bias,
           we_gate, we_up, we_down, ws_gate, ws_up, ws_down, ln2_g, ln2_b):
    bp, tp, d = x_prompt.shape
    bs, ts, _ = x_sample.shape
    past = cache_k.shape[2]
    assert w_in.shape[0] == DEPTH and d == N_HEADS * HEAD_DIM and conv_w.shape[2] == d
    assert tp % 1024 == 0 and ts <= KEY_BLOCK and ts % 16 == 0 and past % KEY_BLOCK == 0
    n_p, n_s = bp * tp, bs * ts
    assert n_s % LANES == 0

    row = lambda a: a.reshape(1, -1)
    lng, lnb = row(ln_in_g), row(ln_in_b)
    w_in_b = w_in[0].astype(BF16)
    w_co_b = w_conv_out[0].astype(BF16)
    conv_args = (lng, lnb, w_in_b, b_in, conv_w[0], conv_b, conv_ln_g, conv_ln_b, w_co_b)

    qp, kp, vp, gcp, sap, csp = _proj_conv(x_prompt, jnp.zeros((bp, CONV_WIDTH - 1, d), F32), *conv_args, tm=256)
    op = _attn_prompt(qp, kp, vp, tq=1024)

    qs, ks, vs, gcs, sas, css = _proj_conv(x_sample, state_conv[0], *conv_args, tm=ts)
    pad_new = lambda a: jnp.pad(a, ((0, 0), (0, KEY_BLOCK - ts), (0, 0)))
    os_ = _attn_sample(qs, pad_new(ks), pad_new(vs), cache_k[0].reshape(bs, past, d), cache_v[0].reshape(bs, past, d))

    wr = w_router[0].T
    wr_hi = wr.astype(BF16)
    wr_lo = (wr - wr_hi.astype(F32)).astype(BF16)
    post_args = (lng, lnb, w_attn_out[0].astype(BF16), w_out[0].astype(BF16), ln1_g, ln1_b, wr_hi, wr_lo,
                 router_bias.reshape(N_EXPERTS, 1),
                 jnp.concatenate([ws_gate[0], ws_up[0]], axis=1).astype(BF16), ws_down[0].astype(BF16))
    flat = lambda a, n: a.reshape(n, d)
    zero_cnt = jnp.zeros((N_EXPERTS, LANES), F32)
    x1p, partp, idxp, wtp, rankp, cnt_p = _post_attn(
        flat(op, n_p), flat(gcp, n_p), flat(sap, n_p), flat(x_prompt, n_p), zero_cnt, *post_args, tm=512)
    x1s, parts, idxs, wts, ranks, cnt = _post_attn(
        flat(os_, n_s), flat(gcs, n_s), flat(sas, n_s), flat(x_sample, n_s), cnt_p, *post_args, tm=n_s)

    i32 = jnp.int32
    n_tok = n_p + n_s
    total_chunks = (n_tok * TOP_K + N_EXPERTS * (GATHER_CHUNK - 1) + GATHER_CHUNK - 1) // GATHER_CHUNK
    n_blocks = total_chunks // BLOCK_CHUNKS + N_EXPERTS
    counts = cnt[:, 0].astype(i32)
    chunks = (counts + GATHER_CHUNK - 1) // GATHER_CHUNK
    chunk_end = jnp.cumsum(chunks)
    chunk_start = chunk_end - chunks
    row_start = chunk_start * GATHER_CHUNK
    used_chunks = chunk_end[-1]
    blocks = (chunks + BLOCK_CHUNKS - 1) // BLOCK_CHUNKS
    blk_end = jnp.cumsum(blocks)
    blk_start = blk_end - blocks
    bid = jnp.arange(n_blocks, dtype=i32)
    block_e = jnp.minimum(jnp.sum(bid[:, None] >= blk_end[None, :], axis=1), N_EXPERTS - 1).astype(i32)
    k_in_e = bid - blk_start[block_e]
    is_tail = bid >= blk_end[-1]
    tail_x = used_chunks + BLOCK_CHUNKS * (bid - blk_end[-1])
    block_x = jnp.where(is_tail, tail_x, chunk_start[block_e] + BLOCK_CHUNKS * k_in_e)
    block_n = jnp.where(is_tail, total_chunks - tail_x, chunks[block_e] - BLOCK_CHUNKS * k_in_e)
    block_n = jnp.clip(block_n, 0, BLOCK_CHUNKS).astype(i32)
    block_x = jnp.where(block_n > 0, block_x, 0).astype(i32)
    block_z = is_tail.astype(i32)
    meta = jnp.stack([used_chunks, used_chunks]).astype(i32)
    last_chunk = jnp.where(chunks > 0, chunk_end - 1, -1).astype(i32)

    rs_f = row_start.astype(F32).reshape(N_EXPERTS, 1)
    dest_p = _dest(idxp, rankp, rs_f, tm=1024)
    dest_s = _dest(idxs, ranks, rs_f, tm=n_s)

    xs = _dispatch(meta, last_chunk, dest_p, dest_s, x1p, x1s, total_chunks * GATHER_CHUNK, tm=1024)
    ys = _experts(block_e, block_x, block_n, block_z, xs, we_gate[0], we_up[0], we_down[0])

    yp = _combine(dest_p, wtp, ys, partp, ln2_g, ln2_b, tm=128, per_step=2)
    ysm = _combine(dest_s, wts, ys, parts, ln2_g, ln2_b, tm=n_s, per_step=1)

    heads = lambda a, b, t: a.reshape(1, b, t, N_HEADS, HEAD_DIM)
    return (yp.reshape(bp, tp, d), ysm.reshape(bs, ts, d),
            heads(kp, bp, tp), heads(vp, bp, tp), csp[None],
            heads(ks, bs, ts), heads(vs, bs, ts), css[None])
```
